```python
import math
import jax, jax.numpy as jnp
from jax import lax
import numpy as np

D_MODEL = 1024
BATCH = 4
SEQ = 4096
DEPTH = 2
DEC_BATCH = 32
DEC_SEQ = 4
PAST_LEN = 8192
PAGE_SIZE = 128

HEAD_DIM = 64
H_A = 8
H_B = 8
N_ATTN_HEADS = H_A + H_B
D_A = H_A * HEAD_DIM
D_B = H_B * HEAD_DIM
MOBA_BLOCK = 256
MOBA_TOPK = 3
DSA_TOPK_MAX = 256
H_IDX = 8
D_IDX = 64
CONV_WIDTH = 31
D_CONV = D_MODEL
N_BUCKETS = 32
MAX_DISTANCE = 128
Q_BLOCK = 128
RMS_EPS = 1e-6
LN_EPS = 1e-5
N_ATTN_LAYERS = (DEPTH + 1) // 2
N_CONV_LAYERS = DEPTH // 2
D_IN_A = 4 * D_A + 4 * D_B + H_IDX * D_IDX + D_IDX + H_IDX
D_IN_C = 3 * D_CONV

kernel_name = 'hybrid_moba_dsa_conformer_step'


def rms_norm(x, g):
    xf = x.astype(jnp.float32)
    y = xf * lax.rsqrt(jnp.mean(xf * xf, axis=-1, keepdims=True) + RMS_EPS)
    return (y * g.astype(jnp.float32)).astype(x.dtype)


def layer_norm(x, g, b):
    xf = x.astype(jnp.float32)
    mu = jnp.mean(xf, axis=-1, keepdims=True)
    var = jnp.mean(jnp.square(xf - mu), axis=-1, keepdims=True)
    y = (xf - mu) * lax.rsqrt(var + LN_EPS)
    return (y * g.astype(jnp.float32) + b.astype(jnp.float32)).astype(x.dtype)


def t5_bucket(n):
    exact = N_BUCKETS // 2
    nf = jnp.maximum(n, 1).astype(jnp.float32)
    large = exact + (jnp.log(nf / exact) / math.log(MAX_DISTANCE / exact) * (N_BUCKETS - exact)).astype(jnp.int32)
    large = jnp.minimum(large, N_BUCKETS - 1)
    return jnp.where(n < exact, n, large)


def pad_blocks(t):
    L = t.shape[0]
    Lp = -(-L // MOBA_BLOCK) * MOBA_BLOCK
    return jnp.pad(t, ((0, Lp - L),) + ((0, 0),) * (t.ndim - 1))


def moba_attend(q, k, v, q_pos, table):
    Q, H = q.shape[0], q.shape[1]
    n_blk = k.shape[0] // MOBA_BLOCK
    kb = k.reshape(n_blk, MOBA_BLOCK, H, HEAD_DIM).transpose(2, 0, 1, 3)
    vb = v.reshape(n_blk, MOBA_BLOCK, H, HEAD_DIM).transpose(2, 0, 1, 3)
    k_mean = jnp.mean(kb.astype(jnp.float32), axis=2)
    own = q_pos // MOBA_BLOCK
    gate = jnp.einsum('qhd,hnd->qhn', q.astype(jnp.float32), k_mean)
    gate = jnp.where(jnp.arange(n_blk)[None, None, :] < own[:, None, None], gate, -jnp.inf)
    n_sel = min(MOBA_TOPK, n_blk)
    _, top_idx = lax.top_k(gate, n_sel)
    top_valid = top_idx < own[:, None, None]
    idx = jnp.concatenate([top_idx, jnp.broadcast_to(own[:, None, None], (Q, H, 1))], axis=-1)
    valid = jnp.concatenate([top_valid, jnp.ones((Q, H, 1), dtype=bool)], axis=-1)
    h_ar = jnp.arange(H)[None, :, None]
    k_sel = kb[h_ar, idx]
    v_sel = vb[h_ar, idx]
    k_pos = idx[..., None] * MOBA_BLOCK + jnp.arange(MOBA_BLOCK)
    dist = q_pos[:, None, None, None] - k_pos
    bias = table[t5_bucket(jnp.maximum(dist, 0)), jnp.arange(H)[None, :, None, None]].astype(jnp.float32)
    logits = jnp.einsum('qhd,qhnbd->qhnb', q, k_sel, preferred_element_type=jnp.float32) * (HEAD_DIM ** -0.5) + bias
    logits = jnp.where(valid[..., None] & (dist >= 0), logits, -jnp.inf)
    p = jax.nn.softmax(logits.reshape(Q, H, -1), axis=-1).reshape(logits.shape)
    return jnp.einsum('qhnb,qhnbd->qhd', p.astype(v.dtype), v_sel)


def dsa_attend(q, k, v, qi, ki, wi, q_pos, table, topk):
    L = k.shape[0]
    score = jnp.einsum('qid,sd->qis', qi, ki, preferred_element_type=jnp.float32) * (D_IDX ** -0.5)
    index = jnp.einsum('qi,qis->qs', wi.astype(jnp.float32) * (H_IDX ** -0.5), jax.nn.relu(score))
    index = jnp.where(jnp.arange(L)[None, :] <= q_pos[:, None], index, -jnp.inf)
    _, idx = lax.top_k(index, topk)
    dist = q_pos[:, None] - idx
    k_sel = k[idx]
    v_sel = v[idx]
    bias = jnp.moveaxis(table[t5_bucket(jnp.maximum(dist, 0))], -1, 1).astype(jnp.float32)
    logits = jnp.einsum('qhd,qkhd->qhk', q, k_sel, preferred_element_type=jnp.float32) * (HEAD_DIM ** -0.5) + bias
    logits = jnp.where((dist >= 0)[:, None, :], logits, -jnp.inf)
    p = jax.nn.softmax(logits, axis=-1)
    return jnp.einsum('qhk,qkhd->qhd', p.astype(v.dtype), v_sel)


def attn_project(h, w_in, qn_a, kn_a, qn_b, kn_b):
    B, T = h.shape[0], h.shape[1]
    z = jnp.einsum('btd,de->bte', h, w_in)
    sizes = [D_A] * 4 + [D_B] * 4 + [H_IDX * D_IDX, D_IDX, H_IDX]
    splits = np.cumsum(sizes)[:-1].tolist()
    qa, ka, va, ga, qb, kb, vb, gb, qi, ki, wi = jnp.split(z, splits, axis=-1)
    qa = rms_norm(qa.reshape(B, T, H_A, HEAD_DIM), qn_a)
    ka = rms_norm(ka.reshape(B, T, H_A, HEAD_DIM), kn_a)
    va = va.reshape(B, T, H_A, HEAD_DIM)
    qb = rms_norm(qb.reshape(B, T, H_B, HEAD_DIM), qn_b)
    kb = rms_norm(kb.reshape(B, T, H_B, HEAD_DIM), kn_b)
    vb = vb.reshape(B, T, H_B, HEAD_DIM)
    qi = qi.reshape(B, T, H_IDX, D_IDX)
    return qa, ka, va, ga, qb, kb, vb, gb, qi, ki, wi


def attn_merge(oa, ga, ob, gb, w_out):
    B, T = ga.shape[0], ga.shape[1]
    o = jnp.concatenate([oa.reshape(B, T, D_A) * jax.nn.silu(ga), ob.reshape(B, T, D_B) * jax.nn.silu(gb)], axis=-1)
    return jnp.einsum('bte,ed->btd', o, w_out)


def attn_prompt(h, w_in, qn_a, kn_a, qn_b, kn_b, w_out, table):
    T = h.shape[1]
    qa, ka, va, ga, qb, kb, vb, gb, qi, ki, wi = attn_project(h, w_in, qn_a, kn_a, qn_b, kn_b)
    pos = jnp.arange(T, dtype=jnp.int32)
    n_qb = T // Q_BLOCK
    topk = min(DSA_TOPK_MAX, T // 4)
    tab_a, tab_b = table[:, :H_A], table[:, H_A:]

    def chunk(t):
        return t.reshape((n_qb, Q_BLOCK) + t.shape[1:])

    def per_seq(args):
        qa_s, ka_s, va_s, qb_s, kb_s, vb_s, qi_s, ki_s, wi_s = args
        ka_p, va_p = pad_blocks(ka_s), pad_blocks(va_s)

        def per_block(blk):
            qa_c, qb_c, qi_c, wi_c, pos_c = blk
            oa = moba_attend(qa_c, ka_p, va_p, pos_c, tab_a)
            ob = dsa_attend(qb_c, kb_s, vb_s, qi_c, ki_s, wi_c, pos_c, tab_b, topk)
            return oa, ob

        oa, ob = lax.map(per_block, (chunk(qa_s), chunk(qb_s), chunk(qi_s), chunk(wi_s), chunk(pos)))
        return oa.reshape(T, H_A, HEAD_DIM), ob.reshape(T, H_B, HEAD_DIM)

    oa, ob = lax.map(per_seq, (qa, ka, va, qb, kb, vb, qi, ki, wi))
    y = attn_merge(oa, ga, ob, gb, w_out)
    return y, ka, va, kb, vb, ki


def attn_sample(h, page_table, ck_a, cv_a, ck_b, cv_b, cki, w_in, qn_a, kn_a, qn_b, kn_b, w_out, table):
    T = h.shape[1]
    past = page_table.shape[1] * PAGE_SIZE
    qa, ka, va, ga, qb, kb, vb, gb, qi, ki, wi = attn_project(h, w_in, qn_a, kn_a, qn_b, kn_b)
    pos = past + jnp.arange(T, dtype=jnp.int32)
    topk = min(DSA_TOPK_MAX, (past + T) // 4)
    tab_a, tab_b = table[:, :H_A], table[:, H_A:]

    def per_seq(args):
        qa_s, ka_s, va_s, qb_s, kb_s, vb_s, qi_s, ki_s, wi_s, pt = args

        def gather(pool, new):
            rows = pool[pt].reshape((-1,) + pool.shape[2:])
            return jnp.concatenate([rows.astype(new.dtype), new], axis=0)

        ka_f, va_f = gather(ck_a, ka_s), gather(cv_a, va_s)
        kb_f, vb_f = gather(ck_b, kb_s), gather(cv_b, vb_s)
        ki_f = gather(cki, ki_s)
        oa = moba_attend(qa_s, pad_blocks(ka_f), pad_blocks(va_f), pos, tab_a)
        ob = dsa_attend(qb_s, kb_f, vb_f, qi_s, ki_f, wi_s, pos, tab_b, topk)
        return oa, ob

    oa, ob = lax.map(per_seq, (qa, ka, va, qb, kb, vb, qi, ki, wi, page_table))
    y = attn_merge(oa, ga, ob, gb, w_out)
    return y, ka, va, kb, vb, ki


def conv_mixer(h, state, w_in, conv_w, conv_b, ln_g, ln_b, w_out):
    z = jnp.einsum('btd,de->bte', h, w_in)
    a, b, g = jnp.split(z, 3, axis=-1)
    u = a * jax.nn.sigmoid(b)
    up = jnp.concatenate([state.astype(u.dtype), u], axis=1)
    new_state = up[:, up.shape[1] - (CONV_WIDTH - 1):]
    c = lax.conv_general_dilated(up, conv_w[:, None, :].astype(u.dtype), window_strides=(1,), padding='VALID',
                                 dimension_numbers=('NWC', 'WIO', 'NWC'), feature_group_count=D_CONV) + conv_b
    c = layer_norm(c, ln_g, ln_b)
    y = jnp.einsum('btc,cd->btd', jax.nn.silu(c) * jax.nn.silu(g), w_out)
    return y, new_state


def setup_inputs(seed: int = 0) -> dict:
    key = jax.random.key(seed)
    ks = jax.random.split(key, 24)
    n_pages = PAST_LEN // PAGE_SIZE
    pool_pages = (DEC_BATCH * n_pages * 5) // 4
    f32 = jnp.float32
    nrm = lambda k, s, sc: jax.random.normal(k, s, f32) * sc
    page_table = jax.random.permutation(ks[8], pool_pages)[:DEC_BATCH * n_pages].reshape(DEC_BATCH, n_pages).astype(jnp.int32)
    return {
        'x_prompt': nrm(ks[0], (BATCH, SEQ, D_MODEL), 1.0),
        'x_sample': nrm(ks[1], (DEC_BATCH, DEC_SEQ, D_MODEL), 1.0),
        'cache_k_a': nrm(ks[2], (N_ATTN_LAYERS, pool_pages, PAGE_SIZE, H_A, HEAD_DIM), 1.0),
        'cache_v_a': nrm(ks[3], (N_ATTN_LAYERS, pool_pages, PAGE_SIZE, H_A, HEAD_DIM), 1.0),
        'cache_k_b': nrm(ks[4], (N_ATTN_LAYERS, pool_pages, PAGE_SIZE, H_B, HEAD_DIM), 1.0),
        'cache_v_b': nrm(ks[5], (N_ATTN_LAYERS, pool_pages, PAGE_SIZE, H_B, HEAD_DIM), 1.0),
        'cache_kidx_b': nrm(ks[6], (N_ATTN_LAYERS, pool_pages, PAGE_SIZE, D_IDX), 1.0),
        'state_conv': nrm(ks[7], (N_CONV_LAYERS, DEC_BATCH, CONV_WIDTH - 1, D_CONV), 0.5),
        'page_table': page_table,
        'norm_g': 1.0 + nrm(ks[9], (DEPTH, D_MODEL), 0.01),
        'rel_bias_table': nrm(ks[10], (N_BUCKETS, N_ATTN_HEADS), 0.1),
        'w_in_attn': nrm(ks[11], (N_ATTN_LAYERS, D_MODEL, D_IN_A), D_MODEL ** -0.5),
        'q_norm_a': 1.0 + nrm(ks[12], (N_ATTN_LAYERS, HEAD_DIM), 0.01),
        'k_norm_a': 1.0 + nrm(ks[13], (N_ATTN_LAYERS, HEAD_DIM), 0.01),
        'q_norm_b': 1.0 + nrm(ks[14], (N_ATTN_LAYERS, HEAD_DIM), 0.01),
        'k_norm_b': 1.0 + nrm(ks[15], (N_ATTN_LAYERS, HEAD_DIM), 0.01),
        'w_out_attn': nrm(ks[16], (N_ATTN_LAYERS, D_A + D_B, D_MODEL), (D_A + D_B) ** -0.5),
        'w_in_conv': nrm(ks[17], (N_CONV_LAYERS, D_MODEL, D_IN_C), D_MODEL ** -0.5),
        'conv_w': nrm(ks[18], (N_CONV_LAYERS, CONV_WIDTH, D_CONV), CONV_WIDTH ** -0.5),
        'conv_b': nrm(ks[19], (N_CONV_LAYERS, D_CONV), 0.01),
        'conv_ln_g': 1.0 + nrm(ks[20], (N_CONV_LAYERS, D_CONV), 0.01),
        'conv_ln_b': nrm(ks[21], (N_CONV_LAYERS, D_CONV), 0.01),
        'w_out_conv': nrm(ks[22], (N_CONV_LAYERS, D_CONV, D_MODEL), D_CONV ** -0.5),
    }


def reference(x_prompt, x_sample, cache_k_a, cache_v_a, cache_k_b, cache_v_b, cache_kidx_b, state_conv, page_table,
              norm_g, rel_bias_table, w_in_attn, q_norm_a, k_norm_a, q_norm_b, k_norm_b, w_out_attn,
              w_in_conv, conv_w, conv_b, conv_ln_g, conv_ln_b, w_out_conv):
    hp, hs = x_prompt, x_sample
    kap, vap, kbp, vbp, kip, cvp = [], [], [], [], [], []
    kas, vas, kbs, vbs, kis, cvs = [], [], [], [], [], []
    for l in range(DEPTH):
        hp_n = rms_norm(hp, norm_g[l])
        hs_n = rms_norm(hs, norm_g[l])
        if l % 2 == 0:
            i = l // 2
            yp, ka, va, kb, vb, ki = attn_prompt(hp_n, w_in_attn[i], q_norm_a[i], k_norm_a[i], q_norm_b[i], k_norm_b[i],
                                                 w_out_attn[i], rel_bias_table)
            ys, ka2, va2, kb2, vb2, ki2 = attn_sample(hs_n, page_table, cache_k_a[i], cache_v_a[i], cache_k_b[i], cache_v_b[i],
                                                      cache_kidx_b[i], w_in_attn[i], q_norm_a[i], k_norm_a[i], q_norm_b[i],
                                                      k_norm_b[i], w_out_attn[i], rel_bias_table)
            kap.append(ka); vap.append(va); kbp.append(kb); vbp.append(vb); kip.append(ki)
            kas.append(ka2); vas.append(va2); kbs.append(kb2); vbs.append(vb2); kis.append(ki2)
        else:
            j = l // 2
            zero_hist = jnp.zeros((hp.shape[0], CONV_WIDTH - 1, D_CONV), dtype=hp.dtype)
            yp, sp = conv_mixer(hp_n, zero_hist, w_in_conv[j], conv_w[j], conv_b[j], conv_ln_g[j], conv_ln_b[j], w_out_conv[j])
            ys, ss = conv_mixer(hs_n, state_conv[j], w_in_conv[j], conv_w[j], conv_b[j], conv_ln_g[j], conv_ln_b[j], w_out_conv[j])
            cvp.append(sp); cvs.append(ss)
        hp = hp + yp
        hs = hs + ys
    return (hp, hs,
            jnp.stack(kap), jnp.stack(vap), jnp.stack(kbp), jnp.stack(vbp), jnp.stack(kip), jnp.stack(cvp),
            jnp.stack(kas), jnp.stack(vas), jnp.stack(kbs), jnp.stack(vbs), jnp.stack(kis), jnp.stack(cvs))
```

```python
import functools
import math

import numpy as np
import jax
import jax.numpy as jnp
from jax import lax
from jax.experimental import pallas as pl
from jax.experimental.pallas import tpu as pltpu

F32 = jnp.float32
BF16 = jnp.bfloat16
I32 = jnp.int32

D_MODEL = 1024
HEAD_DIM = 64
N_HEADS = 8
D_ATT = N_HEADS * HEAD_DIM
MOBA_BLOCK = 256
MOBA_TOPK = 3
DSA_TOPK_MAX = 256
H_IDX = 8
D_IDX = 64
CONV_WIDTH = 31
N_BUCKETS = 32
MAX_DISTANCE = 128
PAGE_SIZE = 128
RMS_EPS = 1e-6
LN_EPS = 1e-5

LANES = 128
SUBLANES = 8
TQ = MOBA_BLOCK
HIST = 32
NEW_PAD = 16
INT_MIN = -2 ** 31
NEG_INF = float("-inf")
VMEM_LIMIT = 56 * 1024 * 1024


def _dot(a, b):
    return jnp.dot(a, b, preferred_element_type=F32)


def _dot_nt(a, b):
    return lax.dot_general(a, b, (((1,), (1,)), ((), ())), preferred_element_type=F32)


def _sigmoid(x):
    return 1.0 / (1.0 + jnp.exp(-x))


def _silu(x):
    return x * _sigmoid(x)


def _rms(x, g):
    return x * lax.rsqrt(jnp.mean(x * x, axis=-1, keepdims=True) + RMS_EPS) * g


def _half_masks():
    lane = lax.broadcasted_iota(I32, (1, LANES), 1)
    return lane < HEAD_DIM, lane >= HEAD_DIM


def _osm_update(s, pv, m, l, acc):
    m_new = jnp.maximum(m, jnp.max(s, axis=1, keepdims=True))
    m_safe = jnp.where(m_new == NEG_INF, 0.0, m_new)
    p = jnp.exp(s - m_safe)
    alpha = jnp.exp(m - m_safe)
    l = alpha * l + jnp.sum(p, axis=1, keepdims=True)
    acc = alpha * acc + pv(p.astype(BF16))
    return m_new, l, acc


def _top_blocks(g, gcol):
    picks = []
    for _ in range(MOBA_TOPK):
        mx = jnp.max(g, axis=1, keepdims=True)
        a = jnp.min(jnp.where(g == mx, gcol, LANES), axis=1, keepdims=True)
        a = jnp.where(mx == NEG_INF, -1, a)
        picks.append(a)
        g = jnp.where(gcol == a, NEG_INF, g)
    return picks


def _f32_key(x):
    b = lax.bitcast_convert_type(x, I32)
    k = b ^ ((b >> 31) & 0x7FFFFFFF)
    return jnp.where(x == 0.0, 0, k)


def _select_threshold(count_fn, k_top, n_cols):
    k_f = float(k_top)
    t = jnp.where(count_fn(lambda kt, gc: kt >= 0) >= k_f, 0, INT_MIN).astype(I32)

    def bit_step(b, t):
        cand = t | jnp.left_shift(jnp.int32(1), 30 - b)
        return jnp.where(count_fn(lambda kt, gc: kt >= cand) >= k_f, cand, t)

    t = lax.fori_loop(0, 31, bit_step, t)
    t = jnp.maximum(t, INT_MIN + 1)
    cnt_ge = count_fn(lambda kt, gc: kt >= t)
    need = k_f - count_fn(lambda kt, gc: kt > t)
    n_bits = max(1, int(math.ceil(math.log2(n_cols))))

    def tie_search():
        def col_step(b, x):
            cand = x | jnp.left_shift(jnp.int32(1), n_bits - 1 - b)
            below = count_fn(lambda kt, gc: (kt == t) & (gc < cand))
            return jnp.where(below < need, cand, x)
        return lax.fori_loop(0, n_bits, col_step, jnp.zeros_like(t))

    any_excess = jnp.max(jnp.where(cnt_ge > k_f, 1.0, 0.0)) > 0.0
    x = lax.cond(any_excess, tie_search, lambda: jnp.full_like(t, n_cols))
    return t, x


_PARAMS = lambda n: pltpu.CompilerParams(dimension_semantics=("arbitrary",) * n, vmem_limit_bytes=VMEM_LIMIT)


def _headnorm_rows(z, ones_ref, gain):
    ssq = _dot((z * z).astype(BF16), ones_ref[...])
    return z * lax.rsqrt(ssq * (1.0 / HEAD_DIM) + RMS_EPS) * gain


def _proj_prompt_kernel(x_ref, g_ref, wn_ref, wt_ref, gains_ref, gains_t_ref, ones_ref,
                        qa_ref, qb_ref, qi_ref, wi_ref, ga_ref, gb_ref,
                        kat_ref, vat_ref, kbt_ref, vbt_ref, kkt_ref,
                        ka32_ref, va32_ref, kb32_ref, vb32_ref, ki32_ref):
    h = _rms(x_ref[...], g_ref[...]).astype(BF16)
    tm = h.shape[0]
    scale = HEAD_DIM ** -0.5

    def grp(c):
        return _dot(h, wn_ref[:, c * D_ATT:(c + 1) * D_ATT])

    def grp_t(c):
        return _dot_nt(wt_ref[c * D_ATT:(c + 1) * D_ATT, :], h)

    def headnorm_t(zt, row):
        z3 = zt.reshape(N_HEADS, HEAD_DIM, tm)
        ssq = jnp.sum(z3 * z3, axis=1, keepdims=True)
        z3 = z3 * lax.rsqrt(ssq * (1.0 / HEAD_DIM) + RMS_EPS)
        return z3.reshape(D_ATT, tm) * gains_t_ref[row]

    qa_ref[...] = (_headnorm_rows(grp(0), ones_ref, gains_ref[0:1, :]) * scale).astype(BF16)
    ga_ref[...] = grp(1)
    qb_ref[...] = (_headnorm_rows(grp(2), ones_ref, gains_ref[1:2, :]) * scale).astype(BF16)
    gb_ref[...] = grp(3)
    qi_ref[...] = grp(4).astype(BF16)
    wi_ref[...] = _dot(h, wn_ref[:, 5 * D_ATT:])
    for c, (t_ref, f_ref, row) in enumerate(((kat_ref, ka32_ref, 0), (vat_ref, va32_ref, None),
                                             (kbt_ref, kb32_ref, 1), (vbt_ref, vb32_ref, None))):
        zt = grp_t(c)
        if row is not None:
            zt = headnorm_t(zt, row)
        f_ref[...] = zt
        t_ref[...] = zt.astype(BF16)
    kk = _dot_nt(wt_ref[4 * D_ATT:, :], h)
    kkt_ref[...] = kk.astype(BF16)
    ki32_ref[...] = kk[:D_IDX, :]


def _project_prompt(x, g, wn, wt, gains, gains_t, ones_bd, batch, seq):
    tm = TQ
    n_t = seq // tm
    n = batch * seq
    row = lambda c: pl.BlockSpec((tm, c), lambda b, t: (b * n_t + t, 0))
    full = lambda a: pl.BlockSpec(a.shape, lambda b, t: (0,) * a.ndim)
    tiles = lambda r: pl.BlockSpec((None, None, r, tm), lambda b, t: (b, t, 0, 0))
    cols = lambda r: pl.BlockSpec((None, r, tm), lambda b, t: (b, 0, t))
    sd = jax.ShapeDtypeStruct
    out_shape = ([sd((n, D_ATT), BF16)] * 3 + [sd((n, LANES), F32)] + [sd((n, D_ATT), F32)] * 2
                 + [sd((batch, n_t, D_ATT, tm), BF16)] * 4 + [sd((batch, n_t, LANES, tm), BF16)]
                 + [sd((batch, D_ATT, seq), F32)] * 4 + [sd((batch, D_IDX, seq), F32)])
    out_specs = ([row(D_ATT)] * 3 + [row(LANES)] + [row(D_ATT)] * 2
                 + [tiles(D_ATT)] * 4 + [tiles(LANES)] + [cols(D_ATT)] * 4 + [cols(D_IDX)])
    return pl.pallas_call(
        _proj_prompt_kernel,
        grid=(batch, n_t),
        in_specs=[row(D_MODEL), full(g), full(wn), full(wt), full(gains), full(gains_t), full(ones_bd)],
        out_specs=out_specs, out_shape=out_shape,
        compiler_params=_PARAMS(2), name="proj_prompt",
    )(x, g, wn, wt, gains, gains_t, ones_bd)


def _proj_sample_kernel(x_ref, g_ref, w_ref, gains_ref, ones_ref,
                        qa_ref, ka_ref, va_ref, qb_ref, kb_ref, vb_ref, qi_ref, wi_ref,
                        ga_ref, gb_ref, ka32_ref, va32_ref, kb32_ref, vb32_ref, ki32_ref):
    h = _rms(x_ref[...], g_ref[...]).astype(BF16)
    scale = HEAD_DIM ** -0.5

    def grp(c):
        return _dot(h, w_ref[:, c * D_ATT:(c + 1) * D_ATT])

    qa_ref[...] = (_headnorm_rows(grp(0), ones_ref, gains_ref[0:1, :]) * scale).astype(BF16)
    ka = _headnorm_rows(grp(1), ones_ref, gains_ref[1:2, :])
    ka32_ref[...] = ka
    ka_ref[...] = ka.astype(BF16)
    va = grp(2)
    va32_ref[...] = va
    va_ref[...] = va.astype(BF16)
    ga_ref[...] = grp(3)
    qb_ref[...] = (_headnorm_rows(grp(4), ones_ref, gains_ref[2:3, :]) * scale).astype(BF16)
    kb = _headnorm_rows(grp(5), ones_ref, gains_ref[3:4, :])
    kb32_ref[...] = kb
    kb_ref[...] = kb.astype(BF16)
    vb = grp(6)
    vb32_ref[...] = vb
    vb_ref[...] = vb.astype(BF16)
    gb_ref[...] = grp(7)
    qi_ref[...] = grp(8).astype(BF16)
    tail = _dot(h, w_ref[:, 9 * D_ATT:])
    ki32_ref[...] = tail[:, :D_IDX]
    wi_ref[...] = tail[:, LANES:]


def _project_sample(x, g, w, gains, ones_bd):
    n = x.shape[0]
    full = lambda a: pl.BlockSpec(a.shape, lambda i: (0,) * a.ndim)
    sd = lambda c, dt: jax.ShapeDtypeStruct((n, c), dt)
    out_shape = ([sd(D_ATT, BF16)] * 7 + [sd(LANES, F32)] + [sd(D_ATT, F32)] * 6 + [sd(D_IDX, F32)])
    return pl.pallas_call(
        _proj_sample_kernel,
        grid=(1,),
        in_specs=[full(x), full(g), full(w), full(gains), full(ones_bd)],
        out_specs=[full(s) for s in out_shape], out_shape=out_shape,
        compiler_params=_PARAMS(1), name="proj_sample",
    )(x, g, w, gains, ones_bd)


def _moba_prompt_kernel(c31_ref, q_ref, k_ref, v_ref, bias_ref, o_ref, kmean_ref, *, n_blk):
    hp = pl.program_id(1)
    i = pl.program_id(2)
    gcol = lax.broadcasted_iota(I32, (TQ, LANES), 1)

    @pl.when(i == 0)
    def _():
        lane = lax.broadcasted_iota(I32, (LANES, LANES), 1)
        km = jnp.zeros((LANES, LANES), F32)
        for j in range(n_blk):
            cj = jnp.mean(k_ref[j].astype(F32), axis=1, keepdims=True)
            km = jnp.where(lane == j, cj, km)
        kmean_ref[...] = km

    masks = _half_masks()
    q = q_ref[...]
    row = lax.broadcasted_iota(I32, (TQ, TQ), 0)
    col = lax.broadcasted_iota(I32, (TQ, TQ), 1)
    kmean = kmean_ref[...].astype(BF16)
    outs = []
    for half in range(2):
        qm = jnp.where(masks[half], q, jnp.zeros_like(q))
        picks = _top_blocks(jnp.where(gcol < i, _dot(qm, kmean), NEG_INF), gcol)
        c31 = c31_ref[hp * 2 + half]

        def attend(j, bias, carry):
            s = _dot(qm, k_ref[j]) + bias
            sel = (picks[0] == j) | (picks[1] == j) | (picks[2] == j)
            s = jnp.where(sel, s, NEG_INF)
            return _osm_update(s, lambda p: _dot_nt(p, v_ref[j]), *carry)

        carry = (jnp.full((TQ, 1), NEG_INF, F32), jnp.zeros((TQ, 1), F32), jnp.zeros((TQ, LANES), F32))
        s = _dot(qm, k_ref[i]) + bias_ref[half, 0]
        s = jnp.where(col <= row, s, NEG_INF)
        carry = _osm_update(s, lambda p: _dot_nt(p, v_ref[i]), *carry)
        n_far = jnp.maximum(i - 1, 0)
        carry = lax.fori_loop(n_far, i, lambda j, c: attend(j, bias_ref[half, 1], c), carry)
        carry = lax.fori_loop(0, n_far, lambda j, c: attend(j, c31, c), carry)
        outs.append(carry[2] / carry[1])
    o_ref[...] = jnp.where(masks[0], outs[0], outs[1])


def _moba_prompt(c31, q, kt, vt, bias, batch, seq):
    n_q = seq // TQ
    kern = functools.partial(_moba_prompt_kernel, n_blk=n_q)
    kv = pl.BlockSpec((None, n_q, LANES, TQ), lambda b, hp, i: (b, 0, hp, 0))
    return pl.pallas_call(
        kern,
        grid=(batch, N_HEADS // 2, n_q),
        in_specs=[
            pl.BlockSpec(memory_space=pltpu.SMEM),
            pl.BlockSpec((TQ, LANES), lambda b, hp, i: (b * n_q + i, hp)),
            kv, kv,
            pl.BlockSpec((2, 2, TQ, TQ), lambda b, hp, i: (hp, 0, 0, 0)),
        ],
        out_specs=pl.BlockSpec((TQ, LANES), lambda b, hp, i: (b * n_q + i, hp)),
        out_shape=jax.ShapeDtypeStruct((batch * seq, D_ATT), F32),
        scratch_shapes=[pltpu.VMEM((LANES, LANES), F32)],
        compiler_params=_PARAMS(3), name="moba_prompt",
    )(c31, q, kt, vt, bias)


def _dsa_prompt_kernel(c31_ref, q_ref, qi_ref, wi_ref, kk_ref, k_ref, v_ref, bias_ref, o_ref,
                       keys_ref, m_ref, l_ref, acc_ref, *, seq, k_top):
    i = pl.program_id(1)
    masks = _half_masks()
    row = lax.broadcasted_iota(I32, (TQ, TQ), 0)
    col = lax.broadcasted_iota(I32, (TQ, TQ), 1)
    zero_b = jnp.zeros((TQ, LANES), BF16)

    def head_q(ref, h):
        pair, half = divmod(h, 2)
        return jnp.where(masks[half], ref[:, pair * LANES:(pair + 1) * LANES], zero_b)

    def index_tile(j):
        kkj = kk_ref[j]
        acc = jnp.zeros((TQ, TQ), F32)
        for h in range(H_IDX):
            acc = acc + wi_ref[:, h:h + 1] * jnp.maximum(_dot(head_q(qi_ref, h), kkj), 0.0)
        return _f32_key(acc)

    def index_body(j, c):
        keys_ref[j] = index_tile(j)
        return c

    lax.fori_loop(0, i, index_body, 0)
    keys_ref[i] = jnp.where(col <= row, index_tile(i), INT_MIN)

    def count_fn(pred):
        def body(j, part):
            return part + jnp.where(pred(keys_ref[j], col + j * TQ), 1.0, 0.0)
        part = lax.fori_loop(0, i + 1, body, jnp.zeros((TQ, TQ), F32))
        return jnp.sum(part, axis=1, keepdims=True)

    t, x = _select_threshold(count_fn, k_top, seq)

    m_ref[...] = jnp.full(m_ref.shape, NEG_INF, F32)
    l_ref[...] = jnp.zeros(l_ref.shape, F32)
    acc_ref[...] = jnp.zeros(acc_ref.shape, F32)

    def attend(j, bias_of):
        kt = keys_ref[j]
        sel = (kt > t) | ((kt == t) & (col + j * TQ <= x))
        pen = jnp.where(sel, 0.0, NEG_INF)
        for h in range(N_HEADS):
            rows = slice((h // 2) * LANES, (h // 2 + 1) * LANES)
            s = _dot(head_q(q_ref, h), k_ref[j, rows, :]) + bias_of(h) + pen
            m, l, acc = _osm_update(s, lambda p: _dot_nt(p, v_ref[j, rows, :]), m_ref[h], l_ref[h], acc_ref[h])
            m_ref[h] = m
            l_ref[h] = l
            acc_ref[h] = acc

    def attend_loop(lo, hi, bias_of):
        def body(j, c):
            attend(j, bias_of)
            return c
        lax.fori_loop(lo, hi, body, 0)

    n_far = jnp.maximum(i - 1, 0)
    attend(i, lambda h: bias_ref[h, 0])
    attend_loop(n_far, i, lambda h: bias_ref[h, 1])
    attend_loop(0, n_far, lambda h: c31_ref[N_HEADS + h])
    for pair in range(N_HEADS // 2):
        o0 = acc_ref[2 * pair] / l_ref[2 * pair]
        o1 = acc_ref[2 * pair + 1] / l_ref[2 * pair + 1]
        o_ref[:, pair * LANES:(pair + 1) * LANES] = jnp.where(masks[0], o0, o1)


def _dsa_prompt(c31, q, qi, wi, kkt, kt, vt, bias, batch, seq):
    n_q = seq // TQ
    k_top = min(DSA_TOPK_MAX, seq // 4)
    kern = functools.partial(_dsa_prompt_kernel, seq=seq, k_top=k_top)
    tile = lambda c: pl.BlockSpec((TQ, c), lambda b, i: (b * n_q + i, 0))
    whole = lambda r: pl.BlockSpec((None, n_q, r, TQ), lambda b, i: (b, 0, 0, 0))
    return pl.pallas_call(
        kern,
        grid=(batch, n_q),
        in_specs=[
            pl.BlockSpec(memory_space=pltpu.SMEM),
            tile(D_ATT), tile(D_ATT), tile(LANES),
            whole(LANES), whole(D_ATT), whole(D_ATT),
            pl.BlockSpec((N_HEADS, 2, TQ, TQ), lambda b, i: (1, 0, 0, 0)),
        ],
        out_specs=tile(D_ATT),
        out_shape=jax.ShapeDtypeStruct((batch * seq, D_ATT), F32),
        scratch_shapes=[
            pltpu.VMEM((n_q, TQ, TQ), I32),
            pltpu.VMEM((N_HEADS, TQ, 1), F32),
            pltpu.VMEM((N_HEADS, TQ, 1), F32),
            pltpu.VMEM((N_HEADS, TQ, LANES), F32),
        ],
        compiler_params=_PARAMS(2), name="dsa_prompt",
    )(c31, q, qi, wi, kkt, kt, vt, bias)


def _expand_tokens(a, n_tok):
    return jnp.broadcast_to(a[:, None, :], (n_tok, N_HEADS, a.shape[-1])).reshape(n_tok * N_HEADS, a.shape[-1])


def _collapse_heads(o, hmask, n_tok):
    return jnp.sum((o * hmask).reshape(n_tok, N_HEADS, D_ATT), axis=1)


def _sample_index_kernel(pt_ref, kit_ref, qi_ref, wcol_ref, knew_ref, pen_ref, keys_ref, *, n_pages, n_tok, k_top):
    p = pl.program_id(1)

    def keys_of(kt):
        x = jnp.maximum(_dot(qi_ref[...], kt), 0.0) * wcol_ref[...]
        return _f32_key(jnp.sum(x.reshape(n_tok, H_IDX, LANES), axis=1))

    keys_ref[p] = keys_of(kit_ref[...].astype(BF16))

    @pl.when(p == n_pages - 1)
    def _():
        lane = lax.broadcasted_iota(I32, (n_tok, LANES), 1)
        tok = lax.broadcasted_iota(I32, (n_tok, LANES), 0)
        keys_ref[n_pages] = jnp.where(lane <= tok, keys_of(knew_ref[...]), INT_MIN)

        def count_fn(pred):
            def body(j, part):
                return part + jnp.where(pred(keys_ref[j], lane + j * LANES), 1.0, 0.0)
            part = lax.fori_loop(0, n_pages + 1, body, jnp.zeros((n_tok, LANES), F32))
            return jnp.sum(part, axis=1, keepdims=True)

        t, x = _select_threshold(count_fn, k_top, (n_pages + 1) * LANES)

        def write(j, c):
            kt = keys_ref[j]
            sel = (kt > t) | ((kt == t) & (lane + j * LANES <= x))
            pen_ref[j] = jnp.where(sel, 0.0, NEG_INF)
            return c

        lax.fori_loop(0, n_pages + 1, write, 0)


def _sample_index(page_table, kit, qi, wcol, knew, n_tok, k_top):
    n_seq, n_pages = page_table.shape
    kern = functools.partial(_sample_index_kernel, n_pages=n_pages, n_tok=n_tok, k_top=k_top)
    per_seq = lambda a: pl.BlockSpec((None,) + a.shape[1:], lambda b, p, pt: (b,) + (0,) * (a.ndim - 1))
    grid_spec = pltpu.PrefetchScalarGridSpec(
        num_scalar_prefetch=1, grid=(n_seq, n_pages),
        in_specs=[pl.BlockSpec((None, D_IDX, PAGE_SIZE), lambda b, p, pt: (pt[b, p], 0, 0)),
                  per_seq(qi), per_seq(wcol), per_seq(knew)],
        out_specs=pl.BlockSpec((None, n_pages + 1, n_tok, LANES), lambda b, p, pt: (b, 0, 0, 0)),
        scratch_shapes=[pltpu.VMEM((n_pages + 1, n_tok, LANES), I32)])
    return pl.pallas_call(
        kern, grid_spec=grid_spec,
        out_shape=jax.ShapeDtypeStruct((n_seq, n_pages + 1, n_tok, LANES), F32),
        compiler_params=_PARAMS(2), name="sample_index",
    )(page_table, kit, qi, wcol, knew)


def _sample_dsa_kernel(pt_ref, kt_ref, vt_ref, q_ref, pen_ref, blast_ref, c31_ref, knew_ref, vnew_ref, bnew_ref,
                       hmask_ref, o_ref, m_ref, l_ref, acc_ref, *, n_pages, n_tok):
    p = pl.program_id(1)

    @pl.when(p == 0)
    def _():
        m_ref[...] = jnp.full(m_ref.shape, NEG_INF, F32)
        l_ref[...] = jnp.zeros(l_ref.shape, F32)
        acc_ref[...] = jnp.zeros(acc_ref.shape, F32)

    q = q_ref[...]
    bias = jnp.where(p == n_pages - 1, blast_ref[...], c31_ref[...])
    s = _dot(q, kt_ref[...].astype(BF16)) + bias + _expand_tokens(pen_ref[p], n_tok)
    vt = vt_ref[...].astype(BF16)
    m, l, acc = _osm_update(s, lambda pp: _dot_nt(pp, vt), m_ref[...], l_ref[...], acc_ref[...])
    m_ref[...] = m
    l_ref[...] = l
    acc_ref[...] = acc

    @pl.when(p == n_pages - 1)
    def _():
        pen_new = _expand_tokens(pen_ref[n_pages][:, :NEW_PAD], n_tok)
        s_n = _dot_nt(q, knew_ref[...]) + bnew_ref[...] + pen_new
        _, l2, acc2 = _osm_update(s_n, lambda pp: _dot(pp, vnew_ref[...]), m, l, acc)
        o_ref[...] = _collapse_heads(acc2 / l2, hmask_ref[...], n_tok)


def _sample_dsa(page_table, kt, vt, q, pen, blast, c31col, knew, vnew, bnew, hmask, n_tok):
    n_seq, n_pages = page_table.shape
    rows = n_tok * N_HEADS
    kern = functools.partial(_sample_dsa_kernel, n_pages=n_pages, n_tok=n_tok)
    per_seq = lambda a: pl.BlockSpec((None,) + a.shape[1:], lambda b, p, pt: (b,) + (0,) * (a.ndim - 1))
    full = lambda a: pl.BlockSpec(a.shape, lambda b, p, pt: (0,) * a.ndim)
    page = pl.BlockSpec((None, D_ATT, PAGE_SIZE), lambda b, p, pt: (pt[b, p], 0, 0))
    grid_spec = pltpu.PrefetchScalarGridSpec(
        num_scalar_prefetch=1, grid=(n_seq, n_pages),
        in_specs=[page, page, per_seq(q), per_seq(pen), full(blast), full(c31col),
                  per_seq(knew), per_seq(vnew), full(bnew), full(hmask)],
        out_specs=pl.BlockSpec((None, n_tok, D_ATT), lambda b, p, pt: (b, 0, 0)),
        scratch_shapes=[pltpu.VMEM((rows, 1), F32), pltpu.VMEM((rows, 1), F32), pltpu.VMEM((rows, D_ATT), F32)])
    return pl.pallas_call(
        kern, grid_spec=grid_spec,
        out_shape=jax.ShapeDtypeStruct((n_seq, n_tok, D_ATT), F32),
        compiler_params=_PARAMS(2), name="sample_dsa",
    )(page_table, kt, vt, q, pen, blast, c31col, knew, vnew, bnew, hmask)


def _sample_moba_kernel(pt_ref, k0_ref, k1_ref, v0_ref, v1_ref, q_ref, blast_ref, c31_ref, knew_ref, vnew_ref,
                        bnew_ref, hmask_ref, o_ref, gate_ref, ms_ref, ls_ref, accs_ref, *, n_blk, n_tok):
    j = pl.program_id(1)
    rows = n_tok * N_HEADS
    lane = lax.broadcasted_iota(I32, (rows, LANES), 1)

    @pl.when(j == 0)
    def _():
        gate_ref[...] = jnp.full(gate_ref.shape, NEG_INF, F32)
        ms_ref[...] = jnp.full(ms_ref.shape, NEG_INF, F32)
        ls_ref[...] = jnp.zeros(ls_ref.shape, F32)

    q = q_ref[...]
    r0 = _dot(q, k0_ref[...].astype(BF16))
    r1 = _dot(q, k1_ref[...].astype(BF16))
    gate = jnp.sum(r0 + r1, axis=1, keepdims=True)
    s0 = r0 + c31_ref[...]
    s1 = r1 + jnp.where(j == n_blk - 1, blast_ref[...], c31_ref[...])
    m = jnp.maximum(jnp.max(s0, axis=1, keepdims=True), jnp.max(s1, axis=1, keepdims=True))
    p0 = jnp.exp(s0 - m)
    p1 = jnp.exp(s1 - m)
    l = jnp.sum(p0, axis=1, keepdims=True) + jnp.sum(p1, axis=1, keepdims=True)
    accs_ref[j] = (_dot_nt(p0.astype(BF16), v0_ref[...].astype(BF16))
                   + _dot_nt(p1.astype(BF16), v1_ref[...].astype(BF16)))
    gate_ref[...] = jnp.where(lane == j, gate, gate_ref[...])
    ms_ref[...] = jnp.where(lane == j, m, ms_ref[...])
    ls_ref[...] = jnp.where(lane == j, l, ls_ref[...])

    @pl.when(j == n_blk - 1)
    def _():
        picks = _top_blocks(jnp.where(lane < n_blk, gate_ref[...], NEG_INF), lane)
        sel = (lane == picks[0]) | (lane == picks[1]) | (lane == picks[2])
        s_n = _dot_nt(q, knew_ref[...]) + bnew_ref[...]
        m_o = jnp.max(s_n, axis=1, keepdims=True)
        p_o = jnp.exp(s_n - m_o)
        l_o = jnp.sum(p_o, axis=1, keepdims=True)
        ms = ms_ref[...]
        m_all = jnp.maximum(m_o, jnp.max(jnp.where(sel, ms, NEG_INF), axis=1, keepdims=True))
        w = jnp.where(sel, jnp.exp(ms - m_all), 0.0)
        w_o = jnp.exp(m_o - m_all)
        l_all = jnp.sum(w * ls_ref[...], axis=1, keepdims=True) + w_o * l_o
        out = w_o * _dot(p_o.astype(BF16), vnew_ref[...])
        for jj in range(n_blk):
            out = out + w[:, jj:jj + 1] * accs_ref[jj]
        o_ref[...] = _collapse_heads(out / l_all, hmask_ref[...], n_tok)


def _sample_moba(page_table, kt, vt, q, blast, c31col, knew, vnew, bnew, hmask, n_tok):
    n_seq, n_pages = page_table.shape
    n_blk = n_pages * PAGE_SIZE // MOBA_BLOCK
    rows = n_tok * N_HEADS
    kern = functools.partial(_sample_moba_kernel, n_blk=n_blk, n_tok=n_tok)
    per_seq = lambda a: pl.BlockSpec((None,) + a.shape[1:], lambda b, j, pt: (b,) + (0,) * (a.ndim - 1))
    full = lambda a: pl.BlockSpec(a.shape, lambda b, j, pt: (0,) * a.ndim)
    page = lambda o: pl.BlockSpec((None, D_ATT, PAGE_SIZE), lambda b, j, pt: (pt[b, 2 * j + o], 0, 0))
    grid_spec = pltpu.PrefetchScalarGridSpec(
        num_scalar_prefetch=1, grid=(n_seq, n_blk),
        in_specs=[page(0), page(1), page(0), page(1), per_seq(q), full(blast), full(c31col),
                  per_seq(knew), per_seq(vnew), full(bnew), full(hmask)],
        out_specs=pl.BlockSpec((None, n_tok, D_ATT), lambda b, j, pt: (b, 0, 0)),
        scratch_shapes=[pltpu.VMEM((rows, LANES), F32)] * 3 + [pltpu.VMEM((n_blk, rows, D_ATT), F32)])
    return pl.pallas_call(
        kern, grid_spec=grid_spec,
        out_shape=jax.ShapeDtypeStruct((n_seq, n_tok, D_ATT), F32),
        compiler_params=_PARAMS(2), name="sample_moba",
    )(page_table, kt, kt, vt, vt, q, blast, c31col, knew, vnew, bnew, hmask)


def _merge_and_conv_in(x_ref, oa_ref, ob_ref, ga_ref, gb_ref, wo_ref, g1_ref, wc_ref):
    o = jnp.concatenate([oa_ref[...] * _silu(ga_ref[...]), ob_ref[...] * _silu(gb_ref[...])], axis=1)
    h1 = x_ref[...] + _dot(o.astype(BF16), wo_ref[...])
    hn = _rms(h1, g1_ref[...]).astype(BF16)
    a = _dot(hn, wc_ref[:, 0:D_MODEL])
    b = _dot(hn, wc_ref[:, D_MODEL:2 * D_MODEL])
    g = _dot(hn, wc_ref[:, 2 * D_MODEL:3 * D_MODEL])
    return h1, a * _sigmoid(b), g


def _ln_gate_out(c, g, h1, lg_ref, lb_ref, wo2_ref):
    mu = jnp.mean(c, axis=-1, keepdims=True)
    d = c - mu
    var = jnp.mean(d * d, axis=-1, keepdims=True)
    cn = d * lax.rsqrt(var + LN_EPS) * lg_ref[...] + lb_ref[...]
    z = (_silu(cn) * _silu(g)).astype(BF16)
    return h1 + _dot(z, wo2_ref[...])


def _layer1_prompt_kernel(x_ref, oa_ref, ob_ref, ga_ref, gb_ref, wo_ref, g1_ref, wc_ref,
                          cw_ref, cb_ref, lg_ref, lb_ref, wo2_ref, y_ref, st_ref, up_ref, c_ref, *, tt):
    t = pl.program_id(1)

    @pl.when(t == 0)
    def _():
        up_ref[0:HIST, :] = jnp.zeros((HIST, D_MODEL), F32)

    h1, u, g = _merge_and_conv_in(x_ref, oa_ref, ob_ref, ga_ref, gb_ref, wo_ref, g1_ref, wc_ref)
    up_ref[HIST:HIST + tt, :] = u
    base = HIST - (CONV_WIDTH - 1)
    for cc in range(0, D_MODEL, 2 * LANES):
        cs = slice(cc, cc + 2 * LANES)
        acc = jnp.zeros((tt, 2 * LANES), F32) + cb_ref[:, cs]
        for w in range(CONV_WIDTH):
            acc = acc + cw_ref[w:w + 1, cs] * up_ref[base + w:base + w + tt, cs]
        c_ref[:, cs] = acc
    y_ref[...] = _ln_gate_out(c_ref[...], g, h1, lg_ref, lb_ref, wo2_ref)
    tail = up_ref[tt:tt + HIST, :]
    st_ref[...] = tail
    up_ref[0:HIST, :] = tail


def _layer1_prompt(x, oa, ob, ga, gb, wo, g1, wc, cw, cb, lg, lb, wo2, batch, seq, tt):
    n_t = seq // tt
    kern = functools.partial(_layer1_prompt_kernel, tt=tt)
    row = lambda c: pl.BlockSpec((tt, c), lambda b, t: (b * n_t + t, 0))
    full = lambda a: pl.BlockSpec(a.shape, lambda b, t: (0,) * a.ndim)
    return pl.pallas_call(
        kern,
        grid=(batch, n_t),
        in_specs=[row(D_MODEL), row(D_ATT), row(D_ATT), row(D_ATT), row(D_ATT),
                  full(wo), full(g1), full(wc), full(cw), full(cb), full(lg), full(lb), full(wo2)],
        out_specs=[row(D_MODEL), pl.BlockSpec((None, HIST, D_MODEL), lambda b, t: (b, 0, 0))],
        out_shape=[jax.ShapeDtypeStruct((batch * seq, D_MODEL), F32),
                   jax.ShapeDtypeStruct((batch, HIST, D_MODEL), F32)],
        scratch_shapes=[pltpu.VMEM((HIST + tt, D_MODEL), F32), pltpu.VMEM((tt, D_MODEL), F32)],
        compiler_params=_PARAMS(2), name="layer1_prompt",
    )(x, oa, ob, ga, gb, wo, g1, wc, cw, cb, lg, lb, wo2)


def _layer1_sample_kernel(x_ref, oa_ref, ob_ref, ga_ref, gb_ref, wo_ref, g1_ref, wc_ref,
                          cw_ref, cb_ref, lg_ref, lb_ref, wo2_ref, st_ref, y_ref, ns_ref, c_ref, *, n_tok, n_seq):
    n_hist = CONV_WIDTH - 1
    h1, u, g = _merge_and_conv_in(x_ref, oa_ref, ob_ref, ga_ref, gb_ref, wo_ref, g1_ref, wc_ref)
    u_t = [u[t * n_seq:(t + 1) * n_seq, :] for t in range(n_tok)]
    for t in range(n_tok):
        acc = jnp.zeros((n_seq, D_MODEL), F32) + cb_ref[...]
        for r in range(t, n_hist):
            acc = acc + cw_ref[r - t:r - t + 1, :] * st_ref[r]
        for r in range(t + 1):
            acc = acc + cw_ref[n_hist - t + r:n_hist - t + r + 1, :] * u_t[r]
        c_ref[t * n_seq:(t + 1) * n_seq, :] = acc
    y_ref[...] = _ln_gate_out(c_ref[...], g, h1, lg_ref, lb_ref, wo2_ref)
    for r in range(n_hist - n_tok):
        ns_ref[r] = st_ref[r + n_tok]
    for t in range(n_tok):
        ns_ref[n_hist - n_tok + t] = u_t[t]


def _layer1_sample(x, oa, ob, ga, gb, wo, g1, wc, cw, cb, lg, lb, wo2, st, n_tok, n_seq):
    kern = functools.partial(_layer1_sample_kernel, n_tok=n_tok, n_seq=n_seq)
    full = lambda a: pl.BlockSpec(a.shape, lambda i: (0,) * a.ndim)
    args = (x, oa, ob, ga, gb, wo, g1, wc, cw, cb, lg, lb, wo2, st)
    out_shape = [jax.ShapeDtypeStruct(x.shape, F32), jax.ShapeDtypeStruct(st.shape, F32)]
    return pl.pallas_call(
        kern, grid=(1,),
        in_specs=[full(a) for a in args],
        out_specs=[full(s) for s in out_shape], out_shape=out_shape,
        scratch_shapes=[pltpu.VMEM(x.shape, F32)],
        compiler_params=_PARAMS(1), name="layer1_sample",
    )(*args)


def _t5_bucket(n):
    exact = N_BUCKETS // 2
    nf = jnp.maximum(n, 1).astype(F32)
    large = exact + (jnp.log(nf / exact) / math.log(MAX_DISTANCE / exact) * (N_BUCKETS - exact)).astype(I32)
    large = jnp.minimum(large, N_BUCKETS - 1)
    return jnp.where(n < exact, n, large)


def _bias_by_distance(table, n):
    return table[_t5_bucket(jnp.arange(n, dtype=I32))]


def _prep_attn_weights(w_in, qn_a, kn_a, qn_b, kn_b):
    grp = lambda c: w_in[:, c * D_ATT:(c + 1) * D_ATT]
    ki = w_in[:, 9 * D_ATT:9 * D_ATT + D_IDX]
    wi = w_in[:, 9 * D_ATT + D_IDX:]
    wi_pad = jnp.concatenate([wi, jnp.zeros((D_MODEL, LANES - H_IDX), w_in.dtype)], axis=1)
    tile = lambda g: jnp.tile(g, N_HEADS)
    head = np.arange(D_ATT) // HEAD_DIM
    ones_bd = jnp.asarray(head[:, None] == head[None, :], dtype=BF16)
    wn = jnp.concatenate([grp(0), grp(3), grp(4), grp(7), grp(8), wi_pad], axis=1).astype(BF16)
    wt = jnp.concatenate([grp(1), grp(2), grp(5), grp(6), ki, ki], axis=1).T.astype(BF16)
    gains_p = jnp.stack([tile(qn_a), tile(qn_b)]).astype(F32)
    gains_t = jnp.stack([tile(kn_a), tile(kn_b)]).astype(F32)[:, :, None]
    ws = jnp.concatenate([w_in[:, :9 * D_ATT + D_IDX], jnp.zeros((D_MODEL, LANES - D_IDX), w_in.dtype), wi_pad],
                         axis=1).astype(BF16)
    gains_s = jnp.stack([tile(qn_a), tile(kn_a), tile(qn_b), tile(kn_b)]).astype(F32)
    return wn, wt, gains_p, gains_t, ws, gains_s, ones_bd


def kernel(x_prompt, x_sample, cache_k_a, cache_v_a, cache_k_b, cache_v_b, cache_kidx_b, state_conv, page_table,
           norm_g, rel_bias_table, w_in_attn, q_norm_a, k_norm_a, q_norm_b, k_norm_b, w_out_attn,
           w_in_conv, conv_w, conv_b, conv_ln_g, conv_ln_b, w_out_conv):
    batch, seq, _ = x_prompt.shape
    n_seq, n_tok, _ = x_sample.shape
    n_pages = page_table.shape[1]
    past = n_pages * PAGE_SIZE
    assert seq % TQ == 0 and past % MOBA_BLOCK == 0 and n_tok <= NEW_PAD and TQ >= MAX_DISTANCE
    n_hist = CONV_WIDTH - 1
    wn, wt, gains_p, gains_t, ws, gains_s, ones_bd = _prep_attn_weights(
        w_in_attn[0], q_norm_a[0], k_norm_a[0], q_norm_b[0], k_norm_b[0])
    g0 = norm_g[0][None, :]
    g1 = norm_g[1][None, :]
    wo = w_out_attn[0].astype(BF16)
    wc = w_in_conv[0].astype(BF16)
    wo2 = w_out_conv[0].astype(BF16)
    cw = jnp.concatenate([conv_w[0], jnp.zeros((HIST - CONV_WIDTH, D_MODEL), F32)], axis=0)
    conv_rows = (cw, conv_b[0][None, :], conv_ln_g[0][None, :], conv_ln_b[0][None, :])

    bd = _bias_by_distance(rel_bias_table, 2 * TQ)
    c31 = rel_bias_table[N_BUCKETS - 1]
    r = np.arange(TQ)
    dist = np.stack([np.maximum(r[:, None] - r[None, :], 0), TQ + r[:, None] - r[None, :]])
    bias_tiles = jnp.transpose(bd[dist], (3, 0, 1, 2))

    xp = x_prompt.reshape(batch * seq, D_MODEL)
    (qa, qb, qi, wi, ga, gb, kat, vat, kbt, vbt, kkt, ka32, va32, kb32, vb32, ki32) = _project_prompt(
        xp, g0, wn, wt, gains_p, gains_t, ones_bd, batch, seq)
    oa = _moba_prompt(c31, qa, kat, vat, bias_tiles, batch, seq)
    ob = _dsa_prompt(c31, qb, qi, wi, kkt, kbt, vbt, bias_tiles, batch, seq)
    y_p, st_p = _layer1_prompt(xp, oa, ob, ga, gb, wo, g1, wc, *conv_rows, wo2, batch, seq, tt=TQ)
    heads_p = lambda a: jnp.transpose(a.reshape(batch, N_HEADS, HEAD_DIM, seq), (0, 3, 1, 2))[None]
    ki_p = jnp.transpose(ki32, (0, 2, 1))[None]
    cv_p = st_p[None, :, HIST - n_hist:, :]

    rows = n_tok * N_HEADS
    xs = x_sample.reshape(n_seq * n_tok, D_MODEL)
    (qa_s, ka_s, va_s, qb_s, kb_s, vb_s, qi_s, wi_s, ga_s, gb_s, ka32_s, va32_s, kb32_s, vb32_s, ki32_s) = \
        _project_sample(xs, g0, ws, gains_s, ones_bd)
    head_of_lane = np.arange(D_ATT) // HEAD_DIM
    hmask8 = (np.arange(N_HEADS)[:, None] == head_of_lane[None, :])
    hmask = jnp.asarray(np.tile(hmask8, (n_tok, 1)), F32)
    qbd = lambda q: (q.reshape(n_seq, n_tok, 1, D_ATT) * jnp.asarray(hmask8, BF16)).reshape(n_seq, rows, D_ATT)
    newpad = lambda a: jnp.pad(a.reshape(n_seq, n_tok, D_ATT), ((0, 0), (0, NEW_PAD - n_tok), (0, 0)))
    t_of_row = np.arange(rows) // N_HEADS
    h_of_row = np.arange(rows) % N_HEADS
    c_idx = np.arange(PAGE_SIZE)
    dist_last = PAGE_SIZE + t_of_row[:, None] - c_idx[None, :]
    tn = np.arange(NEW_PAD)
    dist_new = np.maximum(t_of_row[:, None] - tn[None, :], 0)
    valid_new = (tn[None, :] <= t_of_row[:, None]) & (tn[None, :] < n_tok)

    def sample_bias(off):
        hh = h_of_row + off
        blast = bd[dist_last, hh[:, None]]
        bnew = jnp.where(valid_new, bd[dist_new, hh[:, None]], NEG_INF)
        return blast, c31[hh][:, None], bnew

    pages = lambda c: jnp.transpose(c[0], (0, 2, 3, 1)).reshape(c.shape[1], D_ATT, PAGE_SIZE)
    kit = jnp.transpose(cache_kidx_b[0], (0, 2, 1))
    blast_a, c31_a, bnew_a = sample_bias(0)
    oa_s = _sample_moba(page_table, pages(cache_k_a), pages(cache_v_a), qbd(qa_s), blast_a, c31_a,
                        newpad(ka_s), newpad(va_s), bnew_a, hmask, n_tok)
    k_top = min(DSA_TOPK_MAX, (past + n_tok) // 4)
    knew_i = jnp.pad(jnp.transpose(ki32_s.reshape(n_seq, n_tok, D_IDX), (0, 2, 1)),
                     ((0, 0), (0, 0), (0, PAGE_SIZE - n_tok))).astype(BF16)
    pen = _sample_index(page_table, kit, qi_s.reshape(n_seq, rows, D_IDX),
                        wi_s[:, :H_IDX].reshape(n_seq, rows, 1), knew_i, n_tok, k_top)
    blast_b, c31_b, bnew_b = sample_bias(N_HEADS)
    ob_s = _sample_dsa(page_table, pages(cache_k_b), pages(cache_v_b), qbd(qb_s), pen, blast_b, c31_b,
                       newpad(kb_s), newpad(vb_s), bnew_b, hmask, n_tok)
    tmaj = lambda a: jnp.transpose(a.reshape(n_seq, n_tok, -1), (1, 0, 2)).reshape(n_tok * n_seq, -1)
    st_in = jnp.transpose(state_conv[0], (1, 0, 2))
    y_s, ns = _layer1_sample(tmaj(xs), tmaj(oa_s), tmaj(ob_s), tmaj(ga_s), tmaj(gb_s), wo, g1, wc,
                             *conv_rows, wo2, st_in, n_tok, n_seq)
    y_s = jnp.transpose(y_s.reshape(n_tok, n_seq, D_MODEL), (1, 0, 2))
    cv_s = jnp.transpose(ns, (1, 0, 2))[None]
    heads_s = lambda a: a.reshape(1, n_seq, n_tok, N_HEADS, HEAD_DIM)

    return (y_p.reshape(batch, seq, D_MODEL), y_s,
            heads_p(ka32), heads_p(va32), heads_p(kb32), heads_p(vb32), ki_p, cv_p,
            heads_s(ka32_s), heads_s(va32_s), heads_s(kb32_s), heads_s(vb32_s),
            ki32_s.reshape(1, n_seq, n_tok, D_IDX), cv_s)
```

```python
import functools
import math

import numpy as np
import jax
import jax.numpy as jnp
from jax import lax
from jax.experimental import pallas as pl
from jax.experimental.pallas import tpu as pltpu

F32 = jnp.float32
BF16 = jnp.bfloat16
I32 = jnp.int32

D_MODEL = 1024
HEAD_DIM = 64
N_HEADS = 8
D_ATT = N_HEADS * HEAD_DIM
MOBA_BLOCK = 256
MOBA_TOPK = 3
DSA_TOPK_MAX = 256
H_IDX = 8
D_IDX = 64
CONV_WIDTH = 31
N_BUCKETS = 32
MAX_DISTANCE = 128
PAGE_SIZE = 128
RMS_EPS = 1e-6
LN_EPS = 1e-5

LANES = 128
SUBLANES = 8
TQ = MOBA_BLOCK
HIST = 32
NEW_PAD = 16
PAGES_PER_STEP = 8
IDX_PAGES_PER_STEP = 16
INT_MIN = -2 ** 31
NEG_INF = float("-inf")
VMEM_LIMIT = 56 * 1024 * 1024


def _dot(a, b):
    return jnp.dot(a, b, preferred_element_type=F32)


def _dot_nt(a, b):
    return lax.dot_general(a, b, (((1,), (1,)), ((), ())), preferred_element_type=F32)


def _sigmoid(x):
    return 1.0 / (1.0 + jnp.exp(-x))


def _silu(x):
    return x * _sigmoid(x)


def _rms(x, g):
    return x * lax.rsqrt(jnp.mean(x * x, axis=-1, keepdims=True) + RMS_EPS) * g


def _osm_update(scores, pvs, m, l, acc, axis):
    m_new = m
    for s in scores:
        m_new = jnp.maximum(m_new, jnp.max(s, axis=axis, keepdims=True))
    m_safe = jnp.where(m_new == NEG_INF, 0.0, m_new)
    alpha = jnp.exp(m - m_safe)
    l = alpha * l
    acc = alpha * acc
    for s, pv in zip(scores, pvs):
        p = jnp.exp(s - m_safe)
        l = l + jnp.sum(p, axis=axis, keepdims=True)
        acc = acc + pv(p.astype(BF16))
    return m_new, l, acc


def _top_blocks(g, idx, axis):
    picks = []
    for _ in range(MOBA_TOPK):
        mx = jnp.max(g, axis=axis, keepdims=True)
        a = jnp.min(jnp.where(g == mx, idx, LANES * LANES), axis=axis, keepdims=True)
        a = jnp.where(mx == NEG_INF, -1, a)
        picks.append(a)
        g = jnp.where(idx == a, NEG_INF, g)
    return picks


def _f32_key(x):
    b = lax.bitcast_convert_type(x, I32)
    k = b ^ ((b >> 31) & 0x7FFFFFFF)
    return jnp.where(x == 0.0, 0, k)


def _select_threshold(count_fn, k_top, n_keys):
    k_f = float(k_top)
    t = jnp.where(count_fn(lambda kt, pos: kt >= 0) >= k_f, 0, INT_MIN).astype(I32)

    def bit_step(b, t):
        cand = t | jnp.left_shift(jnp.int32(1), 30 - b)
        return jnp.where(count_fn(lambda kt, pos: kt >= cand) >= k_f, cand, t)

    t = lax.fori_loop(0, 31, bit_step, t)
    t = jnp.maximum(t, INT_MIN + 1)
    cnt_ge = count_fn(lambda kt, pos: kt >= t)
    need = k_f - count_fn(lambda kt, pos: kt > t)
    n_bits = max(1, int(math.ceil(math.log2(n_keys))))

    def tie_search():
        def pos_step(b, x):
            cand = x | jnp.left_shift(jnp.int32(1), n_bits - 1 - b)
            below = count_fn(lambda kt, pos: (kt == t) & (pos < cand))
            return jnp.where(below < need, cand, x)
        return lax.fori_loop(0, n_bits, pos_step, jnp.zeros_like(t))

    any_excess = jnp.max(jnp.where(cnt_ge > k_f, 1.0, 0.0)) > 0.0
    x = lax.cond(any_excess, tie_search, lambda: jnp.full_like(t, n_keys))
    return t, x


def _params(n):
    return pltpu.CompilerParams(dimension_semantics=("arbitrary",) * n, vmem_limit_bytes=VMEM_LIMIT)


def _bucket_thresholds():
    exact = N_BUCKETS // 2
    n = np.arange(MAX_DISTANCE + 1)
    nf = np.maximum(n, 1).astype(np.float32)
    large = exact + (np.log(nf / np.float32(exact)) / np.float32(math.log(MAX_DISTANCE / exact))
                     * np.float32(N_BUCKETS - exact)).astype(np.int32)
    bucket = np.where(n < exact, n, np.minimum(large, N_BUCKETS - 1))
    assert bucket[MAX_DISTANCE] == N_BUCKETS - 1 and np.all(np.diff(bucket) >= 0)
    return [int(np.argmax(bucket >= k)) for k in range(N_BUCKETS)]


def _bias_tiles_kernel(tab_ref, o_ref):
    h = pl.program_id(0)
    thr = _bucket_thresholds()
    key = lax.broadcasted_iota(I32, (TQ, TQ), 0)
    qry = lax.broadcasted_iota(I32, (TQ, TQ), 1)
    for d in range(2):
        dist = d * TQ + qry - key
        val = jnp.full((TQ, TQ), tab_ref[0, h], F32)
        for k in range(1, N_BUCKETS):
            val = jnp.where(dist >= thr[k], tab_ref[k, h], val)
        o_ref[d] = val


def _bias_tiles(table):
    n_h = table.shape[1]
    return pl.pallas_call(
        _bias_tiles_kernel,
        grid=(n_h,),
        in_specs=[pl.BlockSpec(memory_space=pltpu.SMEM)],
        out_specs=pl.BlockSpec((None, 2, TQ, TQ), lambda h: (h, 0, 0, 0)),
        out_shape=jax.ShapeDtypeStruct((n_h, 2, TQ, TQ), F32),
        compiler_params=_params(1), name="bias_tiles",
    )(table)


def _proj_prompt_kernel(x_ref, g_ref, wn_ref, wt_ref, gains_ref,
                        ga_ref, gb_ref, katm_ref, kbtm_ref, kktm_ref,
                        qat_ref, qbt_ref, qit_ref, wit_ref, vat_ref, vbt_ref,
                        ka32_ref, va32_ref, kb32_ref, vb32_ref, ki32_ref):
    h = _rms(x_ref[...], g_ref[...]).astype(BF16)
    tm = h.shape[0]
    scale = HEAD_DIM ** -0.5

    def grp_t(c):
        return _dot_nt(wt_ref[c * D_ATT:(c + 1) * D_ATT, :], h)

    def headnorm_t(zt, row):
        z3 = zt.reshape(N_HEADS, HEAD_DIM, tm)
        ssq = jnp.sum(z3 * z3, axis=1, keepdims=True)
        z3 = z3 * lax.rsqrt(ssq * (1.0 / HEAD_DIM) + RMS_EPS)
        return z3.reshape(D_ATT, tm) * gains_ref[row]

    ga_ref[...] = _dot(h, wn_ref[:, :D_ATT])
    gb_ref[...] = _dot(h, wn_ref[:, D_ATT:])
    qat_ref[...] = (headnorm_t(grp_t(0), 0) * scale).astype(BF16)
    kat = headnorm_t(grp_t(1), 1)
    ka32_ref[...] = kat
    katm_ref[...] = kat.T.astype(BF16)
    vat = grp_t(2)
    va32_ref[...] = vat
    vat_ref[...] = vat.astype(BF16)
    qbt_ref[...] = (headnorm_t(grp_t(3), 2) * scale).astype(BF16)
    kbt = headnorm_t(grp_t(4), 3)
    kb32_ref[...] = kbt
    kbtm_ref[...] = kbt.T.astype(BF16)
    vbt = grp_t(5)
    vb32_ref[...] = vbt
    vbt_ref[...] = vbt.astype(BF16)
    qit_ref[...] = grp_t(6).astype(BF16)
    kk = _dot_nt(wt_ref[7 * D_ATT:7 * D_ATT + LANES, :], h)
    ki32_ref[...] = kk[:D_IDX, :]
    kktm_ref[...] = kk.T.astype(BF16)
    wit_ref[...] = _dot_nt(wt_ref[7 * D_ATT + LANES:, :], h)[:H_IDX, :]


def _project_prompt(x, g, wn, wt, gains_t, batch, seq):
    tm = TQ
    n_t = seq // tm
    n = batch * seq
    row = lambda c: pl.BlockSpec((tm, c), lambda b, t: (b * n_t + t, 0))
    full = lambda a: pl.BlockSpec(a.shape, lambda b, t: (0,) * a.ndim)
    tiles = lambda r: pl.BlockSpec((None, None, r, tm), lambda b, t: (b, t, 0, 0))
    cols = lambda r: pl.BlockSpec((None, r, tm), lambda b, t: (b, 0, t))
    sd = jax.ShapeDtypeStruct
    tsd = lambda r, dt: sd((batch, n_t, r, tm), dt)
    out_shape = ([sd((n, D_ATT), F32)] * 2 + [sd((n, D_ATT), BF16)] * 2 + [sd((n, LANES), BF16)]
                 + [tsd(D_ATT, BF16)] * 3 + [tsd(H_IDX, F32)] + [tsd(D_ATT, BF16)] * 2
                 + [sd((batch, D_ATT, seq), F32)] * 4 + [sd((batch, D_IDX, seq), F32)])
    out_specs = ([row(D_ATT)] * 4 + [row(LANES)] + [tiles(D_ATT)] * 3 + [tiles(H_IDX)] + [tiles(D_ATT)] * 2
                 + [cols(D_ATT)] * 4 + [cols(D_IDX)])
    return pl.pallas_call(
        _proj_prompt_kernel,
        grid=(batch, n_t),
        in_specs=[row(D_MODEL), full(g), full(wn), full(wt), full(gains_t)],
        out_specs=out_specs, out_shape=out_shape,
        compiler_params=_params(2), name="proj_prompt",
    )(x, g, wn, wt, gains_t)


def _headnorm_rows(z, ones_ref, gain):
    ssq = _dot((z * z).astype(BF16), ones_ref[...])
    return z * lax.rsqrt(ssq * (1.0 / HEAD_DIM) + RMS_EPS) * gain


def _proj_sample_kernel(x_ref, g_ref, w_ref, gains_ref, ones_ref,
                        qa_ref, ka_ref, va_ref, qb_ref, kb_ref, vb_ref, qi_ref, wi_ref,
                        ga_ref, gb_ref, ka32_ref, va32_ref, kb32_ref, vb32_ref, ki32_ref):
    h = _rms(x_ref[...], g_ref[...]).astype(BF16)
    scale = HEAD_DIM ** -0.5

    def grp(c):
        return _dot(h, w_ref[:, c * D_ATT:(c + 1) * D_ATT])

    qa_ref[...] = (_headnorm_rows(grp(0), ones_ref, gains_ref[0:1, :]) * scale).astype(BF16)
    ka = _headnorm_rows(grp(1), ones_ref, gains_ref[1:2, :])
    ka32_ref[...] = ka
    ka_ref[...] = ka.astype(BF16)
    va = grp(2)
    va32_ref[...] = va
    va_ref[...] = va.astype(BF16)
    ga_ref[...] = grp(3)
    qb_ref[...] = (_headnorm_rows(grp(4), ones_ref, gains_ref[2:3, :]) * scale).astype(BF16)
    kb = _headnorm_rows(grp(5), ones_ref, gains_ref[3:4, :])
    kb32_ref[...] = kb
    kb_ref[...] = kb.astype(BF16)
    vb = grp(6)
    vb32_ref[...] = vb
    vb_ref[...] = vb.astype(BF16)
    gb_ref[...] = grp(7)
    qi_ref[...] = grp(8).astype(BF16)
    tail = _dot(h, w_ref[:, 9 * D_ATT:])
    ki32_ref[...] = tail[:, :D_IDX]
    wi_ref[...] = tail[:, LANES:]


def _project_sample(x, g, w, gains, ones_bd):
    n = x.shape[0]
    full = lambda a: pl.BlockSpec(a.shape, lambda i: (0,) * a.ndim)
    sd = lambda c, dt: jax.ShapeDtypeStruct((n, c), dt)
    out_shape = ([sd(D_ATT, BF16)] * 7 + [sd(LANES, F32)] + [sd(D_ATT, F32)] * 6 + [sd(D_IDX, F32)])
    return pl.pallas_call(
        _proj_sample_kernel,
        grid=(1,),
        in_specs=[full(x), full(g), full(w), full(gains), full(ones_bd)],
        out_specs=[full(s) for s in out_shape], out_shape=out_shape,
        compiler_params=_params(1), name="proj_sample",
    )(x, g, w, gains, ones_bd)


def _head_row_masks():
    r = lax.broadcasted_iota(I32, (LANES, 1), 0)
    return r < HEAD_DIM, r >= HEAD_DIM


def _moba_prompt_kernel(c31_ref, qt_ref, k_ref, vt_ref, bias_ref, o_ref, kmean_ref, *, n_blk):
    hp = pl.program_id(1)
    i = pl.program_id(2)
    n_sub = kmean_ref.shape[0]

    @pl.when(i == 0)
    def _():
        kmean_ref[...] = jnp.zeros_like(kmean_ref)
        for j in range(n_blk):
            kb = k_ref[j * TQ:(j + 1) * TQ, :].astype(F32)
            kmean_ref[j:j + 1, :] = jnp.mean(kb, axis=0, keepdims=True)

    masks = _head_row_masks()
    qt = qt_ref[...]
    key = lax.broadcasted_iota(I32, (TQ, TQ), 0)
    qry = lax.broadcasted_iota(I32, (TQ, TQ), 1)
    blk = lax.broadcasted_iota(I32, (n_sub, TQ), 0)
    kmean = kmean_ref[...].astype(BF16)
    qms = [jnp.where(masks[half], qt, jnp.zeros_like(qt)) for half in range(2)]
    picks = [_top_blocks(jnp.where(blk < i, _dot(kmean, qm), NEG_INF), blk, axis=0) for qm in qms]
    c31 = [c31_ref[hp * 2 + half] for half in range(2)]

    def update(tiles, carry):
        out = []
        for half in range(2):
            vrows = slice(half * HEAD_DIM, (half + 1) * HEAD_DIM)
            scores, pvs = [], []
            for j, bias_of, mask_of in tiles:
                off = pl.multiple_of(j * TQ, TQ)
                s = _dot(k_ref[pl.ds(off, TQ), :], qms[half]) + bias_of(half)
                scores.append(jnp.where(mask_of(half, j), s, NEG_INF))
                pvs.append(lambda p, j=j, vrows=vrows: _dot(vt_ref[j, vrows, :], p))
            out.append(_osm_update(scores, pvs, *carry[half], axis=0))
        return tuple(out)

    def picked(half, j):
        return (picks[half][0] == j) | (picks[half][1] == j) | (picks[half][2] == j)

    init = (jnp.full((1, TQ), NEG_INF, F32), jnp.zeros((1, TQ), F32), jnp.zeros((HEAD_DIM, TQ), F32))
    carry = update([(i, lambda half: bias_ref[half, 0], lambda half, j: key <= qry)], (init, init))
    n_far = jnp.maximum(i - 1, 0)
    carry = lax.fori_loop(
        n_far, i, lambda j, c: update([(j, lambda half: bias_ref[half, 1], picked)], c), carry)
    far = lambda half: c31[half]
    done = 0
    for width in (4, 2, 1):
        n_w = (n_far - done) // width
        carry = lax.fori_loop(
            0, n_w,
            lambda t, c, done=done, width=width: update(
                [(done + width * t + w, far, picked) for w in range(width)], c),
            carry)
        done = done + n_w * width
    o_ref[...] = jnp.concatenate([c[2] / c[1] for c in carry], axis=0).T


def _moba_prompt(c31, qt, k, vt, bias, batch, seq):
    n_q = seq // TQ
    n_sub = -(-n_q // SUBLANES) * SUBLANES
    kern = functools.partial(_moba_prompt_kernel, n_blk=n_q)
    return pl.pallas_call(
        kern,
        grid=(batch, N_HEADS // 2, n_q),
        in_specs=[
            pl.BlockSpec(memory_space=pltpu.SMEM),
            pl.BlockSpec((None, None, LANES, TQ), lambda b, hp, i: (b, i, hp, 0)),
            pl.BlockSpec((seq, LANES), lambda b, hp, i: (b, hp)),
            pl.BlockSpec((None, n_q, LANES, TQ), lambda b, hp, i: (b, 0, hp, 0)),
            pl.BlockSpec((2, 2, TQ, TQ), lambda b, hp, i: (hp, 0, 0, 0)),
        ],
        out_specs=pl.BlockSpec((TQ, LANES), lambda b, hp, i: (b * n_q + i, hp)),
        out_shape=jax.ShapeDtypeStruct((batch * seq, D_ATT), F32),
        scratch_shapes=[pltpu.VMEM((n_sub, LANES), F32)],
        compiler_params=_params(3), name="moba_prompt",
    )(c31, qt, k, vt, bias)


def _dsa_prompt_kernel(c31_ref, qt_ref, qit_ref, wit_ref, kk_ref, k_ref, vt_ref, bias_ref, o_ref,
                       keys_ref, m_ref, l_ref, acc_ref, *, seq, k_top):
    i = pl.program_id(1)
    masks = _head_row_masks()
    key = lax.broadcasted_iota(I32, (TQ, TQ), 0)
    qry = lax.broadcasted_iota(I32, (TQ, TQ), 1)
    zero_b = jnp.zeros((LANES, TQ), BF16)

    def head_qt(ref, h):
        pair, half = divmod(h, 2)
        return jnp.where(masks[half], ref[pair * LANES:(pair + 1) * LANES, :], zero_b)

    def index_tile(j):
        kkj = kk_ref[pl.ds(pl.multiple_of(j * TQ, TQ), TQ), :]
        acc = jnp.zeros((TQ, TQ), F32)
        for h in range(H_IDX):
            acc = acc + wit_ref[h:h + 1, :] * jnp.maximum(_dot(kkj, head_qt(qit_ref, h)), 0.0)
        return _f32_key(acc)

    def index_body(j, c):
        keys_ref[j] = index_tile(j)
        return c

    lax.fori_loop(0, i, index_body, 0)
    keys_ref[i] = jnp.where(key <= qry, index_tile(i), INT_MIN)

    def count_fn(pred):
        def body(j, part):
            w = jnp.where(pred(keys_ref[j], key + j * TQ), 1.0, 0.0)
            return part + jnp.sum(w.reshape(TQ // SUBLANES, SUBLANES, TQ), axis=0)
        part = lax.fori_loop(0, i + 1, body, jnp.zeros((SUBLANES, TQ), F32))
        return jnp.sum(part, axis=0, keepdims=True)

    t, x = _select_threshold(count_fn, k_top, seq)

    m_ref[...] = jnp.full(m_ref.shape, NEG_INF, F32)
    l_ref[...] = jnp.zeros(l_ref.shape, F32)
    acc_ref[...] = jnp.zeros(acc_ref.shape, F32)

    def attend(js, bias_of):
        pens = []
        for j in js:
            kt = keys_ref[j]
            sel = (kt > t) | ((kt == t) & (key + j * TQ <= x))
            pens.append(jnp.where(sel, 0.0, NEG_INF))
        for h in range(N_HEADS):
            pair = h // 2
            vrows = slice(h * HEAD_DIM, (h + 1) * HEAD_DIM)
            qm = head_qt(qt_ref, h)
            scores, pvs = [], []
            for j, pen in zip(js, pens):
                off = pl.multiple_of(j * TQ, TQ)
                scores.append(_dot(k_ref[pl.ds(off, TQ), pair * LANES:(pair + 1) * LANES], qm) + bias_of(h) + pen)
                pvs.append(lambda p, j=j, vrows=vrows: _dot(vt_ref[j, vrows, :], p))
            m, l, acc = _osm_update(scores, pvs, m_ref[h], l_ref[h], acc_ref[h], axis=0)
            m_ref[h] = m
            l_ref[h] = l
            acc_ref[h] = acc

    def attend_loop(lo, n, width, bias_of):
        def body(u, c):
            attend([lo + width * u + w for w in range(width)], bias_of)
            return c
        lax.fori_loop(0, n, body, 0)

    n_far = jnp.maximum(i - 1, 0)
    attend([i], lambda h: bias_ref[h, 0])
    attend_loop(n_far, i - n_far, 1, lambda h: bias_ref[h, 1])
    far = lambda h: c31_ref[N_HEADS + h]
    attend_loop(0, n_far // 2, 2, far)
    attend_loop((n_far // 2) * 2, n_far % 2, 1, far)
    ot = jnp.concatenate([acc_ref[h] / l_ref[h] for h in range(N_HEADS)], axis=0)
    o_ref[...] = ot.T


def _dsa_prompt(c31, qt, qit, wit, kk, k, vt, bias, batch, seq):
    n_q = seq // TQ
    k_top = min(DSA_TOPK_MAX, seq // 4)
    kern = functools.partial(_dsa_prompt_kernel, seq=seq, k_top=k_top)
    tile = lambda r: pl.BlockSpec((None, None, r, TQ), lambda b, i: (b, i, 0, 0))
    rows = lambda c: pl.BlockSpec((seq, c), lambda b, i: (b, 0))
    return pl.pallas_call(
        kern,
        grid=(batch, n_q),
        in_specs=[
            pl.BlockSpec(memory_space=pltpu.SMEM),
            tile(D_ATT), tile(D_ATT), tile(H_IDX),
            rows(LANES), rows(D_ATT),
            pl.BlockSpec((None, n_q, D_ATT, TQ), lambda b, i: (b, 0, 0, 0)),
            pl.BlockSpec((N_HEADS, 2, TQ, TQ), lambda b, i: (1, 0, 0, 0)),
        ],
        out_specs=pl.BlockSpec((TQ, D_ATT), lambda b, i: (b * n_q + i, 0)),
        out_shape=jax.ShapeDtypeStruct((batch * seq, D_ATT), F32),
        scratch_shapes=[
            pltpu.VMEM((n_q, TQ, TQ), I32),
            pltpu.VMEM((N_HEADS, 1, TQ), F32),
            pltpu.VMEM((N_HEADS, 1, TQ), F32),
            pltpu.VMEM((N_HEADS, HEAD_DIM, TQ), F32),
        ],
        compiler_params=_params(2), name="dsa_prompt",
    )(c31, qt, qit, wit, kk, k, vt, bias)


def _expand_tokens(a, n_tok):
    return jnp.broadcast_to(a[:, None, :], (n_tok, N_HEADS, a.shape[-1])).reshape(n_tok * N_HEADS, a.shape[-1])


def _collapse_heads(o, hmask, n_tok):
    return jnp.sum((o * hmask).reshape(n_tok, N_HEADS, D_ATT), axis=1)


def _sample_index_kernel(pt_ref, *refs, n_pages, n_tok, k_top, n_pp):
    kit_refs = refs[:n_pp]
    qi_ref, wcol_ref, knew_ref, pen_ref, keys_ref = refs[n_pp:]
    g = pl.program_id(1)

    def keys_of(kt):
        x = jnp.maximum(_dot(qi_ref[...], kt), 0.0) * wcol_ref[...]
        return _f32_key(jnp.sum(x.reshape(n_tok, H_IDX, LANES), axis=1))

    for o in range(n_pp):
        keys_ref[g * n_pp + o] = keys_of(kit_refs[o][...].astype(BF16))

    @pl.when(g == n_pages // n_pp - 1)
    def _():
        lane = lax.broadcasted_iota(I32, (n_tok, LANES), 1)
        tok = lax.broadcasted_iota(I32, (n_tok, LANES), 0)
        keys_ref[n_pages] = jnp.where(lane <= tok, keys_of(knew_ref[...]), INT_MIN)

        def count_fn(pred):
            def body(j, part):
                return part + jnp.where(pred(keys_ref[j], lane + j * LANES), 1.0, 0.0)
            part = lax.fori_loop(0, n_pages + 1, body, jnp.zeros((n_tok, LANES), F32))
            return jnp.sum(part, axis=1, keepdims=True)

        t, x = _select_threshold(count_fn, k_top, (n_pages + 1) * LANES)

        def write(j, c):
            kt = keys_ref[j]
            sel = (kt > t) | ((kt == t) & (lane + j * LANES <= x))
            pen_ref[j] = jnp.where(sel, 0.0, NEG_INF)
            return c

        lax.fori_loop(0, n_pages + 1, write, 0)


def _sample_index(page_table, kit, qi, wcol, knew, n_tok, k_top):
    n_seq, n_pages = page_table.shape
    n_pp = math.gcd(IDX_PAGES_PER_STEP, n_pages)
    kern = functools.partial(_sample_index_kernel, n_pages=n_pages, n_tok=n_tok, k_top=k_top, n_pp=n_pp)
    per_seq = lambda a: pl.BlockSpec((None,) + a.shape[1:], lambda b, g, pt: (b,) + (0,) * (a.ndim - 1))
    page = lambda o: pl.BlockSpec((None, D_IDX, PAGE_SIZE), lambda b, g, pt: (pt[b, g * n_pp + o], 0, 0))
    grid_spec = pltpu.PrefetchScalarGridSpec(
        num_scalar_prefetch=1, grid=(n_seq, n_pages // n_pp),
        in_specs=[page(o) for o in range(n_pp)] + [per_seq(qi), per_seq(wcol), per_seq(knew)],
        out_specs=pl.BlockSpec((None, n_pages + 1, n_tok, LANES), lambda b, g, pt: (b, 0, 0, 0)),
        scratch_shapes=[pltpu.VMEM((n_pages + 1, n_tok, LANES), I32)])
    return pl.pallas_call(
        kern, grid_spec=grid_spec,
        out_shape=jax.ShapeDtypeStruct((n_seq, n_pages + 1, n_tok, LANES), F32),
        compiler_params=_params(2), name="sample_index",
    )(page_table, *([kit] * n_pp), qi, wcol, knew)


def _sample_dsa_kernel(pt_ref, *refs, n_pages, n_tok, n_pp):
    kt_refs, vt_refs = refs[:n_pp], refs[n_pp:2 * n_pp]
    (q_ref, pen_ref, blast_ref, c31_ref, knew_ref, vnew_ref, bnew_ref, hmask_ref,
     o_ref, m_ref, l_ref, acc_ref) = refs[2 * n_pp:]
    g = pl.program_id(1)
    last = g == n_pages // n_pp - 1

    @pl.when(g == 0)
    def _():
        m_ref[...] = jnp.full(m_ref.shape, NEG_INF, F32)
        l_ref[...] = jnp.zeros(l_ref.shape, F32)
        acc_ref[...] = jnp.zeros(acc_ref.shape, F32)

    q = q_ref[...]
    scores, pvs = [], []
    for o in range(n_pp):
        bias = c31_ref[...] if o < n_pp - 1 else jnp.where(last, blast_ref[...], c31_ref[...])
        scores.append(_dot(q, kt_refs[o][...].astype(BF16)) + bias + _expand_tokens(pen_ref[g * n_pp + o], n_tok))
        pvs.append(lambda pp, o=o: _dot_nt(pp, vt_refs[o][...].astype(BF16)))
    m, l, acc = _osm_update(scores, pvs, m_ref[...], l_ref[...], acc_ref[...], axis=1)
    m_ref[...] = m
    l_ref[...] = l
    acc_ref[...] = acc

    @pl.when(last)
    def _():
        pen_new = _expand_tokens(pen_ref[n_pages][:, :NEW_PAD], n_tok)
        s_n = _dot_nt(q, knew_ref[...]) + bnew_ref[...] + pen_new
        _, l2, acc2 = _osm_update([s_n], [lambda pp: _dot(pp, vnew_ref[...])], m, l, acc, axis=1)
        o_ref[...] = _collapse_heads(acc2 / l2, hmask_ref[...], n_tok)


def _sample_dsa(page_table, kt, vt, q, pen, blast, c31col, knew, vnew, bnew, hmask, n_tok):
    n_seq, n_pages = page_table.shape
    n_pp = math.gcd(PAGES_PER_STEP, n_pages)
    rows = n_tok * N_HEADS
    kern = functools.partial(_sample_dsa_kernel, n_pages=n_pages, n_tok=n_tok, n_pp=n_pp)
    per_seq = lambda a: pl.BlockSpec((None,) + a.shape[1:], lambda b, g, pt: (b,) + (0,) * (a.ndim - 1))
    full = lambda a: pl.BlockSpec(a.shape, lambda b, g, pt: (0,) * a.ndim)
    page = lambda o: pl.BlockSpec((None, D_ATT, PAGE_SIZE), lambda b, g, pt: (pt[b, g * n_pp + o], 0, 0))
    pages = [page(o) for o in range(n_pp)]
    grid_spec = pltpu.PrefetchScalarGridSpec(
        num_scalar_prefetch=1, grid=(n_seq, n_pages // n_pp),
        in_specs=pages + pages + [per_seq(q), per_seq(pen), full(blast), full(c31col),
                                  per_seq(knew), per_seq(vnew), full(bnew), full(hmask)],
        out_specs=pl.BlockSpec((None, n_tok, D_ATT), lambda b, g, pt: (b, 0, 0)),
        scratch_shapes=[pltpu.VMEM((rows, 1), F32), pltpu.VMEM((rows, 1), F32), pltpu.VMEM((rows, D_ATT), F32)])
    return pl.pallas_call(
        kern, grid_spec=grid_spec,
        out_shape=jax.ShapeDtypeStruct((n_seq, n_tok, D_ATT), F32),
        compiler_params=_params(2), name="sample_dsa",
    )(page_table, *([kt] * n_pp), *([vt] * n_pp), q, pen, blast, c31col, knew, vnew, bnew, hmask)


def _sample_moba_kernel(pt_ref, *refs, n_blk, n_tok, n_pp):
    kt_refs, vt_refs = refs[:n_pp], refs[n_pp:2 * n_pp]
    (q_ref, blast_ref, c31_ref, knew_ref, vnew_ref, bnew_ref, hmask_ref,
     o_ref, gate_ref, ms_ref, ls_ref, accs_ref) = refs[2 * n_pp:]
    g = pl.program_id(1)
    n_bb = n_pp // 2
    last = g == n_blk // n_bb - 1
    rows = n_tok * N_HEADS
    lane = lax.broadcasted_iota(I32, (rows, LANES), 1)

    @pl.when(g == 0)
    def _():
        gate_ref[...] = jnp.full(gate_ref.shape, NEG_INF, F32)
        ms_ref[...] = jnp.full(ms_ref.shape, NEG_INF, F32)
        ls_ref[...] = jnp.zeros(ls_ref.shape, F32)

    q = q_ref[...]
    gates, ms, ls = gate_ref[...], ms_ref[...], ls_ref[...]
    for bb in range(n_bb):
        j = g * n_bb + bb
        r0 = _dot(q, kt_refs[2 * bb][...].astype(BF16))
        r1 = _dot(q, kt_refs[2 * bb + 1][...].astype(BF16))
        gate = jnp.sum(r0 + r1, axis=1, keepdims=True)
        s0 = r0 + c31_ref[...]
        s1 = r1 + (c31_ref[...] if bb < n_bb - 1 else jnp.where(last, blast_ref[...], c31_ref[...]))
        m = jnp.maximum(jnp.max(s0, axis=1, keepdims=True), jnp.max(s1, axis=1, keepdims=True))
        p0 = jnp.exp(s0 - m)
        p1 = jnp.exp(s1 - m)
        l = jnp.sum(p0, axis=1, keepdims=True) + jnp.sum(p1, axis=1, keepdims=True)
        accs_ref[j] = (_dot_nt(p0.astype(BF16), vt_refs[2 * bb][...].astype(BF16))
                       + _dot_nt(p1.astype(BF16), vt_refs[2 * bb + 1][...].astype(BF16)))
        gates = jnp.where(lane == j, gate, gates)
        ms = jnp.where(lane == j, m, ms)
        ls = jnp.where(lane == j, l, ls)
    gate_ref[...] = gates
    ms_ref[...] = ms
    ls_ref[...] = ls

    @pl.when(last)
    def _():
        picks = _top_blocks(jnp.where(lane < n_blk, gates, NEG_INF), lane, axis=1)
        sel = (lane == picks[0]) | (lane == picks[1]) | (lane == picks[2])
        s_n = _dot_nt(q, knew_ref[...]) + bnew_ref[...]
        m_o = jnp.max(s_n, axis=1, keepdims=True)
        p_o = jnp.exp(s_n - m_o)
        l_o = jnp.sum(p_o, axis=1, keepdims=True)
        m_all = jnp.maximum(m_o, jnp.max(jnp.where(sel, ms, NEG_INF), axis=1, keepdims=True))
        w = jnp.where(sel, jnp.exp(ms - m_all), 0.0)
        w_o = jnp.exp(m_o - m_all)
        l_all = jnp.sum(w * ls, axis=1, keepdims=True) + w_o * l_o
        out = w_o * _dot(p_o.astype(BF16), vnew_ref[...])
        for jj in range(n_blk):
            out = out + w[:, jj:jj + 1] * accs_ref[jj]
        o_ref[...] = _collapse_heads(out / l_all, hmask_ref[...], n_tok)


def _sample_moba(page_table, kt, vt, q, blast, c31col, knew, vnew, bnew, hmask, n_tok):
    n_seq, n_pages = page_table.shape
    pages_per_blk = MOBA_BLOCK // PAGE_SIZE
    n_blk = n_pages // pages_per_blk
    n_pp = pages_per_blk * math.gcd(PAGES_PER_STEP // pages_per_blk, n_blk)
    rows = n_tok * N_HEADS
    kern = functools.partial(_sample_moba_kernel, n_blk=n_blk, n_tok=n_tok, n_pp=n_pp)
    per_seq = lambda a: pl.BlockSpec((None,) + a.shape[1:], lambda b, g, pt: (b,) + (0,) * (a.ndim - 1))
    full = lambda a: pl.BlockSpec(a.shape, lambda b, g, pt: (0,) * a.ndim)
    page = lambda o: pl.BlockSpec((None, D_ATT, PAGE_SIZE), lambda b, g, pt: (pt[b, g * n_pp + o], 0, 0))
    pages = [page(o) for o in range(n_pp)]
    grid_spec = pltpu.PrefetchScalarGridSpec(
        num_scalar_prefetch=1, grid=(n_seq, n_pages // n_pp),
        in_specs=pages + pages + [per_seq(q), full(blast), full(c31col),
                                  per_seq(knew), per_seq(vnew), full(bnew), full(hmask)],
        out_specs=pl.BlockSpec((None, n_tok, D_ATT), lambda b, g, pt: (b, 0, 0)),
        scratch_shapes=[pltpu.VMEM((rows, LANES), F32)] * 3 + [pltpu.VMEM((n_blk, rows, D_ATT), F32)])
    return pl.pallas_call(
        kern, grid_spec=grid_spec,
        out_shape=jax.ShapeDtypeStruct((n_seq, n_tok, D_ATT), F32),
        compiler_params=_params(2), name="sample_moba",
    )(page_table, *([kt] * n_pp), *([vt] * n_pp), q, blast, c31col, knew, vnew, bnew, hmask)


def _merge_and_conv_in(x_ref, oa_ref, ob_ref, ga_ref, gb_ref, wo_ref, g1_ref, wc_ref):
    o = jnp.concatenate([oa_ref[...] * _silu(ga_ref[...]), ob_ref[...] * _silu(gb_ref[...])], axis=1)
    h1 = x_ref[...] + _dot(o.astype(BF16), wo_ref[...])
    hn = _rms(h1, g1_ref[...]).astype(BF16)
    a = _dot(hn, wc_ref[:, 0:D_MODEL])
    b = _dot(hn, wc_ref[:, D_MODEL:2 * D_MODEL])
    g = _dot(hn, wc_ref[:, 2 * D_MODEL:3 * D_MODEL])
    return h1, a * _sigmoid(b), g


def _ln_gate_out(c, g, h1, lg_ref, lb_ref, wo2_ref):
    mu = jnp.mean(c, axis=-1, keepdims=True)
    d = c - mu
    var = jnp.mean(d * d, axis=-1, keepdims=True)
    cn = d * lax.rsqrt(var + LN_EPS) * lg_ref[...] + lb_ref[...]
    z = (_silu(cn) * _silu(g)).astype(BF16)
    return h1 + _dot(z, wo2_ref[...])


def _layer1_prompt_kernel(x_ref, oa_ref, ob_ref, ga_ref, gb_ref, wo_ref, g1_ref, wc_ref,
                          cw_ref, cb_ref, lg_ref, lb_ref, wo2_ref, y_ref, st_ref, up_ref, c_ref, *, tt):
    t = pl.program_id(1)

    @pl.when(t == 0)
    def _():
        up_ref[0:HIST, :] = jnp.zeros((HIST, D_MODEL), F32)

    h1, u, g = _merge_and_conv_in(x_ref, oa_ref, ob_ref, ga_ref, gb_ref, wo_ref, g1_ref, wc_ref)
    up_ref[HIST:HIST + tt, :] = u
    base = HIST - (CONV_WIDTH - 1)
    for cc in range(0, D_MODEL, 2 * LANES):
        cs = slice(cc, cc + 2 * LANES)
        acc = jnp.zeros((tt, 2 * LANES), F32) + cb_ref[:, cs]
        for w in range(CONV_WIDTH):
            acc = acc + cw_ref[w:w + 1, cs] * up_ref[base + w:base + w + tt, cs]
        c_ref[:, cs] = acc
    y_ref[...] = _ln_gate_out(c_ref[...], g, h1, lg_ref, lb_ref, wo2_ref)
    tail = up_ref[tt:tt + HIST, :]
    st_ref[...] = tail
    up_ref[0:HIST, :] = tail


def _layer1_prompt(x, oa, ob, ga, gb, wo, g1, wc, cw, cb, lg, lb, wo2, batch, seq, tt):
    n_t = seq // tt
    kern = functools.partial(_layer1_prompt_kernel, tt=tt)
    row = lambda c: pl.BlockSpec((tt, c), lambda b, t: (b * n_t + t, 0))
    full = lambda a: pl.BlockSpec(a.shape, lambda b, t: (0,) * a.ndim)
    return pl.pallas_call(
        kern,
        grid=(batch, n_t),
        in_specs=[row(D_MODEL), row(D_ATT), row(D_ATT), row(D_ATT), row(D_ATT),
                  full(wo), full(g1), full(wc), full(cw), full(cb), full(lg), full(lb), full(wo2)],
        out_specs=[row(D_MODEL), pl.BlockSpec((None, HIST, D_MODEL), lambda b, t: (b, 0, 0))],
        out_shape=[jax.ShapeDtypeStruct((batch * seq, D_MODEL), F32),
                   jax.ShapeDtypeStruct((batch, HIST, D_MODEL), F32)],
        scratch_shapes=[pltpu.VMEM((HIST + tt, D_MODEL), F32), pltpu.VMEM((tt, D_MODEL), F32)],
        compiler_params=_params(2), name="layer1_prompt",
    )(x, oa, ob, ga, gb, wo, g1, wc, cw, cb, lg, lb, wo2)


def _layer1_sample_kernel(x_ref, oa_ref, ob_ref, ga_ref, gb_ref, wo_ref, g1_ref, wc_ref,
                          cw_ref, cb_ref, lg_ref, lb_ref, wo2_ref, st_ref, y_ref, ns_ref, c_ref, *, n_tok, n_seq):
    n_hist = CONV_WIDTH - 1
    h1, u, g = _merge_and_conv_in(x_ref, oa_ref, ob_ref, ga_ref, gb_ref, wo_ref, g1_ref, wc_ref)
    u_t = [u[t * n_seq:(t + 1) * n_seq, :] for t in range(n_tok)]
    for t in range(n_tok):
        acc = jnp.zeros((n_seq, D_MODEL), F32) + cb_ref[...]
        for r in range(t, n_hist):
            acc = acc + cw_ref[r - t:r - t + 1, :] * st_ref[r]
        for r in range(t + 1):
            acc = acc + cw_ref[n_hist - t + r:n_hist - t + r + 1, :] * u_t[r]
        c_ref[t * n_seq:(t + 1) * n_seq, :] = acc
    y_ref[...] = _ln_gate_out(c_ref[...], g, h1, lg_ref, lb_ref, wo2_ref)
    for r in range(n_hist - n_tok):
        ns_ref[r] = st_ref[r + n_tok]
    for t in range(n_tok):
        ns_ref[n_hist - n_tok + t] = u_t[t]


def _layer1_sample(x, oa, ob, ga, gb, wo, g1, wc, cw, cb, lg, lb, wo2, st, n_tok, n_seq):
    kern = functools.partial(_layer1_sample_kernel, n_tok=n_tok, n_seq=n_seq)
    full = lambda a: pl.BlockSpec(a.shape, lambda i: (0,) * a.ndim)
    args = (x, oa, ob, ga, gb, wo, g1, wc, cw, cb, lg, lb, wo2, st)
    out_shape = [jax.ShapeDtypeStruct(x.shape, F32), jax.ShapeDtypeStruct(st.shape, F32)]
    return pl.pallas_call(
        kern, grid=(1,),
        in_specs=[full(a) for a in args],
        out_specs=[full(s) for s in out_shape], out_shape=out_shape,
        scratch_shapes=[pltpu.VMEM(x.shape, F32)],
        compiler_params=_params(1), name="layer1_sample",
    )(*args)


def _t5_bucket(n):
    exact = N_BUCKETS // 2
    nf = jnp.maximum(n, 1).astype(F32)
    large = exact + (jnp.log(nf / exact) / math.log(MAX_DISTANCE / exact) * (N_BUCKETS - exact)).astype(I32)
    large = jnp.minimum(large, N_BUCKETS - 1)
    return jnp.where(n < exact, n, large)


def _bias_by_distance(table, n):
    return table[_t5_bucket(jnp.arange(n, dtype=I32))]


def _prep_attn_weights(w_in, qn_a, kn_a, qn_b, kn_b):
    grp = lambda c: w_in[:, c * D_ATT:(c + 1) * D_ATT]
    ki = w_in[:, 9 * D_ATT:9 * D_ATT + D_IDX]
    wi = w_in[:, 9 * D_ATT + D_IDX:]
    zeros = lambda c: jnp.zeros((D_MODEL, c), w_in.dtype)
    tile = lambda g: jnp.tile(g, N_HEADS)
    head = np.arange(D_ATT) // HEAD_DIM
    ones_bd = jnp.asarray(head[:, None] == head[None, :], dtype=BF16)
    wn = jnp.concatenate([grp(3), grp(7)], axis=1).astype(BF16)
    wt = jnp.concatenate([grp(0), grp(1), grp(2), grp(4), grp(5), grp(6), grp(8), ki, ki, wi,
                          zeros(NEW_PAD - H_IDX)], axis=1).T.astype(BF16)
    gains_t = jnp.stack([tile(qn_a), tile(kn_a), tile(qn_b), tile(kn_b)]).astype(F32)[:, :, None]
    ws = jnp.concatenate([w_in[:, :9 * D_ATT + D_IDX], zeros(LANES - D_IDX), wi, zeros(LANES - H_IDX)],
                         axis=1).astype(BF16)
    gains_s = jnp.stack([tile(qn_a), tile(kn_a), tile(qn_b), tile(kn_b)]).astype(F32)
    return wn, wt, gains_t, ws, gains_s, ones_bd


def kernel(x_prompt, x_sample, cache_k_a, cache_v_a, cache_k_b, cache_v_b, cache_kidx_b, state_conv, page_table,
           norm_g, rel_bias_table, w_in_attn, q_norm_a, k_norm_a, q_norm_b, k_norm_b, w_out_attn,
           w_in_conv, conv_w, conv_b, conv_ln_g, conv_ln_b, w_out_conv):
    batch, seq, _ = x_prompt.shape
    n_seq, n_tok, _ = x_sample.shape
    n_pages = page_table.shape[1]
    past = n_pages * PAGE_SIZE
    assert seq % TQ == 0 and past % MOBA_BLOCK == 0 and n_tok <= NEW_PAD and TQ >= MAX_DISTANCE
    n_hist = CONV_WIDTH - 1
    wn, wt, gains_t, ws, gains_s, ones_bd = _prep_attn_weights(
        w_in_attn[0], q_norm_a[0], k_norm_a[0], q_norm_b[0], k_norm_b[0])
    g0 = norm_g[0][None, :]
    g1 = norm_g[1][None, :]
    wo = w_out_attn[0].astype(BF16)
    wc = w_in_conv[0].astype(BF16)
    wo2 = w_out_conv[0].astype(BF16)
    cw = jnp.concatenate([conv_w[0], jnp.zeros((HIST - CONV_WIDTH, D_MODEL), F32)], axis=0)
    conv_rows = (cw, conv_b[0][None, :], conv_ln_g[0][None, :], conv_ln_b[0][None, :])
    c31 = rel_bias_table[N_BUCKETS - 1]
    bias_tiles = _bias_tiles(rel_bias_table)

    xp = x_prompt.reshape(batch * seq, D_MODEL)
    (ga, gb, katm, kbtm, kktm, qat, qbt, qit, wit, vat, vbt, ka32, va32, kb32, vb32, ki32) = _project_prompt(
        xp, g0, wn, wt, gains_t, batch, seq)
    oa = _moba_prompt(c31, qat, katm, vat, bias_tiles, batch, seq)
    ob = _dsa_prompt(c31, qbt, qit, wit, kktm, kbtm, vbt, bias_tiles, batch, seq)
    y_p, st_p = _layer1_prompt(xp, oa, ob, ga, gb, wo, g1, wc, *conv_rows, wo2, batch, seq, tt=TQ)
    heads_p = lambda a: jnp.transpose(a.reshape(batch, N_HEADS, HEAD_DIM, seq), (0, 3, 1, 2))[None]
    ki_p = jnp.transpose(ki32, (0, 2, 1))[None]
    cv_p = st_p[None, :, HIST - n_hist:, :]

    rows = n_tok * N_HEADS
    xs = x_sample.reshape(n_seq * n_tok, D_MODEL)
    (qa_s, ka_s, va_s, qb_s, kb_s, vb_s, qi_s, wi_s, ga_s, gb_s, ka32_s, va32_s, kb32_s, vb32_s, ki32_s) = \
        _project_sample(xs, g0, ws, gains_s, ones_bd)
    head_of_lane = np.arange(D_ATT) // HEAD_DIM
    hmask8 = (np.arange(N_HEADS)[:, None] == head_of_lane[None, :])
    hmask = jnp.asarray(np.tile(hmask8, (n_tok, 1)), F32)
    qbd = lambda q: (q.reshape(n_seq, n_tok, 1, D_ATT) * jnp.asarray(hmask8, BF16)).reshape(n_seq, rows, D_ATT)
    newpad = lambda a: jnp.pad(a.reshape(n_seq, n_tok, D_ATT), ((0, 0), (0, NEW_PAD - n_tok), (0, 0)))
    bd = _bias_by_distance(rel_bias_table, 2 * PAGE_SIZE)
    t_of_row = np.arange(rows) // N_HEADS
    h_of_row = np.arange(rows) % N_HEADS
    c_idx = np.arange(PAGE_SIZE)
    dist_last = PAGE_SIZE + t_of_row[:, None] - c_idx[None, :]
    tn = np.arange(NEW_PAD)
    dist_new = np.maximum(t_of_row[:, None] - tn[None, :], 0)
    valid_new = (tn[None, :] <= t_of_row[:, None]) & (tn[None, :] < n_tok)

    def sample_bias(off):
        hh = h_of_row + off
        blast = bd[dist_last, hh[:, None]]
        bnew = jnp.where(valid_new, bd[dist_new, hh[:, None]], NEG_INF)
        return blast, c31[hh][:, None], bnew

    pages = lambda c: jnp.transpose(c[0], (0, 2, 3, 1)).reshape(c.shape[1], D_ATT, PAGE_SIZE)
    kit = jnp.transpose(cache_kidx_b[0], (0, 2, 1))
    blast_a, c31_a, bnew_a = sample_bias(0)
    oa_s = _sample_moba(page_table, pages(cache_k_a), pages(cache_v_a), qbd(qa_s), blast_a, c31_a,
                        newpad(ka_s), newpad(va_s), bnew_a, hmask, n_tok)
    k_top = min(DSA_TOPK_MAX, (past + n_tok) // 4)
    knew_i = jnp.pad(jnp.transpose(ki32_s.reshape(n_seq, n_tok, D_IDX), (0, 2, 1)),
                     ((0, 0), (0, 0), (0, PAGE_SIZE - n_tok))).astype(BF16)
    pen = _sample_index(page_table, kit, qi_s.reshape(n_seq, rows, D_IDX),
                        wi_s[:, :H_IDX].reshape(n_seq, rows, 1), knew_i, n_tok, k_top)
    blast_b, c31_b, bnew_b = sample_bias(N_HEADS)
    ob_s = _sample_dsa(page_table, pages(cache_k_b), pages(cache_v_b), qbd(qb_s), pen, blast_b, c31_b,
                       newpad(kb_s), newpad(vb_s), bnew_b, hmask, n_tok)
    tmaj = lambda a: jnp.transpose(a.reshape(n_seq, n_tok, -1), (1, 0, 2)).reshape(n_tok * n_seq, -1)
    st_in = jnp.transpose(state_conv[0], (1, 0, 2))
    y_s, ns = _layer1_sample(tmaj(xs), tmaj(oa_s), tmaj(ob_s), tmaj(ga_s), tmaj(gb_s), wo, g1, wc,
                             *conv_rows, wo2, st_in, n_tok, n_seq)
    y_s = jnp.transpose(y_s.reshape(n_tok, n_seq, D_MODEL), (1, 0, 2))
    cv_s = jnp.transpose(ns, (1, 0, 2))[None]
    heads_s = lambda a: a.reshape(1, n_seq, n_tok, N_HEADS, HEAD_DIM)

    return (y_p.reshape(batch, seq, D_MODEL), y_s,
            heads_p(ka32), heads_p(va32), heads_p(kb32), heads_p(vb32), ki_p, cv_p,
            heads_s(ka32_s), heads_s(va32_s), heads_s(kb32_s), heads_s(vb32_s),
            ki32_s.reshape(1, n_seq, n_tok, D_IDX), cv_s)
```

```python
import functools
import math

import numpy as np
import jax
import jax.numpy as jnp
from jax import lax
from jax.experimental import pallas as pl
from jax.experimental.pallas import tpu as pltpu

F32 = jnp.float32
BF16 = jnp.bfloat16
I32 = jnp.int32

D_MODEL = 1024
HEAD_DIM = 64
N_HEADS = 8
D_ATT = N_HEADS * HEAD_DIM
MOBA_BLOCK = 256
MOBA_TOPK = 3
DSA_TOPK_MAX = 256
H_IDX = 8
D_IDX = 64
CONV_WIDTH = 31
N_BUCKETS = 32
MAX_DISTANCE = 128
PAGE_SIZE = 128
RMS_EPS = 1e-6
LN_EPS = 1e-5

LANES = 128
SUBLANES = 8
TQ = MOBA_BLOCK
HIST = 32
NEW_PAD = 16
PAGES_PER_STEP = 16
IDX_PAGES_PER_STEP = 16
CONV_ROWS = 128
FIRST_FAR = 2
INT_MIN = -2 ** 31
NEG_INF = float("-inf")
VMEM_LIMIT = 56 * 1024 * 1024


def _dot(a, b):
    return jnp.dot(a, b, preferred_element_type=F32)


def _dot_nt(a, b):
    return lax.dot_general(a, b, (((1,), (1,)), ((), ())), preferred_element_type=F32)


def _sigmoid(x):
    return 1.0 / (1.0 + jnp.exp(-x))


def _silu(x):
    return x * _sigmoid(x)


def _rms(x, g):
    return x * lax.rsqrt(jnp.mean(x * x, axis=-1, keepdims=True) + RMS_EPS) * g


def _osm_update(scores, pvs, m, l, acc, axis):
    m_new = m
    for s in scores:
        m_new = jnp.maximum(m_new, jnp.max(s, axis=axis, keepdims=True))
    m_safe = jnp.where(m_new == NEG_INF, 0.0, m_new)
    alpha = jnp.exp(m - m_safe)
    l = alpha * l
    acc = alpha * acc
    for s, pv in zip(scores, pvs):
        p = jnp.exp(s - m_safe)
        l = l + jnp.sum(p, axis=axis, keepdims=True)
        acc = acc + pv(p.astype(BF16))
    return m_new, l, acc


def _top_blocks(g, idx, axis):
    picks = []
    for _ in range(MOBA_TOPK):
        mx = jnp.max(g, axis=axis, keepdims=True)
        a = jnp.min(jnp.where(g == mx, idx, LANES * LANES), axis=axis, keepdims=True)
        a = jnp.where(mx == NEG_INF, -1, a)
        picks.append(a)
        g = jnp.where(idx == a, NEG_INF, g)
    return picks


def _f32_key(x):
    b = lax.bitcast_convert_type(x, I32)
    k = b ^ ((b >> 31) & 0x7FFFFFFF)
    return jnp.where(x == 0.0, 0, k)


def _select_threshold(count_fn, k_top, n_keys):
    k_f = float(k_top)
    t = jnp.where(count_fn(lambda kt, pos: kt >= 0) >= k_f, 0, INT_MIN).astype(I32)

    def bit_step(b, t):
        cand = t | jnp.left_shift(jnp.int32(1), 30 - b)
        return jnp.where(count_fn(lambda kt, pos: kt >= cand) >= k_f, cand, t)

    t = lax.fori_loop(0, 31, bit_step, t)
    t = jnp.maximum(t, INT_MIN + 1)
    cnt_ge = count_fn(lambda kt, pos: kt >= t)
    need = k_f - count_fn(lambda kt, pos: kt > t)
    n_bits = max(1, int(math.ceil(math.log2(n_keys))))

    def tie_search():
        def pos_step(b, x):
            cand = x | jnp.left_shift(jnp.int32(1), n_bits - 1 - b)
            below = count_fn(lambda kt, pos: (kt == t) & (pos < cand))
            return jnp.where(below < need, cand, x)
        return lax.fori_loop(0, n_bits, pos_step, jnp.zeros_like(t))

    any_excess = jnp.max(jnp.where(cnt_ge > k_f, 1.0, 0.0)) > 0.0
    x = lax.cond(any_excess, tie_search, lambda: jnp.full_like(t, n_keys))
    return t, x


def _params(n):
    return pltpu.CompilerParams(dimension_semantics=("arbitrary",) * n, vmem_limit_bytes=VMEM_LIMIT)


def _bucket_thresholds():
    exact = N_BUCKETS // 2
    n = np.arange(MAX_DISTANCE + 1)
    nf = np.maximum(n, 1).astype(np.float32)
    large = exact + (np.log(nf / np.float32(exact)) / np.float32(math.log(MAX_DISTANCE / exact))
                     * np.float32(N_BUCKETS - exact)).astype(np.int32)
    bucket = np.where(n < exact, n, np.minimum(large, N_BUCKETS - 1))
    assert bucket[MAX_DISTANCE] == N_BUCKETS - 1 and np.all(np.diff(bucket) >= 0)
    return [int(np.argmax(bucket >= k)) for k in range(N_BUCKETS)]


def _bias_tiles_kernel(tab_ref, o_ref):
    h = pl.program_id(0)
    thr = _bucket_thresholds()
    key = lax.broadcasted_iota(I32, (TQ, TQ), 0)
    qry = lax.broadcasted_iota(I32, (TQ, TQ), 1)
    for d in range(2):
        dist = d * TQ + qry - key
        val = jnp.full((TQ, TQ), tab_ref[0, h], F32)
        for k in range(1, N_BUCKETS):
            val = jnp.where(dist >= thr[k], tab_ref[k, h], val)
        o_ref[d] = val


def _bias_tiles(table):
    n_h = table.shape[1]
    return pl.pallas_call(
        _bias_tiles_kernel,
        grid=(n_h,),
        in_specs=[pl.BlockSpec(memory_space=pltpu.SMEM)],
        out_specs=pl.BlockSpec((None, 2, TQ, TQ), lambda h: (h, 0, 0, 0)),
        out_shape=jax.ShapeDtypeStruct((n_h, 2, TQ, TQ), F32),
        compiler_params=_params(1), name="bias_tiles",
    )(table)


def _proj_prompt_kernel(x_ref, g_ref, wn_ref, wt_ref, gains_ref,
                        ga_ref, gb_ref, katm_ref, kbtm_ref, kktm_ref,
                        qat_ref, qbt_ref, qit_ref, wit_ref, vat_ref, vbt_ref,
                        ka32_ref, va32_ref, kb32_ref, vb32_ref, ki32_ref):
    h = _rms(x_ref[...], g_ref[...]).astype(BF16)
    tm = h.shape[0]
    scale = HEAD_DIM ** -0.5

    def grp_t(c):
        return _dot_nt(wt_ref[c * D_ATT:(c + 1) * D_ATT, :], h)

    def headnorm_t(zt, row):
        z3 = zt.reshape(N_HEADS, HEAD_DIM, tm)
        ssq = jnp.sum(z3 * z3, axis=1, keepdims=True)
        z3 = z3 * lax.rsqrt(ssq * (1.0 / HEAD_DIM) + RMS_EPS)
        return z3.reshape(D_ATT, tm) * gains_ref[row]

    ga_ref[...] = _dot(h, wn_ref[:, :D_ATT])
    gb_ref[...] = _dot(h, wn_ref[:, D_ATT:])
    qat_ref[...] = (headnorm_t(grp_t(0), 0) * scale).astype(BF16)
    kat = headnorm_t(grp_t(1), 1)
    ka32_ref[...] = kat
    katm_ref[...] = kat.T.astype(BF16)
    vat = grp_t(2)
    va32_ref[...] = vat
    vat_ref[...] = vat.astype(BF16)
    qbt_ref[...] = (headnorm_t(grp_t(3), 2) * scale).astype(BF16)
    kbt = headnorm_t(grp_t(4), 3)
    kb32_ref[...] = kbt
    kbtm_ref[...] = kbt.T.astype(BF16)
    vbt = grp_t(5)
    vb32_ref[...] = vbt
    vbt_ref[...] = vbt.astype(BF16)
    qit_ref[...] = grp_t(6).astype(BF16)
    kk = _dot_nt(wt_ref[7 * D_ATT:7 * D_ATT + LANES, :], h)
    ki32_ref[...] = kk[:D_IDX, :]
    kktm_ref[...] = kk.T.astype(BF16)
    wit_ref[...] = _dot_nt(wt_ref[7 * D_ATT + LANES:, :], h)[:H_IDX, :]


def _project_prompt(x, g, wn, wt, gains_t, batch, seq):
    tm = TQ
    n_t = seq // tm
    n = batch * seq
    row = lambda c: pl.BlockSpec((tm, c), lambda b, t: (b * n_t + t, 0))
    full = lambda a: pl.BlockSpec(a.shape, lambda b, t: (0,) * a.ndim)
    tiles = lambda r: pl.BlockSpec((None, None, r, tm), lambda b, t: (b, t, 0, 0))
    cols = lambda r: pl.BlockSpec((None, r, tm), lambda b, t: (b, 0, t))
    sd = jax.ShapeDtypeStruct
    tsd = lambda r, dt: sd((batch, n_t, r, tm), dt)
    out_shape = ([sd((n, D_ATT), F32)] * 2 + [sd((n, D_ATT), BF16)] * 2 + [sd((n, LANES), BF16)]
                 + [tsd(D_ATT, BF16)] * 3 + [tsd(H_IDX, F32)] + [tsd(D_ATT, BF16)] * 2
                 + [sd((batch, D_ATT, seq), F32)] * 4 + [sd((batch, D_IDX, seq), F32)])
    out_specs = ([row(D_ATT)] * 4 + [row(LANES)] + [tiles(D_ATT)] * 3 + [tiles(H_IDX)] + [tiles(D_ATT)] * 2
                 + [cols(D_ATT)] * 4 + [cols(D_IDX)])
    return pl.pallas_call(
        _proj_prompt_kernel,
        grid=(batch, n_t),
        in_specs=[row(D_MODEL), full(g), full(wn), full(wt), full(gains_t)],
        out_specs=out_specs, out_shape=out_shape,
        compiler_params=_params(2), name="proj_prompt",
    )(x, g, wn, wt, gains_t)


def _headnorm_rows(z, ones_ref, gain):
    ssq = _dot((z * z).astype(BF16), ones_ref[...])
    return z * lax.rsqrt(ssq * (1.0 / HEAD_DIM) + RMS_EPS) * gain


def _proj_sample_kernel(x_ref, g_ref, w_ref, gains_ref, ones_ref,
                        qa_ref, ka_ref, va_ref, qb_ref, kb_ref, vb_ref, qi_ref, wi_ref,
                        ga_ref, gb_ref, ka32_ref, va32_ref, kb32_ref, vb32_ref, ki32_ref):
    h = _rms(x_ref[...], g_ref[...]).astype(BF16)
    scale = HEAD_DIM ** -0.5

    def grp(c):
        return _dot(h, w_ref[:, c * D_ATT:(c + 1) * D_ATT])

    qa_ref[...] = (_headnorm_rows(grp(0), ones_ref, gains_ref[0:1, :]) * scale).astype(BF16)
    ka = _headnorm_rows(grp(1), ones_ref, gains_ref[1:2, :])
    ka32_ref[...] = ka
    ka_ref[...] = ka.astype(BF16)
    va = grp(2)
    va32_ref[...] = va
    va_ref[...] = va.astype(BF16)
    ga_ref[...] = grp(3)
    qb_ref[...] = (_headnorm_rows(grp(4), ones_ref, gains_ref[2:3, :]) * scale).astype(BF16)
    kb = _headnorm_rows(grp(5), ones_ref, gains_ref[3:4, :])
    kb32_ref[...] = kb
    kb_ref[...] = kb.astype(BF16)
    vb = grp(6)
    vb32_ref[...] = vb
    vb_ref[...] = vb.astype(BF16)
    gb_ref[...] = grp(7)
    qi_ref[...] = grp(8).astype(BF16)
    tail = _dot(h, w_ref[:, 9 * D_ATT:])
    ki32_ref[...] = tail[:, :D_IDX]
    wi_ref[...] = tail[:, LANES:]


def _project_sample(x, g, w, gains, ones_bd):
    n = x.shape[0]
    full = lambda a: pl.BlockSpec(a.shape, lambda i: (0,) * a.ndim)
    sd = lambda c, dt: jax.ShapeDtypeStruct((n, c), dt)
    out_shape = ([sd(D_ATT, BF16)] * 7 + [sd(LANES, F32)] + [sd(D_ATT, F32)] * 6 + [sd(D_IDX, F32)])
    return pl.pallas_call(
        _proj_sample_kernel,
        grid=(1,),
        in_specs=[full(x), full(g), full(w), full(gains), full(ones_bd)],
        out_specs=[full(s) for s in out_shape], out_shape=out_shape,
        compiler_params=_params(1), name="proj_sample",
    )(x, g, w, gains, ones_bd)


def _head_row_masks():
    r = lax.broadcasted_iota(I32, (LANES, 1), 0)
    return r < HEAD_DIM, r >= HEAD_DIM


def _moba_prompt_kernel(c31_ref, qt_ref, k_ref, vt_ref, bias_ref, o_ref, kmean_ref, *, n_blk):
    hp = pl.program_id(1)
    i = pl.program_id(2)
    n_sub = kmean_ref.shape[0]

    @pl.when(i == 0)
    def _():
        kmean_ref[...] = jnp.zeros_like(kmean_ref)
        for j in range(n_blk):
            kb = k_ref[j * TQ:(j + 1) * TQ, :].astype(F32)
            kmean_ref[j:j + 1, :] = jnp.mean(kb, axis=0, keepdims=True)

    masks = _head_row_masks()
    qt = qt_ref[...]
    key = lax.broadcasted_iota(I32, (TQ, TQ), 0)
    qry = lax.broadcasted_iota(I32, (TQ, TQ), 1)
    blk = lax.broadcasted_iota(I32, (n_sub, TQ), 0)
    kmean = kmean_ref[...].astype(BF16)
    qms = [jnp.where(masks[half], qt, jnp.zeros_like(qt)) for half in range(2)]
    picks = [_top_blocks(jnp.where(blk < i, _dot(kmean, qm), NEG_INF), blk, axis=0) for qm in qms]
    c31 = [c31_ref[hp * 2 + half] for half in range(2)]

    def scores_of(tiles):
        out = []
        for half in range(2):
            row = []
            for j, bias_of, mask_of in tiles:
                off = pl.multiple_of(j * TQ, TQ)
                s = _dot(k_ref[pl.ds(off, TQ), :], qms[half]) + bias_of(half)
                row.append(jnp.where(mask_of(half, j), s, NEG_INF))
            out.append(row)
        return out

    def consume(scores, js, carry):
        out = []
        for half in range(2):
            vrows = slice(half * HEAD_DIM, (half + 1) * HEAD_DIM)
            pvs = [lambda p, j=j, vrows=vrows: _dot(vt_ref[j, vrows, :], p) for j in js]
            out.append(_osm_update(scores[half], pvs, *carry[half], axis=0))
        return tuple(out)

    def update(tiles, carry):
        return consume(scores_of(tiles), [j for j, _, _ in tiles], carry)

    def picked(half, j):
        return (picks[half][0] == j) | (picks[half][1] == j) | (picks[half][2] == j)

    init = (jnp.full((1, TQ), NEG_INF, F32), jnp.zeros((1, TQ), F32), jnp.zeros((HEAD_DIM, TQ), F32))
    far = lambda half: c31[half]
    tiles = [(i, lambda half: bias_ref[half, 0], lambda half, j: key <= qry)]
    for d in range(1, 2 + FIRST_FAR):
        exists = i >= d
        tiles.append((jnp.maximum(i - d, 0), (lambda half: bias_ref[half, 1]) if d == 1 else far,
                      lambda half, j, exists=exists: picked(half, j) & exists))
    carry = update(tiles, (init, init))
    n_far = jnp.maximum(i - 1 - FIRST_FAR, 0)
    done = 0
    for width in (4, 2, 1):
        n_w = (n_far - done) // width
        carry = lax.fori_loop(
            0, n_w,
            lambda t, c, done=done, width=width: update(
                [(done + width * t + w, far, picked) for w in range(width)], c),
            carry)
        done = done + n_w * width
    o_ref[...] = jnp.concatenate([c[2] / c[1] for c in carry], axis=0).T


def _moba_prompt(c31, qt, k, vt, bias, batch, seq):
    n_q = seq // TQ
    n_sub = -(-n_q // SUBLANES) * SUBLANES
    kern = functools.partial(_moba_prompt_kernel, n_blk=n_q)
    return pl.pallas_call(
        kern,
        grid=(batch, N_HEADS // 2, n_q),
        in_specs=[
            pl.BlockSpec(memory_space=pltpu.SMEM),
            pl.BlockSpec((None, None, LANES, TQ), lambda b, hp, i: (b, i, hp, 0)),
            pl.BlockSpec((seq, LANES), lambda b, hp, i: (b, hp)),
            pl.BlockSpec((None, n_q, LANES, TQ), lambda b, hp, i: (b, 0, hp, 0)),
            pl.BlockSpec((2, 2, TQ, TQ), lambda b, hp, i: (hp, 0, 0, 0)),
        ],
        out_specs=pl.BlockSpec((TQ, LANES), lambda b, hp, i: (b * n_q + i, hp)),
        out_shape=jax.ShapeDtypeStruct((batch * seq, D_ATT), F32),
        scratch_shapes=[pltpu.VMEM((n_sub, LANES), F32)],
        compiler_params=_params(3), name="moba_prompt",
    )(c31, qt, k, vt, bias)


def _dsa_prompt_kernel(c31_ref, qt_ref, qit_ref, wit_ref, kk_ref, k_ref, vt_ref, bias_ref, o_ref,
                       keys_ref, m_ref, l_ref, acc_ref, *, seq, k_top):
    i = pl.program_id(1)
    masks = _head_row_masks()
    key = lax.broadcasted_iota(I32, (TQ, TQ), 0)
    qry = lax.broadcasted_iota(I32, (TQ, TQ), 1)
    zero_b = jnp.zeros((LANES, TQ), BF16)

    def head_qt(ref, h):
        pair, half = divmod(h, 2)
        return jnp.where(masks[half], ref[pair * LANES:(pair + 1) * LANES, :], zero_b)

    def index_tile(j):
        kkj = kk_ref[pl.ds(pl.multiple_of(j * TQ, TQ), TQ), :]
        acc = jnp.zeros((TQ, TQ), F32)
        for h in range(H_IDX):
            acc = acc + wit_ref[h:h + 1, :] * jnp.maximum(_dot(kkj, head_qt(qit_ref, h)), 0.0)
        return _f32_key(acc)

    def index_body(j, c):
        keys_ref[j] = index_tile(j)
        return c

    lax.fori_loop(0, i, index_body, 0)
    keys_ref[i] = jnp.where(key <= qry, index_tile(i), INT_MIN)

    def count_fn(pred):
        def body(j, part):
            w = jnp.where(pred(keys_ref[j], key + j * TQ), 1.0, 0.0)
            return part + jnp.sum(w.reshape(TQ // SUBLANES, SUBLANES, TQ), axis=0)
        part = lax.fori_loop(0, i + 1, body, jnp.zeros((SUBLANES, TQ), F32))
        return jnp.sum(part, axis=0, keepdims=True)

    t, x = _select_threshold(count_fn, k_top, seq)

    m_ref[...] = jnp.full(m_ref.shape, NEG_INF, F32)
    l_ref[...] = jnp.zeros(l_ref.shape, F32)
    acc_ref[...] = jnp.zeros(acc_ref.shape, F32)

    def attend(tiles):
        pens = []
        for j, _, exists in tiles:
            kt = keys_ref[j]
            sel = (kt > t) | ((kt == t) & (key + j * TQ <= x))
            pen = jnp.where(sel, 0.0, NEG_INF)
            pens.append(pen if exists is True else jnp.where(exists, pen, NEG_INF))
        for h in range(N_HEADS):
            pair = h // 2
            vrows = slice(h * HEAD_DIM, (h + 1) * HEAD_DIM)
            qm = head_qt(qt_ref, h)
            scores, pvs = [], []
            for (j, bias_of, _), pen in zip(tiles, pens):
                off = pl.multiple_of(j * TQ, TQ)
                scores.append(_dot(k_ref[pl.ds(off, TQ), pair * LANES:(pair + 1) * LANES], qm) + bias_of(h) + pen)
                pvs.append(lambda p, j=j, vrows=vrows: _dot(vt_ref[j, vrows, :], p))
            m, l, acc = _osm_update(scores, pvs, m_ref[h], l_ref[h], acc_ref[h], axis=0)
            m_ref[h] = m
            l_ref[h] = l
            acc_ref[h] = acc

    far = lambda h: c31_ref[N_HEADS + h]

    def attend_loop(lo, n, width):
        def body(u, c):
            attend([(lo + width * u + w, far, True) for w in range(width)])
            return c
        lax.fori_loop(0, n, body, 0)

    attend([(i, lambda h: bias_ref[h, 0], True), (jnp.maximum(i - 1, 0), lambda h: bias_ref[h, 1], i >= 1)])
    n_far = jnp.maximum(i - 1, 0)
    attend_loop(0, n_far // 2, 2)
    attend_loop((n_far // 2) * 2, n_far % 2, 1)
    ot = jnp.concatenate([acc_ref[h] / l_ref[h] for h in range(N_HEADS)], axis=0)
    o_ref[...] = ot.T


def _dsa_prompt(c31, qt, qit, wit, kk, k, vt, bias, batch, seq):
    n_q = seq // TQ
    k_top = min(DSA_TOPK_MAX, seq // 4)
    kern = functools.partial(_dsa_prompt_kernel, seq=seq, k_top=k_top)
    tile = lambda r: pl.BlockSpec((None, None, r, TQ), lambda b, i: (b, i, 0, 0))
    rows = lambda c: pl.BlockSpec((seq, c), lambda b, i: (b, 0))
    return pl.pallas_call(
        kern,
        grid=(batch, n_q),
        in_specs=[
            pl.BlockSpec(memory_space=pltpu.SMEM),
            tile(D_ATT), tile(D_ATT), tile(H_IDX),
            rows(LANES), rows(D_ATT),
            pl.BlockSpec((None, n_q, D_ATT, TQ), lambda b, i: (b, 0, 0, 0)),
            pl.BlockSpec((N_HEADS, 2, TQ, TQ), lambda b, i: (1, 0, 0, 0)),
        ],
        out_specs=pl.BlockSpec((TQ, D_ATT), lambda b, i: (b * n_q + i, 0)),
        out_shape=jax.ShapeDtypeStruct((batch * seq, D_ATT), F32),
        scratch_shapes=[
            pltpu.VMEM((n_q, TQ, TQ), I32),
            pltpu.VMEM((N_HEADS, 1, TQ), F32),
            pltpu.VMEM((N_HEADS, 1, TQ), F32),
            pltpu.VMEM((N_HEADS, HEAD_DIM, TQ), F32),
        ],
        compiler_params=_params(2), name="dsa_prompt",
    )(c31, qt, qit, wit, kk, k, vt, bias)


def _expand_tokens(a, n_tok):
    return jnp.broadcast_to(a[:, None, :], (n_tok, N_HEADS, a.shape[-1])).reshape(n_tok * N_HEADS, a.shape[-1])


def _collapse_heads(o, hmask, n_tok):
    return jnp.sum((o * hmask).reshape(n_tok, N_HEADS, D_ATT), axis=1)


def _sample_index_kernel(pt_ref, *refs, n_pages, n_tok, k_top, n_pp):
    kit_refs = refs[:n_pp]
    qi_ref, wcol_ref, knew_ref, pen_ref, keys_ref = refs[n_pp:]
    g = pl.program_id(1)

    def keys_of(kt):
        x = jnp.maximum(_dot(qi_ref[...], kt), 0.0) * wcol_ref[...]
        return _f32_key(jnp.sum(x.reshape(n_tok, H_IDX, kt.shape[1]), axis=1))

    keys = keys_of(jnp.concatenate([r[...] for r in kit_refs], axis=1).astype(BF16))
    for o in range(n_pp):
        keys_ref[g * n_pp + o] = keys[:, o * LANES:(o + 1) * LANES]

    @pl.when(g == n_pages // n_pp - 1)
    def _():
        lane = lax.broadcasted_iota(I32, (n_tok, LANES), 1)
        tok = lax.broadcasted_iota(I32, (n_tok, LANES), 0)
        keys_ref[n_pages] = jnp.where(lane <= tok, keys_of(knew_ref[...]), INT_MIN)
        shape = (n_pages + 1, n_tok, LANES)
        pos = lax.broadcasted_iota(I32, shape, 0) * LANES + lax.broadcasted_iota(I32, shape, 2)
        all_keys = keys_ref[...]

        def count_fn(pred):
            w = jnp.where(pred(all_keys, pos), 1.0, 0.0)
            terms = [w[j] for j in range(n_pages + 1)]
            while len(terms) > 1:
                terms = [a + b for a, b in zip(terms[::2], terms[1::2])] + terms[len(terms) & ~1:]
            return jnp.sum(terms[0], axis=1, keepdims=True)

        t, x = _select_threshold(count_fn, k_top, (n_pages + 1) * LANES)
        sel = (all_keys > t) | ((all_keys == t) & (pos <= x))
        pen_ref[...] = jnp.where(sel, 0.0, NEG_INF)


def _sample_index(page_table, kit, qi, wcol, knew, n_tok, k_top):
    n_seq, n_pages = page_table.shape
    n_pp = math.gcd(IDX_PAGES_PER_STEP, n_pages)
    kern = functools.partial(_sample_index_kernel, n_pages=n_pages, n_tok=n_tok, k_top=k_top, n_pp=n_pp)
    per_seq = lambda a: pl.BlockSpec((None,) + a.shape[1:], lambda b, g, pt: (b,) + (0,) * (a.ndim - 1))
    page = lambda o: pl.BlockSpec((None, D_IDX, PAGE_SIZE), lambda b, g, pt: (pt[b, g * n_pp + o], 0, 0))
    grid_spec = pltpu.PrefetchScalarGridSpec(
        num_scalar_prefetch=1, grid=(n_seq, n_pages // n_pp),
        in_specs=[page(o) for o in range(n_pp)] + [per_seq(qi), per_seq(wcol), per_seq(knew)],
        out_specs=pl.BlockSpec((None, n_pages + 1, n_tok, LANES), lambda b, g, pt: (b, 0, 0, 0)),
        scratch_shapes=[pltpu.VMEM((n_pages + 1, n_tok, LANES), I32)])
    return pl.pallas_call(
        kern, grid_spec=grid_spec,
        out_shape=jax.ShapeDtypeStruct((n_seq, n_pages + 1, n_tok, LANES), F32),
        compiler_params=_params(2), name="sample_index",
    )(page_table, *([kit] * n_pp), qi, wcol, knew)


def _sample_dsa_kernel(pt_ref, *refs, n_pages, n_tok, n_pp):
    kt_refs, vt_refs = refs[:n_pp], refs[n_pp:2 * n_pp]
    (q_ref, pen_ref, blast_ref, c31_ref, knew_ref, vnew_ref, bnew_ref, hmask_ref,
     o_ref, m_ref, l_ref, acc_ref) = refs[2 * n_pp:]
    g = pl.program_id(1)
    last = g == n_pages // n_pp - 1

    @pl.when(g == 0)
    def _():
        m_ref[...] = jnp.full(m_ref.shape, NEG_INF, F32)
        l_ref[...] = jnp.zeros(l_ref.shape, F32)
        acc_ref[...] = jnp.zeros(acc_ref.shape, F32)

    q = q_ref[...]
    scores, pvs = [], []
    for o in range(n_pp):
        bias = c31_ref[...] if o < n_pp - 1 else jnp.where(last, blast_ref[...], c31_ref[...])
        scores.append(_dot(q, kt_refs[o][...].astype(BF16)) + bias + _expand_tokens(pen_ref[g * n_pp + o], n_tok))
        pvs.append(lambda pp, o=o: _dot_nt(pp, vt_refs[o][...].astype(BF16)))
    m, l, acc = _osm_update(scores, pvs, m_ref[...], l_ref[...], acc_ref[...], axis=1)
    m_ref[...] = m
    l_ref[...] = l
    acc_ref[...] = acc

    @pl.when(last)
    def _():
        pen_new = _expand_tokens(pen_ref[n_pages][:, :NEW_PAD], n_tok)
        s_n = _dot_nt(q, knew_ref[...]) + bnew_ref[...] + pen_new
        _, l2, acc2 = _osm_update([s_n], [lambda pp: _dot(pp, vnew_ref[...])], m, l, acc, axis=1)
        o_ref[...] = _collapse_heads(acc2 / l2, hmask_ref[...], n_tok)


def _sample_dsa(page_table, kt, vt, q, pen, blast, c31col, knew, vnew, bnew, hmask, n_tok):
    n_seq, n_pages = page_table.shape
    n_pp = math.gcd(PAGES_PER_STEP, n_pages)
    rows = n_tok * N_HEADS
    kern = functools.partial(_sample_dsa_kernel, n_pages=n_pages, n_tok=n_tok, n_pp=n_pp)
    per_seq = lambda a: pl.BlockSpec((None,) + a.shape[1:], lambda b, g, pt: (b,) + (0,) * (a.ndim - 1))
    full = lambda a: pl.BlockSpec(a.shape, lambda b, g, pt: (0,) * a.ndim)
    page = lambda o: pl.BlockSpec((None, D_ATT, PAGE_SIZE), lambda b, g, pt: (pt[b, g * n_pp + o], 0, 0))
    pages = [page(o) for o in range(n_pp)]
    grid_spec = pltpu.PrefetchScalarGridSpec(
        num_scalar_prefetch=1, grid=(n_seq, n_pages // n_pp),
        in_specs=pages + pages + [per_seq(q), per_seq(pen), full(blast), full(c31col),
                                  per_seq(knew), per_seq(vnew), full(bnew), full(hmask)],
        out_specs=pl.BlockSpec((None, n_tok, D_ATT), lambda b, g, pt: (b, 0, 0)),
        scratch_shapes=[pltpu.VMEM((rows, 1), F32), pltpu.VMEM((rows, 1), F32), pltpu.VMEM((rows, D_ATT), F32)])
    return pl.pallas_call(
        kern, grid_spec=grid_spec,
        out_shape=jax.ShapeDtypeStruct((n_seq, n_tok, D_ATT), F32),
        compiler_params=_params(2), name="sample_dsa",
    )(page_table, *([kt] * n_pp), *([vt] * n_pp), q, pen, blast, c31col, knew, vnew, bnew, hmask)


def _sample_moba_kernel(pt_ref, *refs, n_blk, n_tok, n_pp):
    kt_refs, vt_refs = refs[:n_pp], refs[n_pp:2 * n_pp]
    (q_ref, blast_ref, c31_ref, knew_ref, vnew_ref, bnew_ref, hmask_ref,
     o_ref, gate_ref, ms_ref, ls_ref, accs_ref) = refs[2 * n_pp:]
    g = pl.program_id(1)
    n_bb = n_pp // 2
    last = g == n_blk // n_bb - 1
    rows = n_tok * N_HEADS
    lane = lax.broadcasted_iota(I32, (rows, LANES), 1)

    @pl.when(g == 0)
    def _():
        gate_ref[...] = jnp.full(gate_ref.shape, NEG_INF, F32)
        ms_ref[...] = jnp.full(ms_ref.shape, NEG_INF, F32)
        ls_ref[...] = jnp.zeros(ls_ref.shape, F32)

    q = q_ref[...]
    gates, ms, ls = gate_ref[...], ms_ref[...], ls_ref[...]
    for bb in range(n_bb):
        j = g * n_bb + bb
        r0 = _dot(q, kt_refs[2 * bb][...].astype(BF16))
        r1 = _dot(q, kt_refs[2 * bb + 1][...].astype(BF16))
        gate = jnp.sum(r0 + r1, axis=1, keepdims=True)
        s0 = r0 + c31_ref[...]
        s1 = r1 + (c31_ref[...] if bb < n_bb - 1 else jnp.where(last, blast_ref[...], c31_ref[...]))
        m = jnp.maximum(jnp.max(s0, axis=1, keepdims=True), jnp.max(s1, axis=1, keepdims=True))
        p0 = jnp.exp(s0 - m)
        p1 = jnp.exp(s1 - m)
        l = jnp.sum(p0, axis=1, keepdims=True) + jnp.sum(p1, axis=1, keepdims=True)
        accs_ref[j] = (_dot_nt(p0.astype(BF16), vt_refs[2 * bb][...].astype(BF16))
                       + _dot_nt(p1.astype(BF16), vt_refs[2 * bb + 1][...].astype(BF16)))
        gates = jnp.where(lane == j, gate, gates)
        ms = jnp.where(lane == j, m, ms)
        ls = jnp.where(lane == j, l, ls)
    gate_ref[...] = gates
    ms_ref[...] = ms
    ls_ref[...] = ls

    @pl.when(last)
    def _():
        picks = _top_blocks(jnp.where(lane < n_blk, gates, NEG_INF), lane, axis=1)
        sel = (lane == picks[0]) | (lane == picks[1]) | (lane == picks[2])
        s_n = _dot_nt(q, knew_ref[...]) + bnew_ref[...]
        m_o = jnp.max(s_n, axis=1, keepdims=True)
        p_o = jnp.exp(s_n - m_o)
        l_o = jnp.sum(p_o, axis=1, keepdims=True)
        m_all = jnp.maximum(m_o, jnp.max(jnp.where(sel, ms, NEG_INF), axis=1, keepdims=True))
        w = jnp.where(sel, jnp.exp(ms - m_all), 0.0)
        w_o = jnp.exp(m_o - m_all)
        l_all = jnp.sum(w * ls, axis=1, keepdims=True) + w_o * l_o
        out = w_o * _dot(p_o.astype(BF16), vnew_ref[...])
        for jj in range(n_blk):
            out = out + w[:, jj:jj + 1] * accs_ref[jj]
        o_ref[...] = _collapse_heads(out / l_all, hmask_ref[...], n_tok)


def _sample_moba(page_table, kt, vt, q, blast, c31col, knew, vnew, bnew, hmask, n_tok):
    n_seq, n_pages = page_table.shape
    pages_per_blk = MOBA_BLOCK // PAGE_SIZE
    n_blk = n_pages // pages_per_blk
    n_pp = pages_per_blk * math.gcd(PAGES_PER_STEP // pages_per_blk, n_blk)
    rows = n_tok * N_HEADS
    kern = functools.partial(_sample_moba_kernel, n_blk=n_blk, n_tok=n_tok, n_pp=n_pp)
    per_seq = lambda a: pl.BlockSpec((None,) + a.shape[1:], lambda b, g, pt: (b,) + (0,) * (a.ndim - 1))
    full = lambda a: pl.BlockSpec(a.shape, lambda b, g, pt: (0,) * a.ndim)
    page = lambda o: pl.BlockSpec((None, D_ATT, PAGE_SIZE), lambda b, g, pt: (pt[b, g * n_pp + o], 0, 0))
    pages = [page(o) for o in range(n_pp)]
    grid_spec = pltpu.PrefetchScalarGridSpec(
        num_scalar_prefetch=1, grid=(n_seq, n_pages // n_pp),
        in_specs=pages + pages + [per_seq(q), full(blast), full(c31col),
                                  per_seq(knew), per_seq(vnew), full(bnew), full(hmask)],
        out_specs=pl.BlockSpec((None, n_tok, D_ATT), lambda b, g, pt: (b, 0, 0)),
        scratch_shapes=[pltpu.VMEM((rows, LANES), F32)] * 3 + [pltpu.VMEM((n_blk, rows, D_ATT), F32)])
    return pl.pallas_call(
        kern, grid_spec=grid_spec,
        out_shape=jax.ShapeDtypeStruct((n_seq, n_tok, D_ATT), F32),
        compiler_params=_params(2), name="sample_moba",
    )(page_table, *([kt] * n_pp), *([vt] * n_pp), q, blast, c31col, knew, vnew, bnew, hmask)


def _merge_and_conv_in(x_ref, oa_ref, ob_ref, ga_ref, gb_ref, wo_ref, g1_ref, wc_ref):
    o = jnp.concatenate([oa_ref[...] * _silu(ga_ref[...]), ob_ref[...] * _silu(gb_ref[...])], axis=1)
    h1 = x_ref[...] + _dot(o.astype(BF16), wo_ref[...])
    hn = _rms(h1, g1_ref[...]).astype(BF16)
    a = _dot(hn, wc_ref[:, 0:D_MODEL])
    b = _dot(hn, wc_ref[:, D_MODEL:2 * D_MODEL])
    g = _dot(hn, wc_ref[:, 2 * D_MODEL:3 * D_MODEL])
    return h1, a * _sigmoid(b), g


def _ln_gate_out(c, g, h1, lg_ref, lb_ref, wo2_ref):
    mu = jnp.mean(c, axis=-1, keepdims=True)
    d = c - mu
    var = jnp.mean(d * d, axis=-1, keepdims=True)
    cn = d * lax.rsqrt(var + LN_EPS) * lg_ref[...] + lb_ref[...]
    z = (_silu(cn) * _silu(g)).astype(BF16)
    return h1 + _dot(z, wo2_ref[...])


def _layer1_prompt_kernel(x_ref, oa_ref, ob_ref, ga_ref, gb_ref, wo_ref, g1_ref, wc_ref,
                          cw_ref, cb_ref, lg_ref, lb_ref, wo2_ref, y_ref, st_ref, up_ref, c_ref, *, tt):
    t = pl.program_id(1)

    @pl.when(t == 0)
    def _():
        up_ref[0:HIST, :] = jnp.zeros((HIST, D_MODEL), F32)

    h1, u, g = _merge_and_conv_in(x_ref, oa_ref, ob_ref, ga_ref, gb_ref, wo_ref, g1_ref, wc_ref)
    up_ref[HIST:HIST + tt, :] = u
    base = HIST - (CONV_WIDTH - 1)
    rows = min(tt, CONV_ROWS)
    for cc in range(0, D_MODEL, LANES):
        cs = slice(cc, cc + LANES)
        for r0 in range(0, tt, rows):
            acc = jnp.zeros((rows, LANES), F32) + cb_ref[:, cs]
            for phase in range(SUBLANES):
                taps = [w for w in range(CONV_WIDTH) if (base + w) % SUBLANES == phase]
                n_win = rows + (SUBLANES if phase else 0)
                group = None
                for w in taps:
                    start = r0 + base + w - phase
                    term = cw_ref[w:w + 1, cs] * up_ref[start:start + n_win, cs]
                    group = term if group is None else group + term
                if group is not None:
                    acc = acc + group[phase:phase + rows, :]
            c_ref[r0:r0 + rows, cs] = acc
    y_ref[...] = _ln_gate_out(c_ref[...], g, h1, lg_ref, lb_ref, wo2_ref)
    tail = up_ref[tt:tt + HIST, :]
    st_ref[...] = tail
    up_ref[0:HIST, :] = tail


def _layer1_prompt(x, oa, ob, ga, gb, wo, g1, wc, cw, cb, lg, lb, wo2, batch, seq, tt):
    n_t = seq // tt
    kern = functools.partial(_layer1_prompt_kernel, tt=tt)
    row = lambda c: pl.BlockSpec((tt, c), lambda b, t: (b * n_t + t, 0))
    full = lambda a: pl.BlockSpec(a.shape, lambda b, t: (0,) * a.ndim)
    return pl.pallas_call(
        kern,
        grid=(batch, n_t),
        in_specs=[row(D_MODEL), row(D_ATT), row(D_ATT), row(D_ATT), row(D_ATT),
                  full(wo), full(g1), full(wc), full(cw), full(cb), full(lg), full(lb), full(wo2)],
        out_specs=[row(D_MODEL), pl.BlockSpec((None, HIST, D_MODEL), lambda b, t: (b, 0, 0))],
        out_shape=[jax.ShapeDtypeStruct((batch * seq, D_MODEL), F32),
                   jax.ShapeDtypeStruct((batch, HIST, D_MODEL), F32)],
        scratch_shapes=[pltpu.VMEM((HIST + tt, D_MODEL), F32), pltpu.VMEM((tt, D_MODEL), F32)],
        compiler_params=_params(2), name="layer1_prompt",
    )(x, oa, ob, ga, gb, wo, g1, wc, cw, cb, lg, lb, wo2)


def _layer1_sample_kernel(x_ref, oa_ref, ob_ref, ga_ref, gb_ref, wo_ref, g1_ref, wc_ref,
                          cw_ref, cb_ref, lg_ref, lb_ref, wo2_ref, st_ref, y_ref, ns_ref, c_ref, *, n_tok, n_seq):
    n_hist = CONV_WIDTH - 1
    h1, u, g = _merge_and_conv_in(x_ref, oa_ref, ob_ref, ga_ref, gb_ref, wo_ref, g1_ref, wc_ref)
    u_t = [u[t * n_seq:(t + 1) * n_seq, :] for t in range(n_tok)]
    for t in range(n_tok):
        acc = jnp.zeros((n_seq, D_MODEL), F32) + cb_ref[...]
        for r in range(t, n_hist):
            acc = acc + cw_ref[r - t:r - t + 1, :] * st_ref[r]
        for r in range(t + 1):
            acc = acc + cw_ref[n_hist - t + r:n_hist - t + r + 1, :] * u_t[r]
        c_ref[t * n_seq:(t + 1) * n_seq, :] = acc
    y_ref[...] = _ln_gate_out(c_ref[...], g, h1, lg_ref, lb_ref, wo2_ref)
    for r in range(n_hist - n_tok):
        ns_ref[r] = st_ref[r + n_tok]
    for t in range(n_tok):
        ns_ref[n_hist - n_tok + t] = u_t[t]


def _layer1_sample(x, oa, ob, ga, gb, wo, g1, wc, cw, cb, lg, lb, wo2, st, n_tok, n_seq):
    kern = functools.partial(_layer1_sample_kernel, n_tok=n_tok, n_seq=n_seq)
    full = lambda a: pl.BlockSpec(a.shape, lambda i: (0,) * a.ndim)
    args = (x, oa, ob, ga, gb, wo, g1, wc, cw, cb, lg, lb, wo2, st)
    out_shape = [jax.ShapeDtypeStruct(x.shape, F32), jax.ShapeDtypeStruct(st.shape, F32)]
    return pl.pallas_call(
        kern, grid=(1,),
        in_specs=[full(a) for a in args],
        out_specs=[full(s) for s in out_shape], out_shape=out_shape,
        scratch_shapes=[pltpu.VMEM(x.shape, F32)],
        compiler_params=_params(1), name="layer1_sample",
    )(*args)


def _t5_bucket(n):
    exact = N_BUCKETS // 2
    nf = jnp.maximum(n, 1).astype(F32)
    large = exact + (jnp.log(nf / exact) / math.log(MAX_DISTANCE / exact) * (N_BUCKETS - exact)).astype(I32)
    large = jnp.minimum(large, N_BUCKETS - 1)
    return jnp.where(n < exact, n, large)


def _bias_by_distance(table, n):
    return table[_t5_bucket(jnp.arange(n, dtype=I32))]


def _prep_attn_weights(w_in, qn_a, kn_a, qn_b, kn_b):
    grp = lambda c: w_in[:, c * D_ATT:(c + 1) * D_ATT]
    ki = w_in[:, 9 * D_ATT:9 * D_ATT + D_IDX]
    wi = w_in[:, 9 * D_ATT + D_IDX:]
    zeros = lambda c: jnp.zeros((D_MODEL, c), w_in.dtype)
    tile = lambda g: jnp.tile(g, N_HEADS)
    head = np.arange(D_ATT) // HEAD_DIM
    ones_bd = jnp.asarray(head[:, None] == head[None, :], dtype=BF16)
    wn = jnp.concatenate([grp(3), grp(7)], axis=1).astype(BF16)
    wt = jnp.concatenate([grp(0), grp(1), grp(2), grp(4), grp(5), grp(6), grp(8), ki, ki, wi,
                          zeros(NEW_PAD - H_IDX)], axis=1).T.astype(BF16)
    gains_t = jnp.stack([tile(qn_a), tile(kn_a), tile(qn_b), tile(kn_b)]).astype(F32)[:, :, None]
    ws = jnp.concatenate([w_in[:, :9 * D_ATT + D_IDX], zeros(LANES - D_IDX), wi, zeros(LANES - H_IDX)],
                         axis=1).astype(BF16)
    gains_s = jnp.stack([tile(qn_a), tile(kn_a), tile(qn_b), tile(kn_b)]).astype(F32)
    return wn, wt, gains_t, ws, gains_s, ones_bd


def kernel(x_prompt, x_sample, cache_k_a, cache_v_a, cache_k_b, cache_v_b, cache_kidx_b, state_conv, page_table,
           norm_g, rel_bias_table, w_in_attn, q_norm_a, k_norm_a, q_norm_b, k_norm_b, w_out_attn,
           w_in_conv, conv_w, conv_b, conv_ln_g, conv_ln_b, w_out_conv):
    batch, seq, _ = x_prompt.shape
    n_seq, n_tok, _ = x_sample.shape
    n_pages = page_table.shape[1]
    past = n_pages * PAGE_SIZE
    assert seq % TQ == 0 and past % MOBA_BLOCK == 0 and n_tok <= NEW_PAD and TQ >= MAX_DISTANCE
    n_hist = CONV_WIDTH - 1
    wn, wt, gains_t, ws, gains_s, ones_bd = _prep_attn_weights(
        w_in_attn[0], q_norm_a[0], k_norm_a[0], q_norm_b[0], k_norm_b[0])
    g0 = norm_g[0][None, :]
    g1 = norm_g[1][None, :]
    wo = w_out_attn[0].astype(BF16)
    wc = w_in_conv[0].astype(BF16)
    wo2 = w_out_conv[0].astype(BF16)
    cw = jnp.concatenate([conv_w[0], jnp.zeros((HIST - CONV_WIDTH, D_MODEL), F32)], axis=0)
    conv_rows = (cw, conv_b[0][None, :], conv_ln_g[0][None, :], conv_ln_b[0][None, :])
    c31 = rel_bias_table[N_BUCKETS - 1]
    bias_tiles = _bias_tiles(rel_bias_table)

    xp = x_prompt.reshape(batch * seq, D_MODEL)
    (ga, gb, katm, kbtm, kktm, qat, qbt, qit, wit, vat, vbt, ka32, va32, kb32, vb32, ki32) = _project_prompt(
        xp, g0, wn, wt, gains_t, batch, seq)
    oa = _moba_prompt(c31, qat, katm, vat, bias_tiles, batch, seq)
    ob = _dsa_prompt(c31, qbt, qit, wit, kktm, kbtm, vbt, bias_tiles, batch, seq)
    y_p, st_p = _layer1_prompt(xp, oa, ob, ga, gb, wo, g1, wc, *conv_rows, wo2, batch, seq, tt=TQ)
    heads_p = lambda a: jnp.transpose(a.reshape(batch, N_HEADS, HEAD_DIM, seq), (0, 3, 1, 2))[None]
    ki_p = jnp.transpose(ki32, (0, 2, 1))[None]
    cv_p = st_p[None, :, HIST - n_hist:, :]

    rows = n_tok * N_HEADS
    xs = x_sample.reshape(n_seq * n_tok, D_MODEL)
    (qa_s, ka_s, va_s, qb_s, kb_s, vb_s, qi_s, wi_s, ga_s, gb_s, ka32_s, va32_s, kb32_s, vb32_s, ki32_s) = \
        _project_sample(xs, g0, ws, gains_s, ones_bd)
    head_of_lane = np.arange(D_ATT) // HEAD_DIM
    hmask8 = (np.arange(N_HEADS)[:, None] == head_of_lane[None, :])
    hmask = jnp.asarray(np.tile(hmask8, (n_tok, 1)), F32)
    qbd = lambda q: (q.reshape(n_seq, n_tok, 1, D_ATT) * jnp.asarray(hmask8, BF16)).reshape(n_seq, rows, D_ATT)
    newpad = lambda a: jnp.pad(a.reshape(n_seq, n_tok, D_ATT), ((0, 0), (0, NEW_PAD - n_tok), (0, 0)))
    bd = _bias_by_distance(rel_bias_table, 2 * PAGE_SIZE)
    t_of_row = np.arange(rows) // N_HEADS
    h_of_row = np.arange(rows) % N_HEADS
    c_idx = np.arange(PAGE_SIZE)
    dist_last = PAGE_SIZE + t_of_row[:, None] - c_idx[None, :]
    tn = np.arange(NEW_PAD)
    dist_new = np.maximum(t_of_row[:, None] - tn[None, :], 0)
    valid_new = (tn[None, :] <= t_of_row[:, None]) & (tn[None, :] < n_tok)

    def sample_bias(off):
        hh = h_of_row + off
        blast = bd[dist_last, hh[:, None]]
        bnew = jnp.where(valid_new, bd[dist_new, hh[:, None]], NEG_INF)
        return blast, c31[hh][:, None], bnew

    pages = lambda c: jnp.transpose(c[0], (0, 2, 3, 1)).reshape(c.shape[1], D_ATT, PAGE_SIZE)
    kit = jnp.transpose(cache_kidx_b[0], (0, 2, 1))
    blast_a, c31_a, bnew_a = sample_bias(0)
    oa_s = _sample_moba(page_table, pages(cache_k_a), pages(cache_v_a), qbd(qa_s), blast_a, c31_a,
                        newpad(ka_s), newpad(va_s), bnew_a, hmask, n_tok)
    k_top = min(DSA_TOPK_MAX, (past + n_tok) // 4)
    knew_i = jnp.pad(jnp.transpose(ki32_s.reshape(n_seq, n_tok, D_IDX), (0, 2, 1)),
                     ((0, 0), (0, 0), (0, PAGE_SIZE - n_tok))).astype(BF16)
    pen = _sample_index(page_table, kit, qi_s.reshape(n_seq, rows, D_IDX),
                        wi_s[:, :H_IDX].reshape(n_seq, rows, 1), knew_i, n_tok, k_top)
    blast_b, c31_b, bnew_b = sample_bias(N_HEADS)
    ob_s = _sample_dsa(page_table, pages(cache_k_b), pages(cache_v_b), qbd(qb_s), pen, blast_b, c31_b,
                       newpad(kb_s), newpad(vb_s), bnew_b, hmask, n_tok)
    tmaj = lambda a: jnp.transpose(a.reshape(n_seq, n_tok, -1), (1, 0, 2)).reshape(n_tok * n_seq, -1)
    st_in = jnp.transpose(state_conv[0], (1, 0, 2))
    y_s, ns = _layer1_sample(tmaj(xs), tmaj(oa_s), tmaj(ob_s), tmaj(ga_s), tmaj(gb_s), wo, g1, wc,
                             *conv_rows, wo2, st_in, n_tok, n_seq)
    y_s = jnp.transpose(y_s.reshape(n_tok, n_seq, D_MODEL), (1, 0, 2))
    cv_s = jnp.transpose(ns, (1, 0, 2))[None]
    heads_s = lambda a: a.reshape(1, n_seq, n_tok, N_HEADS, HEAD_DIM)

    return (y_p.reshape(batch, seq, D_MODEL), y_s,
            heads_p(ka32), heads_p(va32), heads_p(kb32), heads_p(vb32), ki_p, cv_p,
            heads_s(ka32_s), heads_s(va32_s), heads_s(kb32_s), heads_s(vb32_s),
            ki32_s.reshape(1, n_seq, n_tok, D_IDX), cv_s)
```

```python
import functools
import math

import numpy as np
import jax
import jax.numpy as jnp
from jax import lax
from jax.experimental import pallas as pl
from jax.experimental.pallas import tpu as pltpu

F32 = jnp.float32
BF16 = jnp.bfloat16
I32 = jnp.int32

D_MODEL = 1024
HEAD_DIM = 64
N_HEADS = 8
D_ATT = N_HEADS * HEAD_DIM
MOBA_BLOCK = 256
MOBA_TOPK = 3
DSA_TOPK_MAX = 256
H_IDX = 8
D_IDX = 64
CONV_WIDTH = 31
N_BUCKETS = 32
MAX_DISTANCE = 128
PAGE_SIZE = 128
RMS_EPS = 1e-6
LN_EPS = 1e-5

LANES = 128
SUBLANES = 8
TQ = MOBA_BLOCK
HIST = 32
NEW_PAD = 16
PAGES_PER_STEP = 16
IDX_PAGES_PER_STEP = 16
CONV_ROWS = 128
FIRST_FAR = 2
INT_MIN = -2 ** 31
NEG_INF = float("-inf")
VMEM_LIMIT = 56 * 1024 * 1024


def _dot(a, b):
    return jnp.dot(a, b, preferred_element_type=F32)


def _dot_nt(a, b):
    return lax.dot_general(a, b, (((1,), (1,)), ((), ())), preferred_element_type=F32)


def _sigmoid(x):
    return 1.0 / (1.0 + jnp.exp(-x))


def _silu(x):
    return x * _sigmoid(x)


def _rms(x, g):
    return x * lax.rsqrt(jnp.mean(x * x, axis=-1, keepdims=True) + RMS_EPS) * g


def _osm_update(scores, pvs, m, l, acc, axis):
    m_new = m
    for s in scores:
        m_new = jnp.maximum(m_new, jnp.max(s, axis=axis, keepdims=True))
    m_safe = jnp.where(m_new == NEG_INF, 0.0, m_new)
    alpha = jnp.exp(m - m_safe)
    l = alpha * l
    acc = alpha * acc
    for s, pv in zip(scores, pvs):
        p = jnp.exp(s - m_safe)
        l = l + jnp.sum(p, axis=axis, keepdims=True)
        acc = acc + pv(p.astype(BF16))
    return m_new, l, acc


def _top_blocks(g, idx, axis):
    picks = []
    for _ in range(MOBA_TOPK):
        mx = jnp.max(g, axis=axis, keepdims=True)
        a = jnp.min(jnp.where(g == mx, idx, LANES * LANES), axis=axis, keepdims=True)
        a = jnp.where(mx == NEG_INF, -1, a)
        picks.append(a)
        g = jnp.where(idx == a, NEG_INF, g)
    return picks


def _f32_key(x):
    b = lax.bitcast_convert_type(x, I32)
    k = b ^ ((b >> 31) & 0x7FFFFFFF)
    return jnp.where(x == 0.0, 0, k)


def _select_threshold(count_fn, k_top, n_keys):
    k_f = float(k_top)
    t = jnp.where(count_fn(lambda kt, pos: kt >= 0) >= k_f, 0, INT_MIN).astype(I32)

    def bit_step(b, t):
        cand = t | jnp.left_shift(jnp.int32(1), 30 - b)
        return jnp.where(count_fn(lambda kt, pos: kt >= cand) >= k_f, cand, t)

    t = lax.fori_loop(0, 31, bit_step, t)
    t = jnp.maximum(t, INT_MIN + 1)
    cnt_ge = count_fn(lambda kt, pos: kt >= t)
    need = k_f - count_fn(lambda kt, pos: kt > t)
    n_bits = max(1, int(math.ceil(math.log2(n_keys))))

    def tie_search():
        def pos_step(b, x):
            cand = x | jnp.left_shift(jnp.int32(1), n_bits - 1 - b)
            below = count_fn(lambda kt, pos: (kt == t) & (pos < cand))
            return jnp.where(below < need, cand, x)
        return lax.fori_loop(0, n_bits, pos_step, jnp.zeros_like(t))

    any_excess = jnp.max(jnp.where(cnt_ge > k_f, 1.0, 0.0)) > 0.0
    x = lax.cond(any_excess, tie_search, lambda: jnp.full_like(t, n_keys))
    return t, x


def _params(n):
    return pltpu.CompilerParams(dimension_semantics=("arbitrary",) * n, vmem_limit_bytes=VMEM_LIMIT)


def _bucket_thresholds():
    exact = N_BUCKETS // 2
    n = np.arange(MAX_DISTANCE + 1)
    nf = np.maximum(n, 1).astype(np.float32)
    large = exact + (np.log(nf / np.float32(exact)) / np.float32(math.log(MAX_DISTANCE / exact))
                     * np.float32(N_BUCKETS - exact)).astype(np.int32)
    bucket = np.where(n < exact, n, np.minimum(large, N_BUCKETS - 1))
    assert bucket[MAX_DISTANCE] == N_BUCKETS - 1 and np.all(np.diff(bucket) >= 0)
    return [int(np.argmax(bucket >= k)) for k in range(N_BUCKETS)]


def _bias_tiles_kernel(tab_ref, o_ref):
    h = pl.program_id(0)
    thr = _bucket_thresholds()
    key = lax.broadcasted_iota(I32, (TQ, TQ), 0)
    qry = lax.broadcasted_iota(I32, (TQ, TQ), 1)
    for d in range(2):
        dist = d * TQ + qry - key
        val = jnp.full((TQ, TQ), tab_ref[0, h], F32)
        for k in range(1, N_BUCKETS):
            val = jnp.where(dist >= thr[k], tab_ref[k, h], val)
        o_ref[d] = val


def _bias_tiles(table):
    n_h = table.shape[1]
    return pl.pallas_call(
        _bias_tiles_kernel,
        grid=(n_h,),
        in_specs=[pl.BlockSpec(memory_space=pltpu.SMEM)],
        out_specs=pl.BlockSpec((None, 2, TQ, TQ), lambda h: (h, 0, 0, 0)),
        out_shape=jax.ShapeDtypeStruct((n_h, 2, TQ, TQ), F32),
        compiler_params=_params(1), name="bias_tiles",
    )(table)


def _proj_prompt_kernel(x_ref, g_ref, wn_ref, wt_ref, gains_ref,
                        ga_ref, gb_ref, katm_ref, kbtm_ref, kktm_ref,
                        qat_ref, qbt_ref, qit_ref, wit_ref, vat_ref, vbt_ref,
                        ka32_ref, va32_ref, kb32_ref, vb32_ref, ki32_ref):
    h = _rms(x_ref[...], g_ref[...]).astype(BF16)
    tm = h.shape[0]
    scale = HEAD_DIM ** -0.5

    def grp_t(c):
        return _dot_nt(wt_ref[c * D_ATT:(c + 1) * D_ATT, :], h)

    def headnorm_t(zt, row):
        z3 = zt.reshape(N_HEADS, HEAD_DIM, tm)
        ssq = jnp.sum(z3 * z3, axis=1, keepdims=True)
        z3 = z3 * lax.rsqrt(ssq * (1.0 / HEAD_DIM) + RMS_EPS)
        return z3.reshape(D_ATT, tm) * gains_ref[row]

    ga_ref[...] = _dot(h, wn_ref[:, :D_ATT])
    gb_ref[...] = _dot(h, wn_ref[:, D_ATT:])
    qat_ref[...] = (headnorm_t(grp_t(0), 0) * scale).astype(BF16)
    kat = headnorm_t(grp_t(1), 1)
    ka32_ref[...] = kat
    katm_ref[...] = kat.T.astype(BF16)
    vat = grp_t(2)
    va32_ref[...] = vat
    vat_ref[...] = vat.astype(BF16)
    qbt_ref[...] = (headnorm_t(grp_t(3), 2) * scale).astype(BF16)
    kbt = headnorm_t(grp_t(4), 3)
    kb32_ref[...] = kbt
    kbtm_ref[...] = kbt.T.astype(BF16)
    vbt = grp_t(5)
    vb32_ref[...] = vbt
    vbt_ref[...] = vbt.astype(BF16)
    qit_ref[...] = grp_t(6).astype(BF16)
    kk = _dot_nt(wt_ref[7 * D_ATT:7 * D_ATT + LANES, :], h)
    ki32_ref[...] = kk[:D_IDX, :]
    kktm_ref[...] = kk.T.astype(BF16)
    wit_ref[...] = _dot_nt(wt_ref[7 * D_ATT + LANES:, :], h)[:H_IDX, :]


def _project_prompt(x, g, wn, wt, gains_t, batch, seq):
    tm = TQ
    n_t = seq // tm
    n = batch * seq
    row = lambda c: pl.BlockSpec((tm, c), lambda b, t: (b * n_t + t, 0))
    full = lambda a: pl.BlockSpec(a.shape, lambda b, t: (0,) * a.ndim)
    tiles = lambda r: pl.BlockSpec((None, None, r, tm), lambda b, t: (b, t, 0, 0))
    cols = lambda r: pl.BlockSpec((None, r, tm), lambda b, t: (b, 0, t))
    sd = jax.ShapeDtypeStruct
    tsd = lambda r, dt: sd((batch, n_t, r, tm), dt)
    out_shape = ([sd((n, D_ATT), F32)] * 2 + [sd((n, D_ATT), BF16)] * 2 + [sd((n, LANES), BF16)]
                 + [tsd(D_ATT, BF16)] * 3 + [tsd(H_IDX, F32)] + [tsd(D_ATT, BF16)] * 2
                 + [sd((batch, D_ATT, seq), F32)] * 4 + [sd((batch, D_IDX, seq), F32)])
    out_specs = ([row(D_ATT)] * 4 + [row(LANES)] + [tiles(D_ATT)] * 3 + [tiles(H_IDX)] + [tiles(D_ATT)] * 2
                 + [cols(D_ATT)] * 4 + [cols(D_IDX)])
    return pl.pallas_call(
        _proj_prompt_kernel,
        grid=(batch, n_t),
        in_specs=[row(D_MODEL), full(g), full(wn), full(wt), full(gains_t)],
        out_specs=out_specs, out_shape=out_shape,
        compiler_params=_params(2), name="proj_prompt",
    )(x, g, wn, wt, gains_t)


def _headnorm_rows(z, ones_ref, gain):
    ssq = _dot((z * z).astype(BF16), ones_ref[...])
    return z * lax.rsqrt(ssq * (1.0 / HEAD_DIM) + RMS_EPS) * gain


def _proj_sample_kernel(x_ref, g_ref, w_ref, gains_ref, ones_ref,
                        qa_ref, ka_ref, va_ref, qb_ref, kb_ref, vb_ref, qi_ref, wi_ref,
                        ga_ref, gb_ref, ka32_ref, va32_ref, kb32_ref, vb32_ref, ki32_ref):
    h = _rms(x_ref[...], g_ref[...]).astype(BF16)
    scale = HEAD_DIM ** -0.5

    def grp(c):
        return _dot(h, w_ref[:, c * D_ATT:(c + 1) * D_ATT])

    qa_ref[...] = (_headnorm_rows(grp(0), ones_ref, gains_ref[0:1, :]) * scale).astype(BF16)
    ka = _headnorm_rows(grp(1), ones_ref, gains_ref[1:2, :])
    ka32_ref[...] = ka
    ka_ref[...] = ka.astype(BF16)
    va = grp(2)
    va32_ref[...] = va
    va_ref[...] = va.astype(BF16)
    ga_ref[...] = grp(3)
    qb_ref[...] = (_headnorm_rows(grp(4), ones_ref, gains_ref[2:3, :]) * scale).astype(BF16)
    kb = _headnorm_rows(grp(5), ones_ref, gains_ref[3:4, :])
    kb32_ref[...] = kb
    kb_ref[...] = kb.astype(BF16)
    vb = grp(6)
    vb32_ref[...] = vb
    vb_ref[...] = vb.astype(BF16)
    gb_ref[...] = grp(7)
    qi_ref[...] = grp(8).astype(BF16)
    tail = _dot(h, w_ref[:, 9 * D_ATT:])
    ki32_ref[...] = tail[:, :D_IDX]
    wi_ref[...] = tail[:, LANES:]


def _project_sample(x, g, w, gains, ones_bd):
    n = x.shape[0]
    full = lambda a: pl.BlockSpec(a.shape, lambda i: (0,) * a.ndim)
    sd = lambda c, dt: jax.ShapeDtypeStruct((n, c), dt)
    out_shape = ([sd(D_ATT, BF16)] * 7 + [sd(LANES, F32)] + [sd(D_ATT, F32)] * 6 + [sd(D_IDX, F32)])
    return pl.pallas_call(
        _proj_sample_kernel,
        grid=(1,),
        in_specs=[full(x), full(g), full(w), full(gains), full(ones_bd)],
        out_specs=[full(s) for s in out_shape], out_shape=out_shape,
        compiler_params=_params(1), name="proj_sample",
    )(x, g, w, gains, ones_bd)


def _head_row_masks():
    r = lax.broadcasted_iota(I32, (LANES, 1), 0)
    return r < HEAD_DIM, r >= HEAD_DIM


def _pair_attention(i, qms, k_tile, v_tile, near_bias, far_bias, mask_own, mask_past):
    def update(tiles, carry):
        scores = [[mask(half, j, _dot(k_tile(j), qms[half]) + bias(half)) for j, bias, mask in tiles]
                  for half in range(2)]
        out = []
        for half in range(2):
            pvs = [lambda p, j=j, half=half: _dot(v_tile(j, half), p) for j, _, _ in tiles]
            out.append(_osm_update(scores[half], pvs, *carry[half], axis=0))
        return tuple(out)

    init = (jnp.full((1, TQ), NEG_INF, F32), jnp.zeros((1, TQ), F32), jnp.zeros((HEAD_DIM, TQ), F32))
    tiles = [(i, lambda half: near_bias(half, 0), lambda half, j, s: mask_own(half, s))]
    for d in range(1, 2 + FIRST_FAR):
        exists = i >= d
        tiles.append((jnp.maximum(i - d, 0), (lambda half: near_bias(half, 1)) if d == 1 else far_bias,
                      lambda half, j, s, exists=exists: mask_past(half, j, s, exists)))
    carry = update(tiles, (init, init))
    n_far = jnp.maximum(i - 1 - FIRST_FAR, 0)
    done = 0
    for width in (4, 2, 1):
        n_w = (n_far - done) // width
        carry = lax.fori_loop(
            0, n_w,
            lambda u, c, done=done, width=width: update(
                [(done + width * u + w, far_bias, lambda half, j, s: mask_past(half, j, s, True))
                 for w in range(width)], c),
            carry)
        done = done + n_w * width
    return [c[2] / c[1] for c in carry]


def _moba_prompt_kernel(c31_ref, qt_ref, k_ref, vt_ref, bias_ref, o_ref, kmean_ref, *, n_blk):
    hp = pl.program_id(1)
    i = pl.program_id(2)
    n_sub = kmean_ref.shape[0]

    @pl.when(i == 0)
    def _():
        kmean_ref[...] = jnp.zeros_like(kmean_ref)
        for j in range(n_blk):
            kb = k_ref[j * TQ:(j + 1) * TQ, :].astype(F32)
            kmean_ref[j:j + 1, :] = jnp.mean(kb, axis=0, keepdims=True)

    masks = _head_row_masks()
    qt = qt_ref[...]
    key = lax.broadcasted_iota(I32, (TQ, TQ), 0)
    qry = lax.broadcasted_iota(I32, (TQ, TQ), 1)
    blk = lax.broadcasted_iota(I32, (n_sub, TQ), 0)
    kmean = kmean_ref[...].astype(BF16)
    qms = [jnp.where(masks[half], qt, jnp.zeros_like(qt)) for half in range(2)]
    picks = [_top_blocks(jnp.where(blk < i, _dot(kmean, qm), NEG_INF), blk, axis=0) for qm in qms]
    c31 = [c31_ref[hp * 2 + half] for half in range(2)]

    def picked(half, j, s, exists):
        sel = (picks[half][0] == j) | (picks[half][1] == j) | (picks[half][2] == j)
        return jnp.where(sel if exists is True else sel & exists, s, NEG_INF)

    outs = _pair_attention(
        i, qms,
        k_tile=lambda j: k_ref[pl.ds(pl.multiple_of(j * TQ, TQ), TQ), :],
        v_tile=lambda j, half: vt_ref[j, half * HEAD_DIM:(half + 1) * HEAD_DIM, :],
        near_bias=lambda half, d: bias_ref[half, d],
        far_bias=lambda half: c31[half],
        mask_own=lambda half, s: jnp.where(key <= qry, s, NEG_INF),
        mask_past=picked)
    o_ref[...] = jnp.concatenate(outs, axis=0).T


def _moba_prompt(c31, qt, k, vt, bias, batch, seq):
    n_q = seq // TQ
    n_sub = -(-n_q // SUBLANES) * SUBLANES
    kern = functools.partial(_moba_prompt_kernel, n_blk=n_q)
    return pl.pallas_call(
        kern,
        grid=(batch, N_HEADS // 2, n_q),
        in_specs=[
            pl.BlockSpec(memory_space=pltpu.SMEM),
            pl.BlockSpec((None, None, LANES, TQ), lambda b, hp, i: (b, i, hp, 0)),
            pl.BlockSpec((seq, LANES), lambda b, hp, i: (b, hp)),
            pl.BlockSpec((None, n_q, LANES, TQ), lambda b, hp, i: (b, 0, hp, 0)),
            pl.BlockSpec((2, 2, TQ, TQ), lambda b, hp, i: (hp, 0, 0, 0)),
        ],
        out_specs=pl.BlockSpec((TQ, LANES), lambda b, hp, i: (b * n_q + i, hp)),
        out_shape=jax.ShapeDtypeStruct((batch * seq, D_ATT), F32),
        scratch_shapes=[pltpu.VMEM((n_sub, LANES), F32)],
        compiler_params=_params(3), name="moba_prompt",
    )(c31, qt, k, vt, bias)


def _dsa_prompt_kernel(c31_ref, qt_ref, qit_ref, wit_ref, kk_ref, k_ref, vt_ref, bias_ref, o_ref,
                       keys_ref, pen_ref, *, seq, k_top):
    i = pl.program_id(1)
    masks = _head_row_masks()
    key = lax.broadcasted_iota(I32, (TQ, TQ), 0)
    qry = lax.broadcasted_iota(I32, (TQ, TQ), 1)
    zero_b = jnp.zeros((LANES, TQ), BF16)

    def head_qt(ref, h):
        pair, half = divmod(h, 2)
        return jnp.where(masks[half], ref[pair * LANES:(pair + 1) * LANES, :], zero_b)

    def index_tile(j):
        kkj = kk_ref[pl.ds(pl.multiple_of(j * TQ, TQ), TQ), :]
        acc = jnp.zeros((TQ, TQ), F32)
        for h in range(H_IDX):
            acc = acc + wit_ref[h:h + 1, :] * jnp.maximum(_dot(kkj, head_qt(qit_ref, h)), 0.0)
        return _f32_key(acc)

    def index_body(j, c):
        keys_ref[j] = index_tile(j)
        return c

    lax.fori_loop(0, i, index_body, 0)
    keys_ref[i] = jnp.where(key <= qry, index_tile(i), INT_MIN)

    def count_fn(pred):
        def one(j):
            w = jnp.where(pred(keys_ref[j], key + j * TQ), 1.0, 0.0)
            return jnp.sum(w.reshape(TQ // SUBLANES, SUBLANES, TQ), axis=0)
        n_two = (i + 1) // 2
        part = lax.fori_loop(0, n_two, lambda u, p: p + one(2 * u) + one(2 * u + 1),
                             jnp.zeros((SUBLANES, TQ), F32))
        part = lax.fori_loop(2 * n_two, i + 1, lambda j, p: p + one(j), part)
        return jnp.sum(part, axis=0, keepdims=True)

    t, x = _select_threshold(count_fn, k_top, seq)

    def pen_body(j, c):
        kt = keys_ref[j]
        sel = (kt > t) | ((kt == t) & (key + j * TQ <= x))
        pen_ref[j] = jnp.where(sel, 0.0, NEG_INF)
        return c

    lax.fori_loop(0, i + 1, pen_body, 0)
    outs = []
    for pair in range(N_HEADS // 2):
        lanes = slice(pair * LANES, (pair + 1) * LANES)
        vrows = lambda half, pair=pair: slice((2 * pair + half) * HEAD_DIM, (2 * pair + half + 1) * HEAD_DIM)
        outs += _pair_attention(
            i, [head_qt(qt_ref, 2 * pair + half) for half in range(2)],
            k_tile=lambda j, lanes=lanes: k_ref[pl.ds(pl.multiple_of(j * TQ, TQ), TQ), lanes],
            v_tile=lambda j, half, vrows=vrows: vt_ref[j, vrows(half), :],
            near_bias=lambda half, d, pair=pair: bias_ref[2 * pair + half, d],
            far_bias=lambda half, pair=pair: c31_ref[N_HEADS + 2 * pair + half],
            mask_own=lambda half, s: s + pen_ref[i],
            mask_past=lambda half, j, s, exists: (
                s + pen_ref[j] if exists is True else jnp.where(exists, s + pen_ref[j], NEG_INF)))
    o_ref[...] = jnp.concatenate(outs, axis=0).T


def _dsa_prompt(c31, qt, qit, wit, kk, k, vt, bias, batch, seq):
    n_q = seq // TQ
    k_top = min(DSA_TOPK_MAX, seq // 4)
    kern = functools.partial(_dsa_prompt_kernel, seq=seq, k_top=k_top)
    tile = lambda r: pl.BlockSpec((None, None, r, TQ), lambda b, i: (b, i, 0, 0))
    rows = lambda c: pl.BlockSpec((seq, c), lambda b, i: (b, 0))
    return pl.pallas_call(
        kern,
        grid=(batch, n_q),
        in_specs=[
            pl.BlockSpec(memory_space=pltpu.SMEM),
            tile(D_ATT), tile(D_ATT), tile(H_IDX),
            rows(LANES), rows(D_ATT),
            pl.BlockSpec((None, n_q, D_ATT, TQ), lambda b, i: (b, 0, 0, 0)),
            pl.BlockSpec((N_HEADS, 2, TQ, TQ), lambda b, i: (1, 0, 0, 0)),
        ],
        out_specs=pl.BlockSpec((TQ, D_ATT), lambda b, i: (b * n_q + i, 0)),
        out_shape=jax.ShapeDtypeStruct((batch * seq, D_ATT), F32),
        scratch_shapes=[pltpu.VMEM((n_q, TQ, TQ), I32), pltpu.VMEM((n_q, TQ, TQ), F32)],
        compiler_params=_params(2), name="dsa_prompt",
    )(c31, qt, qit, wit, kk, k, vt, bias)


def _expand_tokens(a, n_tok):
    return jnp.broadcast_to(a[:, None, :], (n_tok, N_HEADS, a.shape[-1])).reshape(n_tok * N_HEADS, a.shape[-1])


def _collapse_heads(o, hmask, n_tok):
    return jnp.sum((o * hmask).reshape(n_tok, N_HEADS, D_ATT), axis=1)


def _sample_index_kernel(pt_ref, *refs, n_pages, n_tok, k_top, n_pp):
    kit_refs = refs[:n_pp]
    qi_ref, wcol_ref, knew_ref, pen_ref, keys_ref = refs[n_pp:]
    g = pl.program_id(1)

    def keys_of(kt):
        x = jnp.maximum(_dot(qi_ref[...], kt), 0.0) * wcol_ref[...]
        return _f32_key(jnp.sum(x.reshape(n_tok, H_IDX, kt.shape[1]), axis=1))

    keys = keys_of(jnp.concatenate([r[...] for r in kit_refs], axis=1).astype(BF16))
    for o in range(n_pp):
        keys_ref[g * n_pp + o] = keys[:, o * LANES:(o + 1) * LANES]

    @pl.when(g == n_pages // n_pp - 1)
    def _():
        lane = lax.broadcasted_iota(I32, (n_tok, LANES), 1)
        tok = lax.broadcasted_iota(I32, (n_tok, LANES), 0)
        keys_ref[n_pages] = jnp.where(lane <= tok, keys_of(knew_ref[...]), INT_MIN)
        shape = (n_pages + 1, n_tok, LANES)
        pos = lax.broadcasted_iota(I32, shape, 0) * LANES + lax.broadcasted_iota(I32, shape, 2)
        all_keys = keys_ref[...]

        def count_fn(pred):
            w = jnp.where(pred(all_keys, pos), 1.0, 0.0)
            terms = [w[j] for j in range(n_pages + 1)]
            while len(terms) > 1:
                terms = [a + b for a, b in zip(terms[::2], terms[1::2])] + terms[len(terms) & ~1:]
            return jnp.sum(terms[0], axis=1, keepdims=True)

        t, x = _select_threshold(count_fn, k_top, (n_pages + 1) * LANES)
        sel = (all_keys > t) | ((all_keys == t) & (pos <= x))
        pen_ref[...] = jnp.where(sel, 0.0, NEG_INF)


def _sample_index(page_table, kit, qi, wcol, knew, n_tok, k_top):
    n_seq, n_pages = page_table.shape
    n_pp = math.gcd(IDX_PAGES_PER_STEP, n_pages)
    kern = functools.partial(_sample_index_kernel, n_pages=n_pages, n_tok=n_tok, k_top=k_top, n_pp=n_pp)
    per_seq = lambda a: pl.BlockSpec((None,) + a.shape[1:], lambda b, g, pt: (b,) + (0,) * (a.ndim - 1))
    page = lambda o: pl.BlockSpec((None, D_IDX, PAGE_SIZE), lambda b, g, pt: (pt[b, g * n_pp + o], 0, 0))
    grid_spec = pltpu.PrefetchScalarGridSpec(
        num_scalar_prefetch=1, grid=(n_seq, n_pages // n_pp),
        in_specs=[page(o) for o in range(n_pp)] + [per_seq(qi), per_seq(wcol), per_seq(knew)],
        out_specs=pl.BlockSpec((None, n_pages + 1, n_tok, LANES), lambda b, g, pt: (b, 0, 0, 0)),
        scratch_shapes=[pltpu.VMEM((n_pages + 1, n_tok, LANES), I32)])
    return pl.pallas_call(
        kern, grid_spec=grid_spec,
        out_shape=jax.ShapeDtypeStruct((n_seq, n_pages + 1, n_tok, LANES), F32),
        compiler_params=_params(2), name="sample_index",
    )(page_table, *([kit] * n_pp), qi, wcol, knew)


def _sample_dsa_kernel(pt_ref, *refs, n_pages, n_tok, n_pp):
    kt_refs, vt_refs = refs[:n_pp], refs[n_pp:2 * n_pp]
    (q_ref, pen_ref, blast_ref, c31_ref, knew_ref, vnew_ref, bnew_ref, hmask_ref,
     o_ref, m_ref, l_ref, acc_ref) = refs[2 * n_pp:]
    g = pl.program_id(1)
    last = g == n_pages // n_pp - 1

    @pl.when(g == 0)
    def _():
        m_ref[...] = jnp.full(m_ref.shape, NEG_INF, F32)
        l_ref[...] = jnp.zeros(l_ref.shape, F32)
        acc_ref[...] = jnp.zeros(acc_ref.shape, F32)

    q = q_ref[...]
    scores, pvs = [], []
    for o in range(n_pp):
        bias = c31_ref[...] if o < n_pp - 1 else jnp.where(last, blast_ref[...], c31_ref[...])
        scores.append(_dot(q, kt_refs[o][...].astype(BF16)) + bias + _expand_tokens(pen_ref[g * n_pp + o], n_tok))
        pvs.append(lambda pp, o=o: _dot_nt(pp, vt_refs[o][...].astype(BF16)))
    m, l, acc = _osm_update(scores, pvs, m_ref[...], l_ref[...], acc_ref[...], axis=1)
    m_ref[...] = m
    l_ref[...] = l
    acc_ref[...] = acc

    @pl.when(last)
    def _():
        pen_new = _expand_tokens(pen_ref[n_pages][:, :NEW_PAD], n_tok)
        s_n = _dot_nt(q, knew_ref[...]) + bnew_ref[...] + pen_new
        _, l2, acc2 = _osm_update([s_n], [lambda pp: _dot(pp, vnew_ref[...])], m, l, acc, axis=1)
        o_ref[...] = _collapse_heads(acc2 / l2, hmask_ref[...], n_tok)


def _sample_dsa(page_table, kt, vt, q, pen, blast, c31col, knew, vnew, bnew, hmask, n_tok):
    n_seq, n_pages = page_table.shape
    n_pp = math.gcd(PAGES_PER_STEP, n_pages)
    rows = n_tok * N_HEADS
    kern = functools.partial(_sample_dsa_kernel, n_pages=n_pages, n_tok=n_tok, n_pp=n_pp)
    per_seq = lambda a: pl.BlockSpec((None,) + a.shape[1:], lambda b, g, pt: (b,) + (0,) * (a.ndim - 1))
    full = lambda a: pl.BlockSpec(a.shape, lambda b, g, pt: (0,) * a.ndim)
    page = lambda o: pl.BlockSpec((None, D_ATT, PAGE_SIZE), lambda b, g, pt: (pt[b, g * n_pp + o], 0, 0))
    pages = [page(o) for o in range(n_pp)]
    grid_spec = pltpu.PrefetchScalarGridSpec(
        num_scalar_prefetch=1, grid=(n_seq, n_pages // n_pp),
        in_specs=pages + pages + [per_seq(q), per_seq(pen), full(blast), full(c31col),
                                  per_seq(knew), per_seq(vnew), full(bnew), full(hmask)],
        out_specs=pl.BlockSpec((None, n_tok, D_ATT), lambda b, g, pt: (b, 0, 0)),
        scratch_shapes=[pltpu.VMEM((rows, 1), F32), pltpu.VMEM((rows, 1), F32), pltpu.VMEM((rows, D_ATT), F32)])
    return pl.pallas_call(
        kern, grid_spec=grid_spec,
        out_shape=jax.ShapeDtypeStruct((n_seq, n_tok, D_ATT), F32),
        compiler_params=_params(2), name="sample_dsa",
    )(page_table, *([kt] * n_pp), *([vt] * n_pp), q, pen, blast, c31col, knew, vnew, bnew, hmask)


def _sample_moba_kernel(pt_ref, *refs, n_blk, n_tok, n_pp):
    kt_refs, vt_refs = refs[:n_pp], refs[n_pp:2 * n_pp]
    (q_ref, blast_ref, c31_ref, knew_ref, vnew_ref, bnew_ref, hmask_ref,
     o_ref, gate_ref, ms_ref, ls_ref, accs_ref) = refs[2 * n_pp:]
    g = pl.program_id(1)
    n_bb = n_pp // 2
    last = g == n_blk // n_bb - 1
    rows = n_tok * N_HEADS
    lane = lax.broadcasted_iota(I32, (rows, LANES), 1)

    @pl.when(g == 0)
    def _():
        gate_ref[...] = jnp.full(gate_ref.shape, NEG_INF, F32)
        ms_ref[...] = jnp.full(ms_ref.shape, NEG_INF, F32)
        ls_ref[...] = jnp.zeros(ls_ref.shape, F32)

    q = q_ref[...]
    gates, ms, ls = gate_ref[...], ms_ref[...], ls_ref[...]
    for bb in range(n_bb):
        j = g * n_bb + bb
        r0 = _dot(q, kt_refs[2 * bb][...].astype(BF16))
        r1 = _dot(q, kt_refs[2 * bb + 1][...].astype(BF16))
        gate = jnp.sum(r0 + r1, axis=1, keepdims=True)
        s0 = r0 + c31_ref[...]
        s1 = r1 + (c31_ref[...] if bb < n_bb - 1 else jnp.where(last, blast_ref[...], c31_ref[...]))
        m = jnp.maximum(jnp.max(s0, axis=1, keepdims=True), jnp.max(s1, axis=1, keepdims=True))
        p0 = jnp.exp(s0 - m)
        p1 = jnp.exp(s1 - m)
        l = jnp.sum(p0, axis=1, keepdims=True) + jnp.sum(p1, axis=1, keepdims=True)
        accs_ref[j] = (_dot_nt(p0.astype(BF16), vt_refs[2 * bb][...].astype(BF16))
                       + _dot_nt(p1.astype(BF16), vt_refs[2 * bb + 1][...].astype(BF16)))
        gates = jnp.where(lane == j, gate, gates)
        ms = jnp.where(lane == j, m, ms)
        ls = jnp.where(lane == j, l, ls)
    gate_ref[...] = gates
    ms_ref[...] = ms
    ls_ref[...] = ls

    @pl.when(last)
    def _():
        picks = _top_blocks(jnp.where(lane < n_blk, gates, NEG_INF), lane, axis=1)
        sel = (lane == picks[0]) | (lane == picks[1]) | (lane == picks[2])
        s_n = _dot_nt(q, knew_ref[...]) + bnew_ref[...]
        m_o = jnp.max(s_n, axis=1, keepdims=True)
        p_o = jnp.exp(s_n - m_o)
        l_o = jnp.sum(p_o, axis=1, keepdims=True)
        m_all = jnp.maximum(m_o, jnp.max(jnp.where(sel, ms, NEG_INF), axis=1, keepdims=True))
        w = jnp.where(sel, jnp.exp(ms - m_all), 0.0)
        w_o = jnp.exp(m_o - m_all)
        l_all = jnp.sum(w * ls, axis=1, keepdims=True) + w_o * l_o
        out = w_o * _dot(p_o.astype(BF16), vnew_ref[...])
        for jj in range(n_blk):
            out = out + w[:, jj:jj + 1] * accs_ref[jj]
        o_ref[...] = _collapse_heads(out / l_all, hmask_ref[...], n_tok)


def _sample_moba(page_table, kt, vt, q, blast, c31col, knew, vnew, bnew, hmask, n_tok):
    n_seq, n_pages = page_table.shape
    pages_per_blk = MOBA_BLOCK // PAGE_SIZE
    n_blk = n_pages // pages_per_blk
    n_pp = pages_per_blk * math.gcd(PAGES_PER_STEP // pages_per_blk, n_blk)
    rows = n_tok * N_HEADS
    kern = functools.partial(_sample_moba_kernel, n_blk=n_blk, n_tok=n_tok, n_pp=n_pp)
    per_seq = lambda a: pl.BlockSpec((None,) + a.shape[1:], lambda b, g, pt: (b,) + (0,) * (a.ndim - 1))
    full = lambda a: pl.BlockSpec(a.shape, lambda b, g, pt: (0,) * a.ndim)
    page = lambda o: pl.BlockSpec((None, D_ATT, PAGE_SIZE), lambda b, g, pt: (pt[b, g * n_pp + o], 0, 0))
    pages = [page(o) for o in range(n_pp)]
    grid_spec = pltpu.PrefetchScalarGridSpec(
        num_scalar_prefetch=1, grid=(n_seq, n_pages // n_pp),
        in_specs=pages + pages + [per_seq(q), full(blast), full(c31col),
                                  per_seq(knew), per_seq(vnew), full(bnew), full(hmask)],
        out_specs=pl.BlockSpec((None, n_tok, D_ATT), lambda b, g, pt: (b, 0, 0)),
        scratch_shapes=[pltpu.VMEM((rows, LANES), F32)] * 3 + [pltpu.VMEM((n_blk, rows, D_ATT), F32)])
    return pl.pallas_call(
        kern, grid_spec=grid_spec,
        out_shape=jax.ShapeDtypeStruct((n_seq, n_tok, D_ATT), F32),
        compiler_params=_params(2), name="sample_moba",
    )(page_table, *([kt] * n_pp), *([vt] * n_pp), q, blast, c31col, knew, vnew, bnew, hmask)


def _merge_and_conv_in(x_ref, oa_ref, ob_ref, ga_ref, gb_ref, wo_ref, g1_ref, wc_ref):
    o = jnp.concatenate([oa_ref[...] * _silu(ga_ref[...]), ob_ref[...] * _silu(gb_ref[...])], axis=1)
    h1 = x_ref[...] + _dot(o.astype(BF16), wo_ref[...])
    hn = _rms(h1, g1_ref[...]).astype(BF16)
    a = _dot(hn, wc_ref[:, 0:D_MODEL])
    b = _dot(hn, wc_ref[:, D_MODEL:2 * D_MODEL])
    g = _dot(hn, wc_ref[:, 2 * D_MODEL:3 * D_MODEL])
    return h1, a * _sigmoid(b), g


def _ln_gate_out(c, g, h1, lg_ref, lb_ref, wo2_ref):
    mu = jnp.mean(c, axis=-1, keepdims=True)
    d = c - mu
    var = jnp.mean(d * d, axis=-1, keepdims=True)
    cn = d * lax.rsqrt(var + LN_EPS) * lg_ref[...] + lb_ref[...]
    z = (_silu(cn) * _silu(g)).astype(BF16)
    return h1 + _dot(z, wo2_ref[...])


def _layer1_prompt_kernel(x_ref, oa_ref, ob_ref, ga_ref, gb_ref, wo_ref, g1_ref, wc_ref,
                          cw_ref, cb_ref, lg_ref, lb_ref, wo2_ref, y_ref, st_ref, up_ref, c_ref, *, tt):
    t = pl.program_id(1)

    @pl.when(t == 0)
    def _():
        up_ref[0:HIST, :] = jnp.zeros((HIST, D_MODEL), F32)

    h1, u, g = _merge_and_conv_in(x_ref, oa_ref, ob_ref, ga_ref, gb_ref, wo_ref, g1_ref, wc_ref)
    up_ref[HIST:HIST + tt, :] = u
    base = HIST - (CONV_WIDTH - 1)
    rows = min(tt, CONV_ROWS)
    for cc in range(0, D_MODEL, LANES):
        cs = slice(cc, cc + LANES)
        for r0 in range(0, tt, rows):
            acc = jnp.zeros((rows, LANES), F32) + cb_ref[:, cs]
            for phase in range(SUBLANES):
                taps = [w for w in range(CONV_WIDTH) if (base + w) % SUBLANES == phase]
                n_win = rows + (SUBLANES if phase else 0)
                group = None
                for w in taps:
                    start = r0 + base + w - phase
                    term = cw_ref[w:w + 1, cs] * up_ref[start:start + n_win, cs]
                    group = term if group is None else group + term
                if group is not None:
                    acc = acc + group[phase:phase + rows, :]
            c_ref[r0:r0 + rows, cs] = acc
    y_ref[...] = _ln_gate_out(c_ref[...], g, h1, lg_ref, lb_ref, wo2_ref)
    tail = up_ref[tt:tt + HIST, :]
    st_ref[...] = tail
    up_ref[0:HIST, :] = tail


def _layer1_prompt(x, oa, ob, ga, gb, wo, g1, wc, cw, cb, lg, lb, wo2, batch, seq, tt):
    n_t = seq // tt
    kern = functools.partial(_layer1_prompt_kernel, tt=tt)
    row = lambda c: pl.BlockSpec((tt, c), lambda b, t: (b * n_t + t, 0))
    full = lambda a: pl.BlockSpec(a.shape, lambda b, t: (0,) * a.ndim)
    return pl.pallas_call(
        kern,
        grid=(batch, n_t),
        in_specs=[row(D_MODEL), row(D_ATT), row(D_ATT), row(D_ATT), row(D_ATT),
                  full(wo), full(g1), full(wc), full(cw), full(cb), full(lg), full(lb), full(wo2)],
        out_specs=[row(D_MODEL), pl.BlockSpec((None, HIST, D_MODEL), lambda b, t: (b, 0, 0))],
        out_shape=[jax.ShapeDtypeStruct((batch * seq, D_MODEL), F32),
                   jax.ShapeDtypeStruct((batch, HIST, D_MODEL), F32)],
        scratch_shapes=[pltpu.VMEM((HIST + tt, D_MODEL), F32), pltpu.VMEM((tt, D_MODEL), F32)],
        compiler_params=_params(2), name="layer1_prompt",
    )(x, oa, ob, ga, gb, wo, g1, wc, cw, cb, lg, lb, wo2)


def _layer1_sample_kernel(x_ref, oa_ref, ob_ref, ga_ref, gb_ref, wo_ref, g1_ref, wc_ref,
                          cw_ref, cb_ref, lg_ref, lb_ref, wo2_ref, st_ref, y_ref, ns_ref, c_ref, *, n_tok, n_seq):
    n_hist = CONV_WIDTH - 1
    h1, u, g = _merge_and_conv_in(x_ref, oa_ref, ob_ref, ga_ref, gb_ref, wo_ref, g1_ref, wc_ref)
    u_t = [u[t * n_seq:(t + 1) * n_seq, :] for t in range(n_tok)]
    for t in range(n_tok):
        acc = jnp.zeros((n_seq, D_MODEL), F32) + cb_ref[...]
        for r in range(t, n_hist):
            acc = acc + cw_ref[r - t:r - t + 1, :] * st_ref[r]
        for r in range(t + 1):
            acc = acc + cw_ref[n_hist - t + r:n_hist - t + r + 1, :] * u_t[r]
        c_ref[t * n_seq:(t + 1) * n_seq, :] = acc
    y_ref[...] = _ln_gate_out(c_ref[...], g, h1, lg_ref, lb_ref, wo2_ref)
    for r in range(n_hist - n_tok):
        ns_ref[r] = st_ref[r + n_tok]
    for t in range(n_tok):
        ns_ref[n_hist - n_tok + t] = u_t[t]


def _layer1_sample(x, oa, ob, ga, gb, wo, g1, wc, cw, cb, lg, lb, wo2, st, n_tok, n_seq):
    kern = functools.partial(_layer1_sample_kernel, n_tok=n_tok, n_seq=n_seq)
    full = lambda a: pl.BlockSpec(a.shape, lambda i: (0,) * a.ndim)
    args = (x, oa, ob, ga, gb, wo, g1, wc, cw, cb, lg, lb, wo2, st)
    out_shape = [jax.ShapeDtypeStruct(x.shape, F32), jax.ShapeDtypeStruct(st.shape, F32)]
    return pl.pallas_call(
        kern, grid=(1,),
        in_specs=[full(a) for a in args],
        out_specs=[full(s) for s in out_shape], out_shape=out_shape,
        scratch_shapes=[pltpu.VMEM(x.shape, F32)],
        compiler_params=_params(1), name="layer1_sample",
    )(*args)


def _t5_bucket(n):
    exact = N_BUCKETS // 2
    nf = jnp.maximum(n, 1).astype(F32)
    large = exact + (jnp.log(nf / exact) / math.log(MAX_DISTANCE / exact) * (N_BUCKETS - exact)).astype(I32)
    large = jnp.minimum(large, N_BUCKETS - 1)
    return jnp.where(n < exact, n, large)


def _bias_by_distance(table, n):
    return table[_t5_bucket(jnp.arange(n, dtype=I32))]


def _prep_attn_weights(w_in, qn_a, kn_a, qn_b, kn_b):
    grp = lambda c: w_in[:, c * D_ATT:(c + 1) * D_ATT]
    ki = w_in[:, 9 * D_ATT:9 * D_ATT + D_IDX]
    wi = w_in[:, 9 * D_ATT + D_IDX:]
    zeros = lambda c: jnp.zeros((D_MODEL, c), w_in.dtype)
    tile = lambda g: jnp.tile(g, N_HEADS)
    head = np.arange(D_ATT) // HEAD_DIM
    ones_bd = jnp.asarray(head[:, None] == head[None, :], dtype=BF16)
    wn = jnp.concatenate([grp(3), grp(7)], axis=1).astype(BF16)
    wt = jnp.concatenate([grp(0), grp(1), grp(2), grp(4), grp(5), grp(6), grp(8), ki, ki, wi,
                          zeros(NEW_PAD - H_IDX)], axis=1).T.astype(BF16)
    gains_t = jnp.stack([tile(qn_a), tile(kn_a), tile(qn_b), tile(kn_b)]).astype(F32)[:, :, None]
    ws = jnp.concatenate([w_in[:, :9 * D_ATT + D_IDX], zeros(LANES - D_IDX), wi, zeros(LANES - H_IDX)],
                         axis=1).astype(BF16)
    gains_s = jnp.stack([tile(qn_a), tile(kn_a), tile(qn_b), tile(kn_b)]).astype(F32)
    return wn, wt, gains_t, ws, gains_s, ones_bd


def kernel(x_prompt, x_sample, cache_k_a, cache_v_a, cache_k_b, cache_v_b, cache_kidx_b, state_conv, page_table,
           norm_g, rel_bias_table, w_in_attn, q_norm_a, k_norm_a, q_norm_b, k_norm_b, w_out_attn,
           w_in_conv, conv_w, conv_b, conv_ln_g, conv_ln_b, w_out_conv):
    batch, seq, _ = x_prompt.shape
    n_seq, n_tok, _ = x_sample.shape
    n_pages = page_table.shape[1]
    past = n_pages * PAGE_SIZE
    assert seq % TQ == 0 and past % MOBA_BLOCK == 0 and n_tok <= NEW_PAD and TQ >= MAX_DISTANCE
    n_hist = CONV_WIDTH - 1
    wn, wt, gains_t, ws, gains_s, ones_bd = _prep_attn_weights(
        w_in_attn[0], q_norm_a[0], k_norm_a[0], q_norm_b[0], k_norm_b[0])
    g0 = norm_g[0][None, :]
    g1 = norm_g[1][None, :]
    wo = w_out_attn[0].astype(BF16)
    wc = w_in_conv[0].astype(BF16)
    wo2 = w_out_conv[0].astype(BF16)
    cw = jnp.concatenate([conv_w[0], jnp.zeros((HIST - CONV_WIDTH, D_MODEL), F32)], axis=0)
    conv_rows = (cw, conv_b[0][None, :], conv_ln_g[0][None, :], conv_ln_b[0][None, :])
    c31 = rel_bias_table[N_BUCKETS - 1]
    bias_tiles = _bias_tiles(rel_bias_table)

    xp = x_prompt.reshape(batch * seq, D_MODEL)
    (ga, gb, katm, kbtm, kktm, qat, qbt, qit, wit, vat, vbt, ka32, va32, kb32, vb32, ki32) = _project_prompt(
        xp, g0, wn, wt, gains_t, batch, seq)
    oa = _moba_prompt(c31, qat, katm, vat, bias_tiles, batch, seq)
    ob = _dsa_prompt(c31, qbt, qit, wit, kktm, kbtm, vbt, bias_tiles, batch, seq)
    y_p, st_p = _layer1_prompt(xp, oa, ob, ga, gb, wo, g1, wc, *conv_rows, wo2, batch, seq, tt=TQ)
    heads_p = lambda a: jnp.transpose(a.reshape(batch, N_HEADS, HEAD_DIM, seq), (0, 3, 1, 2))[None]
    ki_p = jnp.transpose(ki32, (0, 2, 1))[None]
    cv_p = st_p[None, :, HIST - n_hist:, :]

    rows = n_tok * N_HEADS
    xs = x_sample.reshape(n_seq * n_tok, D_MODEL)
    (qa_s, ka_s, va_s, qb_s, kb_s, vb_s, qi_s, wi_s, ga_s, gb_s, ka32_s, va32_s, kb32_s, vb32_s, ki32_s) = \
        _project_sample(xs, g0, ws, gains_s, ones_bd)
    head_of_lane = np.arange(D_ATT) // HEAD_DIM
    hmask8 = (np.arange(N_HEADS)[:, None] == head_of_lane[None, :])
    hmask = jnp.asarray(np.tile(hmask8, (n_tok, 1)), F32)
    qbd = lambda q: (q.reshape(n_seq, n_tok, 1, D_ATT) * jnp.asarray(hmask8, BF16)).reshape(n_seq, rows, D_ATT)
    newpad = lambda a: jnp.pad(a.reshape(n_seq, n_tok, D_ATT), ((0, 0), (0, NEW_PAD - n_tok), (0, 0)))
    bd = _bias_by_distance(rel_bias_table, 2 * PAGE_SIZE)
    t_of_row = np.arange(rows) // N_HEADS
    h_of_row = np.arange(rows) % N_HEADS
    c_idx = np.arange(PAGE_SIZE)
    dist_last = PAGE_SIZE + t_of_row[:, None] - c_idx[None, :]
    tn = np.arange(NEW_PAD)
    dist_new = np.maximum(t_of_row[:, None] - tn[None, :], 0)
    valid_new = (tn[None, :] <= t_of_row[:, None]) & (tn[None, :] < n_tok)

    def sample_bias(off):
        hh = h_of_row + off
        blast = bd[dist_last, hh[:, None]]
        bnew = jnp.where(valid_new, bd[dist_new, hh[:, None]], NEG_INF)
        return blast, c31[hh][:, None], bnew

    pages = lambda c: jnp.transpose(c[0], (0, 2, 3, 1)).reshape(c.shape[1], D_ATT, PAGE_SIZE)
    kit = jnp.transpose(cache_kidx_b[0], (0, 2, 1))
    blast_a, c31_a, bnew_a = sample_bias(0)
    oa_s = _sample_moba(page_table, pages(cache_k_a), pages(cache_v_a), qbd(qa_s), blast_a, c31_a,
                        newpad(ka_s), newpad(va_s), bnew_a, hmask, n_tok)
    k_top = min(DSA_TOPK_MAX, (past + n_tok) // 4)
    knew_i = jnp.pad(jnp.transpose(ki32_s.reshape(n_seq, n_tok, D_IDX), (0, 2, 1)),
                     ((0, 0), (0, 0), (0, PAGE_SIZE - n_tok))).astype(BF16)
    pen = _sample_index(page_table, kit, qi_s.reshape(n_seq, rows, D_IDX),
                        wi_s[:, :H_IDX].reshape(n_seq, rows, 1), knew_i, n_tok, k_top)
    blast_b, c31_b, bnew_b = sample_bias(N_HEADS)
    ob_s = _sample_dsa(page_table, pages(cache_k_b), pages(cache_v_b), qbd(qb_s), pen, blast_b, c31_b,
                       newpad(kb_s), newpad(vb_s), bnew_b, hmask, n_tok)
    tmaj = lambda a: jnp.transpose(a.reshape(n_seq, n_tok, -1), (1, 0, 2)).reshape(n_tok * n_seq, -1)
    st_in = jnp.transpose(state_conv[0], (1, 0, 2))
    y_s, ns = _layer1_sample(tmaj(xs), tmaj(oa_s), tmaj(ob_s), tmaj(ga_s), tmaj(gb_s), wo, g1, wc,
                             *conv_rows, wo2, st_in, n_tok, n_seq)
    y_s = jnp.transpose(y_s.reshape(n_tok, n_seq, D_MODEL), (1, 0, 2))
    cv_s = jnp.transpose(ns, (1, 0, 2))[None]
    heads_s = lambda a: a.reshape(1, n_seq, n_tok, N_HEADS, HEAD_DIM)

    return (y_p.reshape(batch, seq, D_MODEL), y_s,
            heads_p(ka32), heads_p(va32), heads_p(kb32), heads_p(vb32), ki_p, cv_p,
            heads_s(ka32_s), heads_s(va32_s), heads_s(kb32_s), heads_s(vb32_s),
            ki32_s.reshape(1, n_seq, n_tok, D_IDX), cv_s)
```

```python
import functools
import math

import numpy as np
import jax
import jax.numpy as jnp
from jax import lax
from jax.experimental import pallas as pl
from jax.experimental.pallas import tpu as pltpu

F32 = jnp.float32
BF16 = jnp.bfloat16
I32 = jnp.int32

D_MODEL = 1024
HEAD_DIM = 64
N_HEADS = 8
D_ATT = N_HEADS * HEAD_DIM
MOBA_BLOCK = 256
MOBA_TOPK = 3
DSA_TOPK_MAX = 256
H_IDX = 8
D_IDX = 64
CONV_WIDTH = 31
N_BUCKETS = 32
MAX_DISTANCE = 128
PAGE_SIZE = 128
RMS_EPS = 1e-6
LN_EPS = 1e-5

LANES = 128
SUBLANES = 8
TQ = MOBA_BLOCK
HIST = 32
NEW_PAD = 16
PAGES_PER_STEP = 16
IDX_PAGES_PER_STEP = 16
CONV_ROWS = 128
FIRST_FAR = 2
INT_MIN = -2 ** 31
NEG_INF = float("-inf")
LOG2E = math.log2(math.e)
VMEM_LIMIT = 56 * 1024 * 1024


def _dot(a, b):
    return jnp.dot(a, b, preferred_element_type=F32)


def _dot_nt(a, b):
    return lax.dot_general(a, b, (((1,), (1,)), ((), ())), preferred_element_type=F32)


def _sigmoid(x):
    return 1.0 / (1.0 + jnp.exp(-x))


def _silu(x):
    return x * _sigmoid(x)


def _rms(x, g):
    return x * lax.rsqrt(jnp.mean(x * x, axis=-1, keepdims=True) + RMS_EPS) * g


def _osm_update(scores, pvs, m, l, acc, axis, exp=jnp.exp):
    m_new = m
    for s in scores:
        m_new = jnp.maximum(m_new, jnp.max(s, axis=axis, keepdims=True))
    m_safe = jnp.where(m_new == NEG_INF, 0.0, m_new)
    alpha = exp(m - m_safe)
    l = alpha * l
    acc = alpha * acc
    for s, pv in zip(scores, pvs):
        p = exp(s - m_safe)
        l = l + jnp.sum(p, axis=axis, keepdims=True)
        acc = acc + pv(p.astype(BF16))
    return m_new, l, acc


def _top_blocks(g, idx, axis):
    picks = []
    for _ in range(MOBA_TOPK):
        mx = jnp.max(g, axis=axis, keepdims=True)
        a = jnp.min(jnp.where(g == mx, idx, LANES * LANES), axis=axis, keepdims=True)
        a = jnp.where(mx == NEG_INF, -1, a)
        picks.append(a)
        g = jnp.where(idx == a, NEG_INF, g)
    return picks


def _f32_key(x):
    b = lax.bitcast_convert_type(x, I32)
    k = b ^ ((b >> 31) & 0x7FFFFFFF)
    return jnp.where(x == 0.0, 0, k)


def _select_threshold(count_fn, k_top, n_keys):
    k_f = float(k_top)
    c0 = count_fn(lambda kt, pos: kt >= 0)
    t = jnp.where(c0 >= k_f, 0, INT_MIN).astype(I32)
    cnt_ge = jnp.where(c0 >= k_f, c0, 0.0)

    def bit_step(b, carry):
        t, cnt_ge = carry
        cand = t | jnp.left_shift(jnp.int32(1), 30 - b)
        cnt = count_fn(lambda kt, pos: kt >= cand)
        return jnp.where(cnt >= k_f, cand, t), jnp.where(cnt >= k_f, cnt, cnt_ge)

    t, cnt_ge = lax.fori_loop(0, 31, bit_step, (t, cnt_ge))
    t = jnp.maximum(t, INT_MIN + 1)
    n_bits = max(1, int(math.ceil(math.log2(n_keys))))

    def tie_search():
        need = k_f - count_fn(lambda kt, pos: kt > t)

        def pos_step(b, x):
            cand = x | jnp.left_shift(jnp.int32(1), n_bits - 1 - b)
            below = count_fn(lambda kt, pos: (kt == t) & (pos < cand))
            return jnp.where(below < need, cand, x)
        return lax.fori_loop(0, n_bits, pos_step, jnp.zeros_like(t))

    any_excess = jnp.max(jnp.where(cnt_ge > k_f, 1.0, 0.0)) > 0.0
    x = lax.cond(any_excess, tie_search, lambda: jnp.full_like(t, n_keys))
    return t, x


def _params(n):
    return pltpu.CompilerParams(dimension_semantics=("arbitrary",) * n, vmem_limit_bytes=VMEM_LIMIT)


def _bucket_thresholds():
    exact = N_BUCKETS // 2
    n = np.arange(MAX_DISTANCE + 1)
    nf = np.maximum(n, 1).astype(np.float32)
    large = exact + (np.log(nf / np.float32(exact)) / np.float32(math.log(MAX_DISTANCE / exact))
                     * np.float32(N_BUCKETS - exact)).astype(np.int32)
    bucket = np.where(n < exact, n, np.minimum(large, N_BUCKETS - 1))
    assert bucket[MAX_DISTANCE] == N_BUCKETS - 1 and np.all(np.diff(bucket) >= 0)
    return [int(np.argmax(bucket >= k)) for k in range(N_BUCKETS)]


def _bias_tiles_kernel(tab_ref, o_ref):
    h = pl.program_id(0)
    thr = _bucket_thresholds()
    key = lax.broadcasted_iota(I32, (TQ, TQ), 0)
    qry = lax.broadcasted_iota(I32, (TQ, TQ), 1)
    for d in range(2):
        dist = d * TQ + qry - key
        val = jnp.full((TQ, TQ), tab_ref[0, h], F32)
        for k in range(1, N_BUCKETS):
            val = jnp.where(dist >= thr[k], tab_ref[k, h], val)
        o_ref[d] = (val - tab_ref[N_BUCKETS - 1, h]) * LOG2E


def _bias_tiles(table):
    n_h = table.shape[1]
    return pl.pallas_call(
        _bias_tiles_kernel,
        grid=(n_h,),
        in_specs=[pl.BlockSpec(memory_space=pltpu.SMEM)],
        out_specs=pl.BlockSpec((None, 2, TQ, TQ), lambda h: (h, 0, 0, 0)),
        out_shape=jax.ShapeDtypeStruct((n_h, 2, TQ, TQ), F32),
        compiler_params=_params(1), name="bias_tiles",
    )(table)


def _proj_prompt_kernel(x_ref, g_ref, wn_ref, wt_ref, gains_ref,
                        ga_ref, gb_ref, katm_ref, kbtm_ref, kktm_ref,
                        qat_ref, qbt_ref, qit_ref, wit_ref, vat_ref, vbt_ref,
                        ka32_ref, va32_ref, kb32_ref, vb32_ref, ki32_ref):
    h = _rms(x_ref[...], g_ref[...]).astype(BF16)
    tm = h.shape[0]
    scale = HEAD_DIM ** -0.5 * LOG2E

    def grp_t(c):
        return _dot_nt(wt_ref[c * D_ATT:(c + 1) * D_ATT, :], h)

    def headnorm_t(zt, row):
        z3 = zt.reshape(N_HEADS, HEAD_DIM, tm)
        ssq = jnp.sum(z3 * z3, axis=1, keepdims=True)
        z3 = z3 * lax.rsqrt(ssq * (1.0 / HEAD_DIM) + RMS_EPS)
        return z3.reshape(D_ATT, tm) * gains_ref[row]

    ga_ref[...] = _dot(h, wn_ref[:, :D_ATT])
    gb_ref[...] = _dot(h, wn_ref[:, D_ATT:])
    qat_ref[...] = (headnorm_t(grp_t(0), 0) * scale).astype(BF16)
    kat = headnorm_t(grp_t(1), 1)
    ka32_ref[...] = kat
    katm_ref[...] = kat.T.astype(BF16)
    vat = grp_t(2)
    va32_ref[...] = vat
    vat_ref[...] = vat.astype(BF16)
    qbt_ref[...] = (headnorm_t(grp_t(3), 2) * scale).astype(BF16)
    kbt = headnorm_t(grp_t(4), 3)
    kb32_ref[...] = kbt
    kbtm_ref[...] = kbt.T.astype(BF16)
    vbt = grp_t(5)
    vb32_ref[...] = vbt
    vbt_ref[...] = vbt.astype(BF16)
    qit_ref[...] = grp_t(6).astype(BF16)
    kk = _dot_nt(wt_ref[7 * D_ATT:7 * D_ATT + LANES, :], h)
    ki32_ref[...] = kk[:D_IDX, :]
    kktm_ref[...] = kk.T.astype(BF16)
    wit_ref[...] = _dot_nt(wt_ref[7 * D_ATT + LANES:, :], h)[:H_IDX, :]


def _project_prompt(x, g, wn, wt, gains_t, batch, seq):
    tm = TQ
    n_t = seq // tm
    n = batch * seq
    row = lambda c: pl.BlockSpec((tm, c), lambda b, t: (b * n_t + t, 0))
    full = lambda a: pl.BlockSpec(a.shape, lambda b, t: (0,) * a.ndim)
    tiles = lambda r: pl.BlockSpec((None, None, r, tm), lambda b, t: (b, t, 0, 0))
    cols = lambda r: pl.BlockSpec((None, r, tm), lambda b, t: (b, 0, t))
    sd = jax.ShapeDtypeStruct
    tsd = lambda r, dt: sd((batch, n_t, r, tm), dt)
    out_shape = ([sd((n, D_ATT), F32)] * 2 + [sd((n, D_ATT), BF16)] * 2 + [sd((n, LANES), BF16)]
                 + [tsd(D_ATT, BF16)] * 3 + [tsd(H_IDX, F32)] + [tsd(D_ATT, BF16)] * 2
                 + [sd((batch, D_ATT, seq), F32)] * 4 + [sd((batch, D_IDX, seq), F32)])
    out_specs = ([row(D_ATT)] * 4 + [row(LANES)] + [tiles(D_ATT)] * 3 + [tiles(H_IDX)] + [tiles(D_ATT)] * 2
                 + [cols(D_ATT)] * 4 + [cols(D_IDX)])
    return pl.pallas_call(
        _proj_prompt_kernel,
        grid=(batch, n_t),
        in_specs=[row(D_MODEL), full(g), full(wn), full(wt), full(gains_t)],
        out_specs=out_specs, out_shape=out_shape,
        compiler_params=_params(2), name="proj_prompt",
    )(x, g, wn, wt, gains_t)


def _headnorm_rows(z, ones_ref, gain):
    ssq = _dot((z * z).astype(BF16), ones_ref[...])
    return z * lax.rsqrt(ssq * (1.0 / HEAD_DIM) + RMS_EPS) * gain


def _proj_sample_kernel(x_ref, g_ref, w_ref, gains_ref, ones_ref,
                        qa_ref, ka_ref, va_ref, qb_ref, kb_ref, vb_ref, qi_ref, wi_ref,
                        ga_ref, gb_ref, ka32_ref, va32_ref, kb32_ref, vb32_ref, ki32_ref):
    h = _rms(x_ref[...], g_ref[...]).astype(BF16)
    scale = HEAD_DIM ** -0.5

    def grp(c):
        return _dot(h, w_ref[:, c * D_ATT:(c + 1) * D_ATT])

    qa_ref[...] = (_headnorm_rows(grp(0), ones_ref, gains_ref[0:1, :]) * scale).astype(BF16)
    ka = _headnorm_rows(grp(1), ones_ref, gains_ref[1:2, :])
    ka32_ref[...] = ka
    ka_ref[...] = ka.astype(BF16)
    va = grp(2)
    va32_ref[...] = va
    va_ref[...] = va.astype(BF16)
    ga_ref[...] = grp(3)
    qb_ref[...] = (_headnorm_rows(grp(4), ones_ref, gains_ref[2:3, :]) * scale).astype(BF16)
    kb = _headnorm_rows(grp(5), ones_ref, gains_ref[3:4, :])
    kb32_ref[...] = kb
    kb_ref[...] = kb.astype(BF16)
    vb = grp(6)
    vb32_ref[...] = vb
    vb_ref[...] = vb.astype(BF16)
    gb_ref[...] = grp(7)
    qi_ref[...] = grp(8).astype(BF16)
    tail = _dot(h, w_ref[:, 9 * D_ATT:])
    ki32_ref[...] = tail[:, :D_IDX]
    wi_ref[...] = tail[:, LANES:]


def _project_sample(x, g, w, gains, ones_bd):
    n = x.shape[0]
    full = lambda a: pl.BlockSpec(a.shape, lambda i: (0,) * a.ndim)
    sd = lambda c, dt: jax.ShapeDtypeStruct((n, c), dt)
    out_shape = ([sd(D_ATT, BF16)] * 7 + [sd(LANES, F32)] + [sd(D_ATT, F32)] * 6 + [sd(D_IDX, F32)])
    return pl.pallas_call(
        _proj_sample_kernel,
        grid=(1,),
        in_specs=[full(x), full(g), full(w), full(gains), full(ones_bd)],
        out_specs=[full(s) for s in out_shape], out_shape=out_shape,
        compiler_params=_params(1), name="proj_sample",
    )(x, g, w, gains, ones_bd)


def _head_row_masks():
    r = lax.broadcasted_iota(I32, (LANES, 1), 0)
    return r < HEAD_DIM, r >= HEAD_DIM


def _pair_attention(i, qms, k_tile, v_tile, near_bias, mask_own, mask_past):
    def update(tiles, carry):
        raw = lambda half, j, bias: _dot(k_tile(j), qms[half]) + (bias(half) if bias else 0.0)
        scores = [[mask(half, j, raw(half, j, bias)) for j, bias, mask in tiles] for half in range(2)]
        out = []
        for half in range(2):
            pvs = [lambda p, j=j, half=half: _dot(v_tile(j, half), p) for j, _, _ in tiles]
            out.append(_osm_update(scores[half], pvs, *carry[half], axis=0, exp=jnp.exp2))
        return tuple(out)

    init = (jnp.full((1, TQ), NEG_INF, F32), jnp.zeros((1, TQ), F32), jnp.zeros((HEAD_DIM, TQ), F32))
    tiles = [(i, lambda half: near_bias(half, 0), lambda half, j, s: mask_own(half, s))]
    for d in range(1, 2 + FIRST_FAR):
        exists = i >= d
        tiles.append((jnp.maximum(i - d, 0), (lambda half: near_bias(half, 1)) if d == 1 else None,
                      lambda half, j, s, exists=exists: mask_past(half, j, s, exists)))
    carry = update(tiles, (init, init))
    n_far = jnp.maximum(i - 1 - FIRST_FAR, 0)
    done = 0
    for width in (4, 2, 1):
        n_w = (n_far - done) // width
        carry = lax.fori_loop(
            0, n_w,
            lambda u, c, done=done, width=width: update(
                [(done + width * u + w, None, lambda half, j, s: mask_past(half, j, s, True))
                 for w in range(width)], c),
            carry)
        done = done + n_w * width
    return [c[2] / c[1] for c in carry]


def _moba_prompt_kernel(qt_ref, k_ref, vt_ref, bias_ref, o_ref, kmean_ref, *, n_blk):
    i = pl.program_id(2)
    n_sub = kmean_ref.shape[0]

    @pl.when(i == 0)
    def _():
        kmean_ref[...] = jnp.zeros_like(kmean_ref)
        for j in range(n_blk):
            kb = k_ref[j * TQ:(j + 1) * TQ, :].astype(F32)
            kmean_ref[j:j + 1, :] = jnp.mean(kb, axis=0, keepdims=True)

    masks = _head_row_masks()
    qt = qt_ref[...]
    key = lax.broadcasted_iota(I32, (TQ, TQ), 0)
    qry = lax.broadcasted_iota(I32, (TQ, TQ), 1)
    blk = lax.broadcasted_iota(I32, (n_sub, TQ), 0)
    kmean = kmean_ref[...].astype(BF16)
    qms = [jnp.where(masks[half], qt, jnp.zeros_like(qt)) for half in range(2)]
    picks = [_top_blocks(jnp.where(blk < i, _dot(kmean, qm), NEG_INF), blk, axis=0) for qm in qms]

    def picked(half, j, s, exists):
        sel = (picks[half][0] == j) | (picks[half][1] == j) | (picks[half][2] == j)
        return jnp.where(sel if exists is True else sel & exists, s, NEG_INF)

    outs = _pair_attention(
        i, qms,
        k_tile=lambda j: k_ref[pl.ds(pl.multiple_of(j * TQ, TQ), TQ), :],
        v_tile=lambda j, half: vt_ref[j, half * HEAD_DIM:(half + 1) * HEAD_DIM, :],
        near_bias=lambda half, d: bias_ref[half, d],
        mask_own=lambda half, s: jnp.where(key <= qry, s, NEG_INF),
        mask_past=picked)
    o_ref[...] = jnp.concatenate(outs, axis=0).T


def _moba_prompt(qt, k, vt, bias, batch, seq):
    n_q = seq // TQ
    n_sub = -(-n_q // SUBLANES) * SUBLANES
    kern = functools.partial(_moba_prompt_kernel, n_blk=n_q)
    return pl.pallas_call(
        kern,
        grid=(batch, N_HEADS // 2, n_q),
        in_specs=[
            pl.BlockSpec((None, None, LANES, TQ), lambda b, hp, i: (b, i, hp, 0)),
            pl.BlockSpec((seq, LANES), lambda b, hp, i: (b, hp)),
            pl.BlockSpec((None, n_q, LANES, TQ), lambda b, hp, i: (b, 0, hp, 0)),
            pl.BlockSpec((2, 2, TQ, TQ), lambda b, hp, i: (hp, 0, 0, 0)),
        ],
        out_specs=pl.BlockSpec((TQ, LANES), lambda b, hp, i: (b * n_q + i, hp)),
        out_shape=jax.ShapeDtypeStruct((batch * seq, D_ATT), F32),
        scratch_shapes=[pltpu.VMEM((n_sub, LANES), F32)],
        compiler_params=_params(3), name="moba_prompt",
    )(qt, k, vt, bias)


def _dsa_prompt_kernel(qt_ref, qit_ref, wit_ref, kk_ref, k_ref, vt_ref, bias_ref, o_ref,
                       keys_ref, pen_ref, *, seq, k_top):
    i = pl.program_id(1)
    masks = _head_row_masks()
    key = lax.broadcasted_iota(I32, (TQ, TQ), 0)
    qry = lax.broadcasted_iota(I32, (TQ, TQ), 1)
    zero_b = jnp.zeros((LANES, TQ), BF16)

    def head_qt(ref, h):
        pair, half = divmod(h, 2)
        return jnp.where(masks[half], ref[pair * LANES:(pair + 1) * LANES, :], zero_b)

    def index_tile(j):
        kkj = kk_ref[pl.ds(pl.multiple_of(j * TQ, TQ), TQ), :]
        acc = jnp.zeros((TQ, TQ), F32)
        for h in range(H_IDX):
            acc = acc + wit_ref[h:h + 1, :] * jnp.maximum(_dot(kkj, head_qt(qit_ref, h)), 0.0)
        return _f32_key(acc)

    def index_pair(u, c):
        keys_ref[2 * u] = index_tile(2 * u)
        keys_ref[2 * u + 1] = index_tile(2 * u + 1)
        return c

    def index_one(j, c):
        keys_ref[j] = index_tile(j)
        return c

    lax.fori_loop(0, i // 2, index_pair, 0)
    lax.fori_loop(2 * (i // 2), i, index_one, 0)
    keys_ref[i] = jnp.where(key <= qry, index_tile(i), INT_MIN)

    def count_fn(pred):
        def one(j):
            w = jnp.where(pred(keys_ref[j], key + j * TQ), 1.0, 0.0)
            return jnp.sum(w.reshape(TQ // SUBLANES, SUBLANES, TQ), axis=0)
        n_two = (i + 1) // 2
        part = lax.fori_loop(0, n_two, lambda u, p: p + one(2 * u) + one(2 * u + 1),
                             jnp.zeros((SUBLANES, TQ), F32))
        part = lax.fori_loop(2 * n_two, i + 1, lambda j, p: p + one(j), part)
        return jnp.sum(part, axis=0, keepdims=True)

    t, x = _select_threshold(count_fn, k_top, seq)

    def pen_body(j, c):
        kt = keys_ref[j]
        sel = (kt > t) | ((kt == t) & (key + j * TQ <= x))
        pen_ref[j] = jnp.where(sel, 0.0, NEG_INF)
        return c

    lax.fori_loop(0, i + 1, pen_body, 0)
    outs = []
    for pair in range(N_HEADS // 2):
        lanes = slice(pair * LANES, (pair + 1) * LANES)
        vrows = lambda half, pair=pair: slice((2 * pair + half) * HEAD_DIM, (2 * pair + half + 1) * HEAD_DIM)
        outs += _pair_attention(
            i, [head_qt(qt_ref, 2 * pair + half) for half in range(2)],
            k_tile=lambda j, lanes=lanes: k_ref[pl.ds(pl.multiple_of(j * TQ, TQ), TQ), lanes],
            v_tile=lambda j, half, vrows=vrows: vt_ref[j, vrows(half), :],
            near_bias=lambda half, d, pair=pair: bias_ref[2 * pair + half, d],
            mask_own=lambda half, s: s + pen_ref[i],
            mask_past=lambda half, j, s, exists: (
                s + pen_ref[j] if exists is True else jnp.where(exists, s + pen_ref[j], NEG_INF)))
    o_ref[...] = jnp.concatenate(outs, axis=0).T


def _dsa_prompt(qt, qit, wit, kk, k, vt, bias, batch, seq):
    n_q = seq // TQ
    k_top = min(DSA_TOPK_MAX, seq // 4)
    kern = functools.partial(_dsa_prompt_kernel, seq=seq, k_top=k_top)
    tile = lambda r: pl.BlockSpec((None, None, r, TQ), lambda b, i: (b, i, 0, 0))
    rows = lambda c: pl.BlockSpec((seq, c), lambda b, i: (b, 0))
    return pl.pallas_call(
        kern,
        grid=(batch, n_q),
        in_specs=[
            tile(D_ATT), tile(D_ATT), tile(H_IDX),
            rows(LANES), rows(D_ATT),
            pl.BlockSpec((None, n_q, D_ATT, TQ), lambda b, i: (b, 0, 0, 0)),
            pl.BlockSpec((N_HEADS, 2, TQ, TQ), lambda b, i: (1, 0, 0, 0)),
        ],
        out_specs=pl.BlockSpec((TQ, D_ATT), lambda b, i: (b * n_q + i, 0)),
        out_shape=jax.ShapeDtypeStruct((batch * seq, D_ATT), F32),
        scratch_shapes=[pltpu.VMEM((n_q, TQ, TQ), I32), pltpu.VMEM((n_q, TQ, TQ), F32)],
        compiler_params=_params(2), name="dsa_prompt",
    )(qt, qit, wit, kk, k, vt, bias)


def _expand_tokens(a, n_tok):
    return jnp.broadcast_to(a[:, None, :], (n_tok, N_HEADS, a.shape[-1])).reshape(n_tok * N_HEADS, a.shape[-1])


def _collapse_heads(o, hmask, n_tok):
    return jnp.sum((o * hmask).reshape(n_tok, N_HEADS, D_ATT), axis=1)


def _sample_index_kernel(pt_ref, *refs, n_pages, n_tok, k_top, n_pp):
    kit_refs = refs[:n_pp]
    qi_ref, wcol_ref, knew_ref, pen_ref, keys_ref = refs[n_pp:]
    g = pl.program_id(1)

    def keys_of(kt):
        x = jnp.maximum(_dot(qi_ref[...], kt), 0.0) * wcol_ref[...]
        return _f32_key(jnp.sum(x.reshape(n_tok, H_IDX, kt.shape[1]), axis=1))

    keys = keys_of(jnp.concatenate([r[...] for r in kit_refs], axis=1).astype(BF16))
    for o in range(n_pp):
        keys_ref[g * n_pp + o] = keys[:, o * LANES:(o + 1) * LANES]

    @pl.when(g == n_pages // n_pp - 1)
    def _():
        lane = lax.broadcasted_iota(I32, (n_tok, LANES), 1)
        tok = lax.broadcasted_iota(I32, (n_tok, LANES), 0)
        keys_ref[n_pages] = jnp.where(lane <= tok, keys_of(knew_ref[...]), INT_MIN)
        shape = (n_pages + 1, n_tok, LANES)
        pos = lax.broadcasted_iota(I32, shape, 0) * LANES + lax.broadcasted_iota(I32, shape, 2)
        all_keys = keys_ref[...]

        def count_fn(pred):
            w = jnp.where(pred(all_keys, pos), 1.0, 0.0)
            terms = [w[j] for j in range(n_pages + 1)]
            while len(terms) > 1:
                terms = [a + b for a, b in zip(terms[::2], terms[1::2])] + terms[len(terms) & ~1:]
            return jnp.sum(terms[0], axis=1, keepdims=True)

        t, x = _select_threshold(count_fn, k_top, (n_pages + 1) * LANES)
        sel = (all_keys > t) | ((all_keys == t) & (pos <= x))
        pen_ref[...] = jnp.where(sel, 0.0, NEG_INF)


def _sample_index(page_table, kit, qi, wcol, knew, n_tok, k_top):
    n_seq, n_pages = page_table.shape
    n_pp = math.gcd(IDX_PAGES_PER_STEP, n_pages)
    kern = functools.partial(_sample_index_kernel, n_pages=n_pages, n_tok=n_tok, k_top=k_top, n_pp=n_pp)
    per_seq = lambda a: pl.BlockSpec((None,) + a.shape[1:], lambda b, g, pt: (b,) + (0,) * (a.ndim - 1))
    page = lambda o: pl.BlockSpec((None, D_IDX, PAGE_SIZE), lambda b, g, pt: (pt[b, g * n_pp + o], 0, 0))
    grid_spec = pltpu.PrefetchScalarGridSpec(
        num_scalar_prefetch=1, grid=(n_seq, n_pages // n_pp),
        in_specs=[page(o) for o in range(n_pp)] + [per_seq(qi), per_seq(wcol), per_seq(knew)],
        out_specs=pl.BlockSpec((None, n_pages + 1, n_tok, LANES), lambda b, g, pt: (b, 0, 0, 0)),
        scratch_shapes=[pltpu.VMEM((n_pages + 1, n_tok, LANES), I32)])
    return pl.pallas_call(
        kern, grid_spec=grid_spec,
        out_shape=jax.ShapeDtypeStruct((n_seq, n_pages + 1, n_tok, LANES), F32),
        compiler_params=_params(2), name="sample_index",
    )(page_table, *([kit] * n_pp), qi, wcol, knew)


def _sample_dsa_kernel(pt_ref, *refs, n_pages, n_tok, n_pp):
    kt_refs, vt_refs = refs[:n_pp], refs[n_pp:2 * n_pp]
    (q_ref, pen_ref, blast_ref, c31_ref, knew_ref, vnew_ref, bnew_ref, hmask_ref,
     o_ref, m_ref, l_ref, acc_ref) = refs[2 * n_pp:]
    g = pl.program_id(1)
    last = g == n_pages // n_pp - 1

    @pl.when(g == 0)
    def _():
        m_ref[...] = jnp.full(m_ref.shape, NEG_INF, F32)
        l_ref[...] = jnp.zeros(l_ref.shape, F32)
        acc_ref[...] = jnp.zeros(acc_ref.shape, F32)

    q = q_ref[...]
    scores, pvs = [], []
    for o in range(n_pp):
        bias = c31_ref[...] if o < n_pp - 1 else jnp.where(last, blast_ref[...], c31_ref[...])
        scores.append(_dot(q, kt_refs[o][...].astype(BF16)) + bias + _expand_tokens(pen_ref[g * n_pp + o], n_tok))
        pvs.append(lambda pp, o=o: _dot_nt(pp, vt_refs[o][...].astype(BF16)))
    m, l, acc = _osm_update(scores, pvs, m_ref[...], l_ref[...], acc_ref[...], axis=1)
    m_ref[...] = m
    l_ref[...] = l
    acc_ref[...] = acc

    @pl.when(last)
    def _():
        pen_new = _expand_tokens(pen_ref[n_pages][:, :NEW_PAD], n_tok)
        s_n = _dot_nt(q, knew_ref[...]) + bnew_ref[...] + pen_new
        _, l2, acc2 = _osm_update([s_n], [lambda pp: _dot(pp, vnew_ref[...])], m, l, acc, axis=1)
        o_ref[...] = _collapse_heads(acc2 / l2, hmask_ref[...], n_tok)


def _sample_dsa(page_table, kt, vt, q, pen, blast, c31col, knew, vnew, bnew, hmask, n_tok):
    n_seq, n_pages = page_table.shape
    n_pp = math.gcd(PAGES_PER_STEP, n_pages)
    rows = n_tok * N_HEADS
    kern = functools.partial(_sample_dsa_kernel, n_pages=n_pages, n_tok=n_tok, n_pp=n_pp)
    per_seq = lambda a: pl.BlockSpec((None,) + a.shape[1:], lambda b, g, pt: (b,) + (0,) * (a.ndim - 1))
    full = lambda a: pl.BlockSpec(a.shape, lambda b, g, pt: (0,) * a.ndim)
    page = lambda o: pl.BlockSpec((None, D_ATT, PAGE_SIZE), lambda b, g, pt: (pt[b, g * n_pp + o], 0, 0))
    pages = [page(o) for o in range(n_pp)]
    grid_spec = pltpu.PrefetchScalarGridSpec(
        num_scalar_prefetch=1, grid=(n_seq, n_pages // n_pp),
        in_specs=pages + pages + [per_seq(q), per_seq(pen), full(blast), full(c31col),
                                  per_seq(knew), per_seq(vnew), full(bnew), full(hmask)],
        out_specs=pl.BlockSpec((None, n_tok, D_ATT), lambda b, g, pt: (b, 0, 0)),
        scratch_shapes=[pltpu.VMEM((rows, 1), F32), pltpu.VMEM((rows, 1), F32), pltpu.VMEM((rows, D_ATT), F32)])
    return pl.pallas_call(
        kern, grid_spec=grid_spec,
        out_shape=jax.ShapeDtypeStruct((n_seq, n_tok, D_ATT), F32),
        compiler_params=_params(2), name="sample_dsa",
    )(page_table, *([kt] * n_pp), *([vt] * n_pp), q, pen, blast, c31col, knew, vnew, bnew, hmask)


def _sample_moba_kernel(pt_ref, *refs, n_blk, n_tok, n_pp):
    kt_refs, vt_refs = refs[:n_pp], refs[n_pp:2 * n_pp]
    (q_ref, blast_ref, c31_ref, knew_ref, vnew_ref, bnew_ref, hmask_ref,
     o_ref, gate_ref, ms_ref, ls_ref, accs_ref) = refs[2 * n_pp:]
    g = pl.program_id(1)
    n_bb = n_pp // 2
    last = g == n_blk // n_bb - 1
    rows = n_tok * N_HEADS
    lane = lax.broadcasted_iota(I32, (rows, LANES), 1)

    @pl.when(g == 0)
    def _():
        gate_ref[...] = jnp.full(gate_ref.shape, NEG_INF, F32)
        ms_ref[...] = jnp.full(ms_ref.shape, NEG_INF, F32)
        ls_ref[...] = jnp.zeros(ls_ref.shape, F32)

    q = q_ref[...]
    gates, ms, ls = gate_ref[...], ms_ref[...], ls_ref[...]
    for bb in range(n_bb):
        j = g * n_bb + bb
        r0 = _dot(q, kt_refs[2 * bb][...].astype(BF16))
        r1 = _dot(q, kt_refs[2 * bb + 1][...].astype(BF16))
        gate = jnp.sum(r0 + r1, axis=1, keepdims=True)
        s0 = r0 + c31_ref[...]
        s1 = r1 + (c31_ref[...] if bb < n_bb - 1 else jnp.where(last, blast_ref[...], c31_ref[...]))
        m = jnp.maximum(jnp.max(s0, axis=1, keepdims=True), jnp.max(s1, axis=1, keepdims=True))
        p0 = jnp.exp(s0 - m)
        p1 = jnp.exp(s1 - m)
        l = jnp.sum(p0, axis=1, keepdims=True) + jnp.sum(p1, axis=1, keepdims=True)
        accs_ref[j] = (_dot_nt(p0.astype(BF16), vt_refs[2 * bb][...].astype(BF16))
                       + _dot_nt(p1.astype(BF16), vt_refs[2 * bb + 1][...].astype(BF16)))
        gates = jnp.where(lane == j, gate, gates)
        ms = jnp.where(lane == j, m, ms)
        ls = jnp.where(lane == j, l, ls)
    gate_ref[...] = gates
    ms_ref[...] = ms
    ls_ref[...] = ls

    @pl.when(last)
    def _():
        picks = _top_blocks(jnp.where(lane < n_blk, gates, NEG_INF), lane, axis=1)
        sel = (lane == picks[0]) | (lane == picks[1]) | (lane == picks[2])
        s_n = _dot_nt(q, knew_ref[...]) + bnew_ref[...]
        m_o = jnp.max(s_n, axis=1, keepdims=True)
        p_o = jnp.exp(s_n - m_o)
        l_o = jnp.sum(p_o, axis=1, keepdims=True)
        m_all = jnp.maximum(m_o, jnp.max(jnp.where(sel, ms, NEG_INF), axis=1, keepdims=True))
        w = jnp.where(sel, jnp.exp(ms - m_all), 0.0)
        w_o = jnp.exp(m_o - m_all)
        l_all = jnp.sum(w * ls, axis=1, keepdims=True) + w_o * l_o
        out = w_o * _dot(p_o.astype(BF16), vnew_ref[...])
        for jj in range(n_blk):
            out = out + w[:, jj:jj + 1] * accs_ref[jj]
        o_ref[...] = _collapse_heads(out / l_all, hmask_ref[...], n_tok)


def _sample_moba(page_table, kt, vt, q, blast, c31col, knew, vnew, bnew, hmask, n_tok):
    n_seq, n_pages = page_table.shape
    pages_per_blk = MOBA_BLOCK // PAGE_SIZE
    n_blk = n_pages // pages_per_blk
    n_pp = pages_per_blk * math.gcd(PAGES_PER_STEP // pages_per_blk, n_blk)
    rows = n_tok * N_HEADS
    kern = functools.partial(_sample_moba_kernel, n_blk=n_blk, n_tok=n_tok, n_pp=n_pp)
    per_seq = lambda a: pl.BlockSpec((None,) + a.shape[1:], lambda b, g, pt: (b,) + (0,) * (a.ndim - 1))
    full = lambda a: pl.BlockSpec(a.shape, lambda b, g, pt: (0,) * a.ndim)
    page = lambda o: pl.BlockSpec((None, D_ATT, PAGE_SIZE), lambda b, g, pt: (pt[b, g * n_pp + o], 0, 0))
    pages = [page(o) for o in range(n_pp)]
    grid_spec = pltpu.PrefetchScalarGridSpec(
        num_scalar_prefetch=1, grid=(n_seq, n_pages // n_pp),
        in_specs=pages + pages + [per_seq(q), full(blast), full(c31col),
                                  per_seq(knew), per_seq(vnew), full(bnew), full(hmask)],
        out_specs=pl.BlockSpec((None, n_tok, D_ATT), lambda b, g, pt: (b, 0, 0)),
        scratch_shapes=[pltpu.VMEM((rows, LANES), F32)] * 3 + [pltpu.VMEM((n_blk, rows, D_ATT), F32)])
    return pl.pallas_call(
        kern, grid_spec=grid_spec,
        out_shape=jax.ShapeDtypeStruct((n_seq, n_tok, D_ATT), F32),
        compiler_params=_params(2), name="sample_moba",
    )(page_table, *([kt] * n_pp), *([vt] * n_pp), q, blast, c31col, knew, vnew, bnew, hmask)


def _merge_and_conv_in(x_ref, oa_ref, ob_ref, ga_ref, gb_ref, wo_ref, g1_ref, wc_ref):
    o = jnp.concatenate([oa_ref[...] * _silu(ga_ref[...]), ob_ref[...] * _silu(gb_ref[...])], axis=1)
    h1 = x_ref[...] + _dot(o.astype(BF16), wo_ref[...])
    hn = _rms(h1, g1_ref[...]).astype(BF16)
    a = _dot(hn, wc_ref[:, 0:D_MODEL])
    b = _dot(hn, wc_ref[:, D_MODEL:2 * D_MODEL])
    g = _dot(hn, wc_ref[:, 2 * D_MODEL:3 * D_MODEL])
    return h1, a * _sigmoid(b), g


def _ln_gate_out(c, g, h1, lg_ref, lb_ref, wo2_ref):
    mu = jnp.mean(c, axis=-1, keepdims=True)
    d = c - mu
    var = jnp.mean(d * d, axis=-1, keepdims=True)
    cn = d * lax.rsqrt(var + LN_EPS) * lg_ref[...] + lb_ref[...]
    z = (_silu(cn) * _silu(g)).astype(BF16)
    return h1 + _dot(z, wo2_ref[...])


def _layer1_prompt_kernel(x_ref, oa_ref, ob_ref, ga_ref, gb_ref, wo_ref, g1_ref, wc_ref,
                          cw_ref, cb_ref, lg_ref, lb_ref, wo2_ref, y_ref, st_ref, up_ref, c_ref, *, tt):
    t = pl.program_id(1)

    @pl.when(t == 0)
    def _():
        up_ref[0:HIST, :] = jnp.zeros((HIST, D_MODEL), F32)

    h1, u, g = _merge_and_conv_in(x_ref, oa_ref, ob_ref, ga_ref, gb_ref, wo_ref, g1_ref, wc_ref)
    up_ref[HIST:HIST + tt, :] = u
    base = HIST - (CONV_WIDTH - 1)
    rows = min(tt, CONV_ROWS)
    for cc in range(0, D_MODEL, LANES):
        cs = slice(cc, cc + LANES)
        for r0 in range(0, tt, rows):
            acc = jnp.zeros((rows, LANES), F32) + cb_ref[:, cs]
            for phase in range(SUBLANES):
                taps = [w for w in range(CONV_WIDTH) if (base + w) % SUBLANES == phase]
                n_win = rows + (SUBLANES if phase else 0)
                group = None
                for w in taps:
                    start = r0 + base + w - phase
                    term = cw_ref[w:w + 1, cs] * up_ref[start:start + n_win, cs]
                    group = term if group is None else group + term
                if group is not None:
                    acc = acc + group[phase:phase + rows, :]
            c_ref[r0:r0 + rows, cs] = acc
    y_ref[...] = _ln_gate_out(c_ref[...], g, h1, lg_ref, lb_ref, wo2_ref)
    tail = up_ref[tt:tt + HIST, :]
    st_ref[...] = tail
    up_ref[0:HIST, :] = tail


def _layer1_prompt(x, oa, ob, ga, gb, wo, g1, wc, cw, cb, lg, lb, wo2, batch, seq, tt):
    n_t = seq // tt
    kern = functools.partial(_layer1_prompt_kernel, tt=tt)
    row = lambda c: pl.BlockSpec((tt, c), lambda b, t: (b * n_t + t, 0))
    full = lambda a: pl.BlockSpec(a.shape, lambda b, t: (0,) * a.ndim)
    return pl.pallas_call(
        kern,
        grid=(batch, n_t),
        in_specs=[row(D_MODEL), row(D_ATT), row(D_ATT), row(D_ATT), row(D_ATT),
                  full(wo), full(g1), full(wc), full(cw), full(cb), full(lg), full(lb), full(wo2)],
        out_specs=[row(D_MODEL), pl.BlockSpec((None, HIST, D_MODEL), lambda b, t: (b, 0, 0))],
        out_shape=[jax.ShapeDtypeStruct((batch * seq, D_MODEL), F32),
                   jax.ShapeDtypeStruct((batch, HIST, D_MODEL), F32)],
        scratch_shapes=[pltpu.VMEM((HIST + tt, D_MODEL), F32), pltpu.VMEM((tt, D_MODEL), F32)],
        compiler_params=_params(2), name="layer1_prompt",
    )(x, oa, ob, ga, gb, wo, g1, wc, cw, cb, lg, lb, wo2)


def _layer1_sample_kernel(x_ref, oa_ref, ob_ref, ga_ref, gb_ref, wo_ref, g1_ref, wc_ref,
                          cw_ref, cb_ref, lg_ref, lb_ref, wo2_ref, st_ref, y_ref, ns_ref, c_ref, *, n_tok, n_seq):
    n_hist = CONV_WIDTH - 1
    h1, u, g = _merge_and_conv_in(x_ref, oa_ref, ob_ref, ga_ref, gb_ref, wo_ref, g1_ref, wc_ref)
    u_t = [u[t * n_seq:(t + 1) * n_seq, :] for t in range(n_tok)]
    for t in range(n_tok):
        acc = jnp.zeros((n_seq, D_MODEL), F32) + cb_ref[...]
        for r in range(t, n_hist):
            acc = acc + cw_ref[r - t:r - t + 1, :] * st_ref[r]
        for r in range(t + 1):
            acc = acc + cw_ref[n_hist - t + r:n_hist - t + r + 1, :] * u_t[r]
        c_ref[t * n_seq:(t + 1) * n_seq, :] = acc
    y_ref[...] = _ln_gate_out(c_ref[...], g, h1, lg_ref, lb_ref, wo2_ref)
    for r in range(n_hist - n_tok):
        ns_ref[r] = st_ref[r + n_tok]
    for t in range(n_tok):
        ns_ref[n_hist - n_tok + t] = u_t[t]


def _layer1_sample(x, oa, ob, ga, gb, wo, g1, wc, cw, cb, lg, lb, wo2, st, n_tok, n_seq):
    kern = functools.partial(_layer1_sample_kernel, n_tok=n_tok, n_seq=n_seq)
    full = lambda a: pl.BlockSpec(a.shape, lambda i: (0,) * a.ndim)
    args = (x, oa, ob, ga, gb, wo, g1, wc, cw, cb, lg, lb, wo2, st)
    out_shape = [jax.ShapeDtypeStruct(x.shape, F32), jax.ShapeDtypeStruct(st.shape, F32)]
    return pl.pallas_call(
        kern, grid=(1,),
        in_specs=[full(a) for a in args],
        out_specs=[full(s) for s in out_shape], out_shape=out_shape,
        scratch_shapes=[pltpu.VMEM(x.shape, F32)],
        compiler_params=_params(1), name="layer1_sample",
    )(*args)


def _t5_bucket(n):
    exact = N_BUCKETS // 2
    nf = jnp.maximum(n, 1).astype(F32)
    large = exact + (jnp.log(nf / exact) / math.log(MAX_DISTANCE / exact) * (N_BUCKETS - exact)).astype(I32)
    large = jnp.minimum(large, N_BUCKETS - 1)
    return jnp.where(n < exact, n, large)


def _bias_by_distance(table, n):
    return table[_t5_bucket(jnp.arange(n, dtype=I32))]


def _prep_attn_weights(w_in, qn_a, kn_a, qn_b, kn_b):
    grp = lambda c: w_in[:, c * D_ATT:(c + 1) * D_ATT]
    ki = w_in[:, 9 * D_ATT:9 * D_ATT + D_IDX]
    wi = w_in[:, 9 * D_ATT + D_IDX:]
    zeros = lambda c: jnp.zeros((D_MODEL, c), w_in.dtype)
    tile = lambda g: jnp.tile(g, N_HEADS)
    head = np.arange(D_ATT) // HEAD_DIM
    ones_bd = jnp.asarray(head[:, None] == head[None, :], dtype=BF16)
    wn = jnp.concatenate([grp(3), grp(7)], axis=1).astype(BF16)
    wt = jnp.concatenate([grp(0), grp(1), grp(2), grp(4), grp(5), grp(6), grp(8), ki, ki, wi,
                          zeros(NEW_PAD - H_IDX)], axis=1).T.astype(BF16)
    gains_t = jnp.stack([tile(qn_a), tile(kn_a), tile(qn_b), tile(kn_b)]).astype(F32)[:, :, None]
    ws = jnp.concatenate([w_in[:, :9 * D_ATT + D_IDX], zeros(LANES - D_IDX), wi, zeros(LANES - H_IDX)],
                         axis=1).astype(BF16)
    gains_s = jnp.stack([tile(qn_a), tile(kn_a), tile(qn_b), tile(kn_b)]).astype(F32)
    return wn, wt, gains_t, ws, gains_s, ones_bd


def kernel(x_prompt, x_sample, cache_k_a, cache_v_a, cache_k_b, cache_v_b, cache_kidx_b, state_conv, page_table,
           norm_g, rel_bias_table, w_in_attn, q_norm_a, k_norm_a, q_norm_b, k_norm_b, w_out_attn,
           w_in_conv, conv_w, conv_b, conv_ln_g, conv_ln_b, w_out_conv):
    batch, seq, _ = x_prompt.shape
    n_seq, n_tok, _ = x_sample.shape
    n_pages = page_table.shape[1]
    past = n_pages * PAGE_SIZE
    assert seq % TQ == 0 and past % MOBA_BLOCK == 0 and n_tok <= NEW_PAD and TQ >= MAX_DISTANCE
    n_hist = CONV_WIDTH - 1
    wn, wt, gains_t, ws, gains_s, ones_bd = _prep_attn_weights(
        w_in_attn[0], q_norm_a[0], k_norm_a[0], q_norm_b[0], k_norm_b[0])
    g0 = norm_g[0][None, :]
    g1 = norm_g[1][None, :]
    wo = w_out_attn[0].astype(BF16)
    wc = w_in_conv[0].astype(BF16)
    wo2 = w_out_conv[0].astype(BF16)
    cw = jnp.concatenate([conv_w[0], jnp.zeros((HIST - CONV_WIDTH, D_MODEL), F32)], axis=0)
    conv_rows = (cw, conv_b[0][None, :], conv_ln_g[0][None, :], conv_ln_b[0][None, :])
    c31 = rel_bias_table[N_BUCKETS - 1]
    bias_tiles = _bias_tiles(rel_bias_table)

    xp = x_prompt.reshape(batch * seq, D_MODEL)
    (ga, gb, katm, kbtm, kktm, qat, qbt, qit, wit, vat, vbt, ka32, va32, kb32, vb32, ki32) = _project_prompt(
        xp, g0, wn, wt, gains_t, batch, seq)
    oa = _moba_prompt(qat, katm, vat, bias_tiles, batch, seq)
    ob = _dsa_prompt(qbt, qit, wit, kktm, kbtm, vbt, bias_tiles, batch, seq)
    y_p, st_p = _layer1_prompt(xp, oa, ob, ga, gb, wo, g1, wc, *conv_rows, wo2, batch, seq, tt=TQ)
    heads_p = lambda a: jnp.transpose(a.reshape(batch, N_HEADS, HEAD_DIM, seq), (0, 3, 1, 2))[None]
    ki_p = jnp.transpose(ki32, (0, 2, 1))[None]
    cv_p = st_p[None, :, HIST - n_hist:, :]

    rows = n_tok * N_HEADS
    xs = x_sample.reshape(n_seq * n_tok, D_MODEL)
    (qa_s, ka_s, va_s, qb_s, kb_s, vb_s, qi_s, wi_s, ga_s, gb_s, ka32_s, va32_s, kb32_s, vb32_s, ki32_s) = \
        _project_sample(xs, g0, ws, gains_s, ones_bd)
    head_of_lane = np.arange(D_ATT) // HEAD_DIM
    hmask8 = (np.arange(N_HEADS)[:, None] == head_of_lane[None, :])
    hmask = jnp.asarray(np.tile(hmask8, (n_tok, 1)), F32)
    qbd = lambda q: (q.reshape(n_seq, n_tok, 1, D_ATT) * jnp.asarray(hmask8, BF16)).reshape(n_seq, rows, D_ATT)
    newpad = lambda a: jnp.pad(a.reshape(n_seq, n_tok, D_ATT), ((0, 0), (0, NEW_PAD - n_tok), (0, 0)))
    bd = _bias_by_distance(rel_bias_table, 2 * PAGE_SIZE)
    t_of_row = np.arange(rows) // N_HEADS
    h_of_row = np.arange(rows) % N_HEADS
    c_idx = np.arange(PAGE_SIZE)
    dist_last = PAGE_SIZE + t_of_row[:, None] - c_idx[None, :]
    tn = np.arange(NEW_PAD)
    dist_new = np.maximum(t_of_row[:, None] - tn[None, :], 0)
    valid_new = (tn[None, :] <= t_of_row[:, None]) & (tn[None, :] < n_tok)

    def sample_bias(off):
        hh = h_of_row + off
        blast = bd[dist_last, hh[:, None]]
        bnew = jnp.where(valid_new, bd[dist_new, hh[:, None]], NEG_INF)
        return blast, c31[hh][:, None], bnew

    pages = lambda c: jnp.transpose(c[0], (0, 2, 3, 1)).reshape(c.shape[1], D_ATT, PAGE_SIZE)
    kit = jnp.transpose(cache_kidx_b[0], (0, 2, 1))
    blast_a, c31_a, bnew_a = sample_bias(0)
    oa_s = _sample_moba(page_table, pages(cache_k_a), pages(cache_v_a), qbd(qa_s), blast_a, c31_a,
                        newpad(ka_s), newpad(va_s), bnew_a, hmask, n_tok)
    k_top = min(DSA_TOPK_MAX, (past + n_tok) // 4)
    knew_i = jnp.pad(jnp.transpose(ki32_s.reshape(n_seq, n_tok, D_IDX), (0, 2, 1)),
                     ((0, 0), (0, 0), (0, PAGE_SIZE - n_tok))).astype(BF16)
    pen = _sample_index(page_table, kit, qi_s.reshape(n_seq, rows, D_IDX),
                        wi_s[:, :H_IDX].reshape(n_seq, rows, 1), knew_i, n_tok, k_top)
    blast_b, c31_b, bnew_b = sample_bias(N_HEADS)
    ob_s = _sample_dsa(page_table, pages(cache_k_b), pages(cache_v_b), qbd(qb_s), pen, blast_b, c31_b,
                       newpad(kb_s), newpad(vb_s), bnew_b, hmask, n_tok)
    tmaj = lambda a: jnp.transpose(a.reshape(n_seq, n_tok, -1), (1, 0, 2)).reshape(n_tok * n_seq, -1)
    st_in = jnp.transpose(state_conv[0], (1, 0, 2))
    y_s, ns = _layer1_sample(tmaj(xs), tmaj(oa_s), tmaj(ob_s), tmaj(ga_s), tmaj(gb_s), wo, g1, wc,
                             *conv_rows, wo2, st_in, n_tok, n_seq)
    y_s = jnp.transpose(y_s.reshape(n_tok, n_seq, D_MODEL), (1, 0, 2))
    cv_s = jnp.transpose(ns, (1, 0, 2))[None]
    heads_s = lambda a: a.reshape(1, n_seq, n_tok, N_HEADS, HEAD_DIM)

    return (y_p.reshape(batch, seq, D_MODEL), y_s,
            heads_p(ka32), heads_p(va32), heads_p(kb32), heads_p(vb32), ki_p, cv_p,
            heads_s(ka32_s), heads_s(va32_s), heads_s(kb32_s), heads_s(vb32_s),
            ki32_s.reshape(1, n_seq, n_tok, D_IDX), cv_s)
```

```python
import functools
import math

import numpy as np
import jax
import jax.numpy as jnp
from jax import lax
from jax.experimental import pallas as pl
from jax.experimental.pallas import tpu as pltpu

F32 = jnp.float32
BF16 = jnp.bfloat16
I32 = jnp.int32

D_MODEL = 1024
HEAD_DIM = 64
N_HEADS = 8
D_ATT = N_HEADS * HEAD_DIM
MOBA_BLOCK = 256
MOBA_TOPK = 3
DSA_TOPK_MAX = 256
H_IDX = 8
D_IDX = 64
CONV_WIDTH = 31
N_BUCKETS = 32
MAX_DISTANCE = 128
PAGE_SIZE = 128
RMS_EPS = 1e-6
LN_EPS = 1e-5

LANES = 128
SUBLANES = 8
TQ = MOBA_BLOCK
HIST = 32
NEW_PAD = 16
PAGES_PER_STEP = 16
IDX_PAGES_PER_STEP = 16
CONV_ROWS = 128
FIRST_FAR = 2
INT_MIN = -2 ** 31
NEG_INF = float("-inf")
LOG2E = math.log2(math.e)
VMEM_LIMIT = 56 * 1024 * 1024


def _dot(a, b):
    return jnp.dot(a, b, preferred_element_type=F32)


def _dot_nt(a, b):
    return lax.dot_general(a, b, (((1,), (1,)), ((), ())), preferred_element_type=F32)


def _sigmoid(x):
    return 1.0 / (1.0 + jnp.exp(-x))


def _silu(x):
    return x * _sigmoid(x)


def _rms(x, g):
    return x * lax.rsqrt(jnp.mean(x * x, axis=-1, keepdims=True) + RMS_EPS) * g


def _osm_update(scores, pvs, m, l, acc, axis, exp=jnp.exp):
    m_new = m
    for s in scores:
        m_new = jnp.maximum(m_new, jnp.max(s, axis=axis, keepdims=True))
    m_safe = jnp.where(m_new == NEG_INF, 0.0, m_new)
    alpha = exp(m - m_safe)
    l = alpha * l
    acc = alpha * acc
    for s, pv in zip(scores, pvs):
        p = exp(s - m_safe)
        l = l + jnp.sum(p, axis=axis, keepdims=True)
        acc = acc + pv(p.astype(BF16))
    return m_new, l, acc


def _top_blocks(g, idx, axis):
    picks = []
    for _ in range(MOBA_TOPK):
        mx = jnp.max(g, axis=axis, keepdims=True)
        a = jnp.min(jnp.where(g == mx, idx, LANES * LANES), axis=axis, keepdims=True)
        a = jnp.where(mx == NEG_INF, -1, a)
        picks.append(a)
        g = jnp.where(idx == a, NEG_INF, g)
    return picks


def _f32_key(x):
    b = lax.bitcast_convert_type(x, I32)
    k = b ^ ((b >> 31) & 0x7FFFFFFF)
    return jnp.where(x == 0.0, 0, k)


def _select_threshold(count_fn, k_top, n_keys):
    k_f = float(k_top)
    c0 = count_fn(lambda kt, pos: kt >= 0)
    t = jnp.where(c0 >= k_f, 0, INT_MIN).astype(I32)
    cnt_ge = jnp.where(c0 >= k_f, c0, 0.0)

    def bit_step(b, carry):
        t, cnt_ge = carry
        cand = t | jnp.left_shift(jnp.int32(1), 30 - b)
        cnt = count_fn(lambda kt, pos: kt >= cand)
        return jnp.where(cnt >= k_f, cand, t), jnp.where(cnt >= k_f, cnt, cnt_ge)

    t, cnt_ge = lax.fori_loop(0, 31, bit_step, (t, cnt_ge))
    t = jnp.maximum(t, INT_MIN + 1)
    n_bits = max(1, int(math.ceil(math.log2(n_keys))))

    def tie_search():
        need = k_f - count_fn(lambda kt, pos: kt > t)

        def pos_step(b, x):
            cand = x | jnp.left_shift(jnp.int32(1), n_bits - 1 - b)
            below = count_fn(lambda kt, pos: (kt == t) & (pos < cand))
            return jnp.where(below < need, cand, x)
        return lax.fori_loop(0, n_bits, pos_step, jnp.zeros_like(t))

    any_excess = jnp.max(jnp.where(cnt_ge > k_f, 1.0, 0.0)) > 0.0
    x = lax.cond(any_excess, tie_search, lambda: jnp.full_like(t, n_keys))
    return t, x


def _params(n):
    return pltpu.CompilerParams(dimension_semantics=("arbitrary",) * n, vmem_limit_bytes=VMEM_LIMIT)


def _bucket_thresholds():
    exact = N_BUCKETS // 2
    n = np.arange(MAX_DISTANCE + 1)
    nf = np.maximum(n, 1).astype(np.float32)
    large = exact + (np.log(nf / np.float32(exact)) / np.float32(math.log(MAX_DISTANCE / exact))
                     * np.float32(N_BUCKETS - exact)).astype(np.int32)
    bucket = np.where(n < exact, n, np.minimum(large, N_BUCKETS - 1))
    assert bucket[MAX_DISTANCE] == N_BUCKETS - 1 and np.all(np.diff(bucket) >= 0)
    return [int(np.argmax(bucket >= k)) for k in range(N_BUCKETS)]


def _bias_tiles_kernel(tab_ref, o_ref):
    h = pl.program_id(0)
    thr = _bucket_thresholds()
    key = lax.broadcasted_iota(I32, (TQ, TQ), 0)
    qry = lax.broadcasted_iota(I32, (TQ, TQ), 1)
    for d in range(2):
        dist = d * TQ + qry - key
        val = jnp.full((TQ, TQ), tab_ref[0, h], F32)
        for k in range(1, N_BUCKETS):
            val = jnp.where(dist >= thr[k], tab_ref[k, h], val)
        o_ref[d] = (val - tab_ref[N_BUCKETS - 1, h]) * LOG2E


def _bias_tiles(table):
    n_h = table.shape[1]
    return pl.pallas_call(
        _bias_tiles_kernel,
        grid=(n_h,),
        in_specs=[pl.BlockSpec(memory_space=pltpu.SMEM)],
        out_specs=pl.BlockSpec((None, 2, TQ, TQ), lambda h: (h, 0, 0, 0)),
        out_shape=jax.ShapeDtypeStruct((n_h, 2, TQ, TQ), F32),
        compiler_params=_params(1), name="bias_tiles",
    )(table)


def _proj_prompt_kernel(x_ref, g_ref, wn_ref, wt_ref, gains_ref,
                        ga_ref, gb_ref, katm_ref, kbtm_ref, kktm_ref,
                        qat_ref, qbt_ref, qit_ref, wit_ref, vat_ref, vbt_ref,
                        ka32_ref, va32_ref, kb32_ref, vb32_ref, ki32_ref):
    h = _rms(x_ref[...], g_ref[...]).astype(BF16)
    tm = h.shape[0]
    scale = HEAD_DIM ** -0.5 * LOG2E

    def grp_t(c):
        return _dot_nt(wt_ref[c * D_ATT:(c + 1) * D_ATT, :], h)

    def headnorm_t(zt, row):
        z3 = zt.reshape(N_HEADS, HEAD_DIM, tm)
        ssq = jnp.sum(z3 * z3, axis=1, keepdims=True)
        z3 = z3 * lax.rsqrt(ssq * (1.0 / HEAD_DIM) + RMS_EPS)
        return z3.reshape(D_ATT, tm) * gains_ref[row]

    ga_ref[...] = _dot(h, wn_ref[:, :D_ATT])
    gb_ref[...] = _dot(h, wn_ref[:, D_ATT:])
    qat_ref[...] = (headnorm_t(grp_t(0), 0) * scale).astype(BF16)
    kat = headnorm_t(grp_t(1), 1)
    ka32_ref[...] = kat
    katm_ref[...] = kat.T.astype(BF16)
    vat = grp_t(2)
    va32_ref[...] = vat
    vat_ref[...] = vat.astype(BF16)
    qbt_ref[...] = (headnorm_t(grp_t(3), 2) * scale).astype(BF16)
    kbt = headnorm_t(grp_t(4), 3)
    kb32_ref[...] = kbt
    kbtm_ref[...] = kbt.T.astype(BF16)
    vbt = grp_t(5)
    vb32_ref[...] = vbt
    vbt_ref[...] = vbt.astype(BF16)
    qit_ref[...] = grp_t(6).astype(BF16)
    kk = _dot_nt(wt_ref[7 * D_ATT:7 * D_ATT + LANES, :], h)
    ki32_ref[...] = kk[:D_IDX, :]
    kktm_ref[...] = kk.T.astype(BF16)
    wit_ref[...] = _dot_nt(wt_ref[7 * D_ATT + LANES:, :], h)[:H_IDX, :]


def _project_prompt(x, g, wn, wt, gains_t, batch, seq):
    tm = TQ
    n_t = seq // tm
    n = batch * seq
    row = lambda c: pl.BlockSpec((tm, c), lambda b, t: (b * n_t + t, 0))
    full = lambda a: pl.BlockSpec(a.shape, lambda b, t: (0,) * a.ndim)
    tiles = lambda r: pl.BlockSpec((None, None, r, tm), lambda b, t: (b, t, 0, 0))
    cols = lambda r: pl.BlockSpec((None, r, tm), lambda b, t: (b, 0, t))
    sd = jax.ShapeDtypeStruct
    tsd = lambda r, dt: sd((batch, n_t, r, tm), dt)
    out_shape = ([sd((n, D_ATT), F32)] * 2 + [sd((n, D_ATT), BF16)] * 2 + [sd((n, LANES), BF16)]
                 + [tsd(D_ATT, BF16)] * 3 + [tsd(H_IDX, F32)] + [tsd(D_ATT, BF16)] * 2
                 + [sd((batch, D_ATT, seq), F32)] * 4 + [sd((batch, D_IDX, seq), F32)])
    out_specs = ([row(D_ATT)] * 4 + [row(LANES)] + [tiles(D_ATT)] * 3 + [tiles(H_IDX)] + [tiles(D_ATT)] * 2
                 + [cols(D_ATT)] * 4 + [cols(D_IDX)])
    return pl.pallas_call(
        _proj_prompt_kernel,
        grid=(batch, n_t),
        in_specs=[row(D_MODEL), full(g), full(wn), full(wt), full(gains_t)],
        out_specs=out_specs, out_shape=out_shape,
        compiler_params=_params(2), name="proj_prompt",
    )(x, g, wn, wt, gains_t)


def _headnorm_rows(z, ones_ref, gain):
    ssq = _dot((z * z).astype(BF16), ones_ref[...])
    return z * lax.rsqrt(ssq * (1.0 / HEAD_DIM) + RMS_EPS) * gain


def _proj_sample_kernel(x_ref, g_ref, w_ref, gains_ref, ones_ref,
                        qa_ref, ka_ref, va_ref, qb_ref, kb_ref, vb_ref, qi_ref, wi_ref,
                        ga_ref, gb_ref, ka32_ref, va32_ref, kb32_ref, vb32_ref, ki32_ref):
    h = _rms(x_ref[...], g_ref[...]).astype(BF16)
    scale = HEAD_DIM ** -0.5

    def grp(c):
        return _dot(h, w_ref[:, c * D_ATT:(c + 1) * D_ATT])

    qa_ref[...] = (_headnorm_rows(grp(0), ones_ref, gains_ref[0:1, :]) * scale).astype(BF16)
    ka = _headnorm_rows(grp(1), ones_ref, gains_ref[1:2, :])
    ka32_ref[...] = ka
    ka_ref[...] = ka.astype(BF16)
    va = grp(2)
    va32_ref[...] = va
    va_ref[...] = va.astype(BF16)
    ga_ref[...] = grp(3)
    qb_ref[...] = (_headnorm_rows(grp(4), ones_ref, gains_ref[2:3, :]) * scale).astype(BF16)
    kb = _headnorm_rows(grp(5), ones_ref, gains_ref[3:4, :])
    kb32_ref[...] = kb
    kb_ref[...] = kb.astype(BF16)
    vb = grp(6)
    vb32_ref[...] = vb
    vb_ref[...] = vb.astype(BF16)
    gb_ref[...] = grp(7)
    qi_ref[...] = grp(8).astype(BF16)
    tail = _dot(h, w_ref[:, 9 * D_ATT:])
    ki32_ref[...] = tail[:, :D_IDX]
    wi_ref[...] = tail[:, LANES:]


def _project_sample(x, g, w, gains, ones_bd):
    n = x.shape[0]
    full = lambda a: pl.BlockSpec(a.shape, lambda i: (0,) * a.ndim)
    sd = lambda c, dt: jax.ShapeDtypeStruct((n, c), dt)
    out_shape = ([sd(D_ATT, BF16)] * 7 + [sd(LANES, F32)] + [sd(D_ATT, F32)] * 6 + [sd(D_IDX, F32)])
    return pl.pallas_call(
        _proj_sample_kernel,
        grid=(1,),
        in_specs=[full(x), full(g), full(w), full(gains), full(ones_bd)],
        out_specs=[full(s) for s in out_shape], out_shape=out_shape,
        compiler_params=_params(1), name="proj_sample",
    )(x, g, w, gains, ones_bd)


def _head_row_masks():
    r = lax.broadcasted_iota(I32, (LANES, 1), 0)
    return r < HEAD_DIM, r >= HEAD_DIM


def _pair_attention(i, qms, k_tile, v_tile, near_bias, mask_own, mask_past):
    def update(tiles, carry):
        raw = lambda half, j, bias: _dot(k_tile(j), qms[half]) + (bias(half) if bias else 0.0)
        scores = [[mask(half, j, raw(half, j, bias)) for j, bias, mask in tiles] for half in range(2)]
        out = []
        for half in range(2):
            pvs = [lambda p, j=j, half=half: _dot(v_tile(j, half), p) for j, _, _ in tiles]
            out.append(_osm_update(scores[half], pvs, *carry[half], axis=0, exp=jnp.exp2))
        return tuple(out)

    init = (jnp.full((1, TQ), NEG_INF, F32), jnp.zeros((1, TQ), F32), jnp.zeros((HEAD_DIM, TQ), F32))
    tiles = [(i, lambda half: near_bias(half, 0), lambda half, j, s: mask_own(half, s))]
    for d in range(1, 2 + FIRST_FAR):
        exists = i >= d
        tiles.append((jnp.maximum(i - d, 0), (lambda half: near_bias(half, 1)) if d == 1 else None,
                      lambda half, j, s, exists=exists: mask_past(half, j, s, exists)))
    carry = update(tiles, (init, init))
    n_far = jnp.maximum(i - 1 - FIRST_FAR, 0)
    done = 0
    for width in (4, 2, 1):
        n_w = (n_far - done) // width
        carry = lax.fori_loop(
            0, n_w,
            lambda u, c, done=done, width=width: update(
                [(done + width * u + w, None, lambda half, j, s: mask_past(half, j, s, True))
                 for w in range(width)], c),
            carry)
        done = done + n_w * width
    return [c[2] / c[1] for c in carry]


def _moba_prompt_kernel(qt_ref, k_ref, vt_ref, bias_ref, o_ref, kmean_ref, *, n_blk):
    i = pl.program_id(2)
    n_sub = kmean_ref.shape[0]

    @pl.when(i == 0)
    def _():
        kmean_ref[...] = jnp.zeros_like(kmean_ref)
        for j in range(n_blk):
            kb = k_ref[j * TQ:(j + 1) * TQ, :].astype(F32)
            kmean_ref[j:j + 1, :] = jnp.mean(kb, axis=0, keepdims=True)

    masks = _head_row_masks()
    qt = qt_ref[...]
    key = lax.broadcasted_iota(I32, (TQ, TQ), 0)
    qry = lax.broadcasted_iota(I32, (TQ, TQ), 1)
    blk = lax.broadcasted_iota(I32, (n_sub, TQ), 0)
    kmean = kmean_ref[...].astype(BF16)
    qms = [jnp.where(masks[half], qt, jnp.zeros_like(qt)) for half in range(2)]
    picks = [_top_blocks(jnp.where(blk < i, _dot(kmean, qm), NEG_INF), blk, axis=0) for qm in qms]

    def picked(half, j, s, exists):
        sel = (picks[half][0] == j) | (picks[half][1] == j) | (picks[half][2] == j)
        return jnp.where(sel if exists is True else sel & exists, s, NEG_INF)

    outs = _pair_attention(
        i, qms,
        k_tile=lambda j: k_ref[pl.ds(pl.multiple_of(j * TQ, TQ), TQ), :],
        v_tile=lambda j, half: vt_ref[j, half * HEAD_DIM:(half + 1) * HEAD_DIM, :],
        near_bias=lambda half, d: bias_ref[half, d],
        mask_own=lambda half, s: jnp.where(key <= qry, s, NEG_INF),
        mask_past=picked)
    o_ref[...] = jnp.concatenate(outs, axis=0).T


def _moba_prompt(qt, k, vt, bias, batch, seq):
    n_q = seq // TQ
    n_sub = -(-n_q // SUBLANES) * SUBLANES
    kern = functools.partial(_moba_prompt_kernel, n_blk=n_q)
    return pl.pallas_call(
        kern,
        grid=(batch, N_HEADS // 2, n_q),
        in_specs=[
            pl.BlockSpec((None, None, LANES, TQ), lambda b, hp, i: (b, i, hp, 0)),
            pl.BlockSpec((seq, LANES), lambda b, hp, i: (b, hp)),
            pl.BlockSpec((None, n_q, LANES, TQ), lambda b, hp, i: (b, 0, hp, 0)),
            pl.BlockSpec((2, 2, TQ, TQ), lambda b, hp, i: (hp, 0, 0, 0)),
        ],
        out_specs=pl.BlockSpec((TQ, LANES), lambda b, hp, i: (b * n_q + i, hp)),
        out_shape=jax.ShapeDtypeStruct((batch * seq, D_ATT), F32),
        scratch_shapes=[pltpu.VMEM((n_sub, LANES), F32)],
        compiler_params=_params(3), name="moba_prompt",
    )(qt, k, vt, bias)


def _dsa_prompt_kernel(qt_ref, qit_ref, wit_ref, kk_ref, k_ref, vt_ref, bias_ref, o_ref,
                       keys_ref, pen_ref, *, seq, k_top):
    i = pl.program_id(1)
    masks = _head_row_masks()
    key = lax.broadcasted_iota(I32, (TQ, TQ), 0)
    qry = lax.broadcasted_iota(I32, (TQ, TQ), 1)
    zero_b = jnp.zeros((LANES, TQ), BF16)

    def head_qt(ref, h):
        pair, half = divmod(h, 2)
        return jnp.where(masks[half], ref[pair * LANES:(pair + 1) * LANES, :], zero_b)

    def index_tile(j):
        kkj = kk_ref[pl.ds(pl.multiple_of(j * TQ, TQ), TQ), :]
        acc = jnp.zeros((TQ, TQ), F32)
        for h in range(H_IDX):
            acc = acc + wit_ref[h:h + 1, :] * jnp.maximum(_dot(kkj, head_qt(qit_ref, h)), 0.0)
        return _f32_key(acc)

    def index_pair(u, c):
        keys_ref[2 * u] = index_tile(2 * u)
        keys_ref[2 * u + 1] = index_tile(2 * u + 1)
        return c

    def index_one(j, c):
        keys_ref[j] = index_tile(j)
        return c

    lax.fori_loop(0, i // 2, index_pair, 0)
    lax.fori_loop(2 * (i // 2), i, index_one, 0)
    keys_ref[i] = jnp.where(key <= qry, index_tile(i), INT_MIN)

    def count_fn(pred):
        def one(j):
            w = jnp.where(pred(keys_ref[j], key + j * TQ), 1.0, 0.0)
            return jnp.sum(w.reshape(TQ // SUBLANES, SUBLANES, TQ), axis=0)
        n_two = (i + 1) // 2
        part = lax.fori_loop(0, n_two, lambda u, p: p + one(2 * u) + one(2 * u + 1),
                             jnp.zeros((SUBLANES, TQ), F32))
        part = lax.fori_loop(2 * n_two, i + 1, lambda j, p: p + one(j), part)
        return jnp.sum(part, axis=0, keepdims=True)

    t, x = _select_threshold(count_fn, k_top, seq)

    def pen_body(j, c):
        kt = keys_ref[j]
        sel = (kt > t) | ((kt == t) & (key + j * TQ <= x))
        pen_ref[j] = jnp.where(sel, 0.0, NEG_INF)
        return c

    lax.fori_loop(0, i + 1, pen_body, 0)
    outs = []
    for pair in range(N_HEADS // 2):
        lanes = slice(pair * LANES, (pair + 1) * LANES)
        vrows = lambda half, pair=pair: slice((2 * pair + half) * HEAD_DIM, (2 * pair + half + 1) * HEAD_DIM)
        outs += _pair_attention(
            i, [head_qt(qt_ref, 2 * pair + half) for half in range(2)],
            k_tile=lambda j, lanes=lanes: k_ref[pl.ds(pl.multiple_of(j * TQ, TQ), TQ), lanes],
            v_tile=lambda j, half, vrows=vrows: vt_ref[j, vrows(half), :],
            near_bias=lambda half, d, pair=pair: bias_ref[2 * pair + half, d],
            mask_own=lambda half, s: s + pen_ref[i],
            mask_past=lambda half, j, s, exists: (
                s + pen_ref[j] if exists is True else jnp.where(exists, s + pen_ref[j], NEG_INF)))
    o_ref[...] = jnp.concatenate(outs, axis=0).T


def _dsa_prompt(qt, qit, wit, kk, k, vt, bias, batch, seq):
    n_q = seq // TQ
    k_top = min(DSA_TOPK_MAX, seq // 4)
    kern = functools.partial(_dsa_prompt_kernel, seq=seq, k_top=k_top)
    tile = lambda r: pl.BlockSpec((None, None, r, TQ), lambda b, i: (b, i, 0, 0))
    rows = lambda c: pl.BlockSpec((seq, c), lambda b, i: (b, 0))
    return pl.pallas_call(
        kern,
        grid=(batch, n_q),
        in_specs=[
            tile(D_ATT), tile(D_ATT), tile(H_IDX),
            rows(LANES), rows(D_ATT),
            pl.BlockSpec((None, n_q, D_ATT, TQ), lambda b, i: (b, 0, 0, 0)),
            pl.BlockSpec((N_HEADS, 2, TQ, TQ), lambda b, i: (1, 0, 0, 0)),
        ],
        out_specs=pl.BlockSpec((TQ, D_ATT), lambda b, i: (b * n_q + i, 0)),
        out_shape=jax.ShapeDtypeStruct((batch * seq, D_ATT), F32),
        scratch_shapes=[pltpu.VMEM((n_q, TQ, TQ), I32), pltpu.VMEM((n_q, TQ, TQ), F32)],
        compiler_params=_params(2), name="dsa_prompt",
    )(qt, qit, wit, kk, k, vt, bias)


def _expand_tokens(a, n_tok):
    return jnp.broadcast_to(a[:, None, :], (n_tok, N_HEADS, a.shape[-1])).reshape(n_tok * N_HEADS, a.shape[-1])


def _collapse_heads(o, hmask, n_tok):
    return jnp.sum((o * hmask).reshape(n_tok, N_HEADS, D_ATT), axis=1)


def _sample_index_kernel(pt_ref, *refs, n_pages, n_tok, k_top, n_pp):
    kit_refs = refs[:n_pp]
    qi_ref, wcol_ref, knew_ref, pen_ref, keys_ref = refs[n_pp:]
    g = pl.program_id(1)

    def keys_of(kt):
        x = jnp.maximum(_dot(qi_ref[...], kt), 0.0) * wcol_ref[...]
        return _f32_key(jnp.sum(x.reshape(n_tok, H_IDX, kt.shape[1]), axis=1))

    keys = keys_of(jnp.concatenate([r[...] for r in kit_refs], axis=1).astype(BF16))
    for o in range(n_pp):
        keys_ref[g * n_pp + o] = keys[:, o * LANES:(o + 1) * LANES]

    @pl.when(g == n_pages // n_pp - 1)
    def _():
        lane = lax.broadcasted_iota(I32, (n_tok, LANES), 1)
        tok = lax.broadcasted_iota(I32, (n_tok, LANES), 0)
        keys_ref[n_pages] = jnp.where(lane <= tok, keys_of(knew_ref[...]), INT_MIN)
        shape = (n_pages + 1, n_tok, LANES)
        pos = lax.broadcasted_iota(I32, shape, 0) * LANES + lax.broadcasted_iota(I32, shape, 2)
        all_keys = keys_ref[...]

        def count_fn(pred):
            w = jnp.where(pred(all_keys, pos), 1.0, 0.0)
            terms = [w[j] for j in range(n_pages + 1)]
            while len(terms) > 1:
                terms = [a + b for a, b in zip(terms[::2], terms[1::2])] + terms[len(terms) & ~1:]
            return jnp.sum(terms[0], axis=1, keepdims=True)

        t, x = _select_threshold(count_fn, k_top, (n_pages + 1) * LANES)
        sel = (all_keys > t) | ((all_keys == t) & (pos <= x))
        pen_ref[...] = jnp.where(sel, 0.0, NEG_INF)


def _sample_index(page_table, kit, qi, wcol, knew, n_tok, k_top):
    n_seq, n_pages = page_table.shape
    n_pp = math.gcd(IDX_PAGES_PER_STEP, n_pages)
    kern = functools.partial(_sample_index_kernel, n_pages=n_pages, n_tok=n_tok, k_top=k_top, n_pp=n_pp)
    per_seq = lambda a: pl.BlockSpec((None,) + a.shape[1:], lambda b, g, pt: (b,) + (0,) * (a.ndim - 1))
    page = lambda o: pl.BlockSpec((None, D_IDX, PAGE_SIZE), lambda b, g, pt: (pt[b, g * n_pp + o], 0, 0))
    grid_spec = pltpu.PrefetchScalarGridSpec(
        num_scalar_prefetch=1, grid=(n_seq, n_pages // n_pp),
        in_specs=[page(o) for o in range(n_pp)] + [per_seq(qi), per_seq(wcol), per_seq(knew)],
        out_specs=pl.BlockSpec((None, n_pages + 1, n_tok, LANES), lambda b, g, pt: (b, 0, 0, 0)),
        scratch_shapes=[pltpu.VMEM((n_pages + 1, n_tok, LANES), I32)])
    return pl.pallas_call(
        kern, grid_spec=grid_spec,
        out_shape=jax.ShapeDtypeStruct((n_seq, n_pages + 1, n_tok, LANES), F32),
        compiler_params=_params(2), name="sample_index",
    )(page_table, *([kit] * n_pp), qi, wcol, knew)


def _sample_dsa_kernel(pt_ref, *refs, n_pages, n_tok, n_pp):
    kt_refs, vt_refs = refs[:n_pp], refs[n_pp:2 * n_pp]
    (q_ref, pen_ref, blast_ref, c31_ref, knew_ref, vnew_ref, bnew_ref, hmask_ref,
     o_ref, m_ref, l_ref, acc_ref) = refs[2 * n_pp:]
    g = pl.program_id(1)
    last = g == n_pages // n_pp - 1

    @pl.when(g == 0)
    def _():
        m_ref[...] = jnp.full(m_ref.shape, NEG_INF, F32)
        l_ref[...] = jnp.zeros(l_ref.shape, F32)
        acc_ref[...] = jnp.zeros(acc_ref.shape, F32)

    q = q_ref[...]
    scores, pvs = [], []
    for o in range(n_pp):
        bias = c31_ref[...] if o < n_pp - 1 else jnp.where(last, blast_ref[...], c31_ref[...])
        scores.append(_dot(q, kt_refs[o][...].astype(BF16)) + bias + _expand_tokens(pen_ref[g * n_pp + o], n_tok))
        pvs.append(lambda pp, o=o: _dot_nt(pp, vt_refs[o][...].astype(BF16)))
    m, l, acc = _osm_update(scores, pvs, m_ref[...], l_ref[...], acc_ref[...], axis=1)
    m_ref[...] = m
    l_ref[...] = l
    acc_ref[...] = acc

    @pl.when(last)
    def _():
        pen_new = _expand_tokens(pen_ref[n_pages][:, :NEW_PAD], n_tok)
        s_n = _dot_nt(q, knew_ref[...]) + bnew_ref[...] + pen_new
        _, l2, acc2 = _osm_update([s_n], [lambda pp: _dot(pp, vnew_ref[...])], m, l, acc, axis=1)
        o_ref[...] = _collapse_heads(acc2 / l2, hmask_ref[...], n_tok)


def _sample_dsa(page_table, kt, vt, q, pen, blast, c31col, knew, vnew, bnew, hmask, n_tok):
    n_seq, n_pages = page_table.shape
    n_pp = math.gcd(PAGES_PER_STEP, n_pages)
    rows = n_tok * N_HEADS
    kern = functools.partial(_sample_dsa_kernel, n_pages=n_pages, n_tok=n_tok, n_pp=n_pp)
    per_seq = lambda a: pl.BlockSpec((None,) + a.shape[1:], lambda b, g, pt: (b,) + (0,) * (a.ndim - 1))
    full = lambda a: pl.BlockSpec(a.shape, lambda b, g, pt: (0,) * a.ndim)
    page = lambda o: pl.BlockSpec((None, D_ATT, PAGE_SIZE), lambda b, g, pt: (pt[b, g * n_pp + o], 0, 0))
    pages = [page(o) for o in range(n_pp)]
    grid_spec = pltpu.PrefetchScalarGridSpec(
        num_scalar_prefetch=1, grid=(n_seq, n_pages // n_pp),
        in_specs=pages + pages + [per_seq(q), per_seq(pen), full(blast), full(c31col),
                                  per_seq(knew), per_seq(vnew), full(bnew), full(hmask)],
        out_specs=pl.BlockSpec((None, n_tok, D_ATT), lambda b, g, pt: (b, 0, 0)),
        scratch_shapes=[pltpu.VMEM((rows, 1), F32), pltpu.VMEM((rows, 1), F32), pltpu.VMEM((rows, D_ATT), F32)])
    return pl.pallas_call(
        kern, grid_spec=grid_spec,
        out_shape=jax.ShapeDtypeStruct((n_seq, n_tok, D_ATT), F32),
        compiler_params=_params(2), name="sample_dsa",
    )(page_table, *([kt] * n_pp), *([vt] * n_pp), q, pen, blast, c31col, knew, vnew, bnew, hmask)


def _sample_moba_kernel(pt_ref, *refs, n_blk, n_tok, n_pp):
    kt_refs, vt_refs = refs[:n_pp], refs[n_pp:2 * n_pp]
    (q_ref, blast_ref, c31_ref, knew_ref, vnew_ref, bnew_ref, hmask_ref,
     o_ref, gate_ref, ms_ref, ls_ref, accs_ref) = refs[2 * n_pp:]
    g = pl.program_id(1)
    n_bb = n_pp // 2
    last = g == n_blk // n_bb - 1
    rows = n_tok * N_HEADS
    lane = lax.broadcasted_iota(I32, (rows, LANES), 1)

    @pl.when(g == 0)
    def _():
        gate_ref[...] = jnp.full(gate_ref.shape, NEG_INF, F32)
        ms_ref[...] = jnp.full(ms_ref.shape, NEG_INF, F32)
        ls_ref[...] = jnp.zeros(ls_ref.shape, F32)

    q = q_ref[...]
    gates, ms, ls = gate_ref[...], ms_ref[...], ls_ref[...]
    raw = [_dot(q, r[...].astype(BF16)) for r in kt_refs]
    parts = []
    for bb in range(n_bb):
        r0, r1 = raw[2 * bb], raw[2 * bb + 1]
        gate = jnp.sum(r0 + r1, axis=1, keepdims=True)
        s0 = r0 + c31_ref[...]
        s1 = r1 + (c31_ref[...] if bb < n_bb - 1 else jnp.where(last, blast_ref[...], c31_ref[...]))
        m = jnp.maximum(jnp.max(s0, axis=1, keepdims=True), jnp.max(s1, axis=1, keepdims=True))
        p0 = jnp.exp(s0 - m)
        p1 = jnp.exp(s1 - m)
        l = jnp.sum(p0, axis=1, keepdims=True) + jnp.sum(p1, axis=1, keepdims=True)
        parts.append((gate, m, l, p0.astype(BF16), p1.astype(BF16)))
    for bb, (gate, m, l, p0, p1) in enumerate(parts):
        j = g * n_bb + bb
        accs_ref[j] = (_dot_nt(p0, vt_refs[2 * bb][...].astype(BF16))
                       + _dot_nt(p1, vt_refs[2 * bb + 1][...].astype(BF16)))
        gates = jnp.where(lane == j, gate, gates)
        ms = jnp.where(lane == j, m, ms)
        ls = jnp.where(lane == j, l, ls)
    gate_ref[...] = gates
    ms_ref[...] = ms
    ls_ref[...] = ls

    @pl.when(last)
    def _():
        picks = _top_blocks(jnp.where(lane < n_blk, gates, NEG_INF), lane, axis=1)
        sel = (lane == picks[0]) | (lane == picks[1]) | (lane == picks[2])
        s_n = _dot_nt(q, knew_ref[...]) + bnew_ref[...]
        m_o = jnp.max(s_n, axis=1, keepdims=True)
        p_o = jnp.exp(s_n - m_o)
        l_o = jnp.sum(p_o, axis=1, keepdims=True)
        m_all = jnp.maximum(m_o, jnp.max(jnp.where(sel, ms, NEG_INF), axis=1, keepdims=True))
        w = jnp.where(sel, jnp.exp(ms - m_all), 0.0)
        w_o = jnp.exp(m_o - m_all)
        l_all = jnp.sum(w * ls, axis=1, keepdims=True) + w_o * l_o
        out = w_o * _dot(p_o.astype(BF16), vnew_ref[...])
        for jj in range(n_blk):
            out = out + w[:, jj:jj + 1] * accs_ref[jj]
        o_ref[...] = _collapse_heads(out / l_all, hmask_ref[...], n_tok)


def _sample_moba(page_table, kt, vt, q, blast, c31col, knew, vnew, bnew, hmask, n_tok):
    n_seq, n_pages = page_table.shape
    pages_per_blk = MOBA_BLOCK // PAGE_SIZE
    n_blk = n_pages // pages_per_blk
    n_pp = pages_per_blk * math.gcd(PAGES_PER_STEP // pages_per_blk, n_blk)
    rows = n_tok * N_HEADS
    kern = functools.partial(_sample_moba_kernel, n_blk=n_blk, n_tok=n_tok, n_pp=n_pp)
    per_seq = lambda a: pl.BlockSpec((None,) + a.shape[1:], lambda b, g, pt: (b,) + (0,) * (a.ndim - 1))
    full = lambda a: pl.BlockSpec(a.shape, lambda b, g, pt: (0,) * a.ndim)
    page = lambda o: pl.BlockSpec((None, D_ATT, PAGE_SIZE), lambda b, g, pt: (pt[b, g * n_pp + o], 0, 0))
    pages = [page(o) for o in range(n_pp)]
    grid_spec = pltpu.PrefetchScalarGridSpec(
        num_scalar_prefetch=1, grid=(n_seq, n_pages // n_pp),
        in_specs=pages + pages + [per_seq(q), full(blast), full(c31col),
                                  per_seq(knew), per_seq(vnew), full(bnew), full(hmask)],
        out_specs=pl.BlockSpec((None, n_tok, D_ATT), lambda b, g, pt: (b, 0, 0)),
        scratch_shapes=[pltpu.VMEM((rows, LANES), F32)] * 3 + [pltpu.VMEM((n_blk, rows, D_ATT), F32)])
    return pl.pallas_call(
        kern, grid_spec=grid_spec,
        out_shape=jax.ShapeDtypeStruct((n_seq, n_tok, D_ATT), F32),
        compiler_params=_params(2), name="sample_moba",
    )(page_table, *([kt] * n_pp), *([vt] * n_pp), q, blast, c31col, knew, vnew, bnew, hmask)


def _merge_and_conv_in(x_ref, oa_ref, ob_ref, ga_ref, gb_ref, wo_ref, g1_ref, wc_ref):
    o = jnp.concatenate([oa_ref[...] * _silu(ga_ref[...]), ob_ref[...] * _silu(gb_ref[...])], axis=1)
    h1 = x_ref[...] + _dot(o.astype(BF16), wo_ref[...])
    hn = _rms(h1, g1_ref[...]).astype(BF16)
    a = _dot(hn, wc_ref[:, 0:D_MODEL])
    b = _dot(hn, wc_ref[:, D_MODEL:2 * D_MODEL])
    g = _dot(hn, wc_ref[:, 2 * D_MODEL:3 * D_MODEL])
    return h1, a * _sigmoid(b), g


def _ln_gate_out(c, g, h1, lg_ref, lb_ref, wo2_ref):
    mu = jnp.mean(c, axis=-1, keepdims=True)
    d = c - mu
    var = jnp.mean(d * d, axis=-1, keepdims=True)
    cn = d * lax.rsqrt(var + LN_EPS) * lg_ref[...] + lb_ref[...]
    z = (_silu(cn) * _silu(g)).astype(BF16)
    return h1 + _dot(z, wo2_ref[...])


def _layer1_prompt_kernel(x_ref, oa_ref, ob_ref, ga_ref, gb_ref, wo_ref, g1_ref, wc_ref,
                          cw_ref, cb_ref, lg_ref, lb_ref, wo2_ref, y_ref, st_ref, up_ref, c_ref, *, tt):
    t = pl.program_id(1)

    @pl.when(t == 0)
    def _():
        up_ref[0:HIST, :] = jnp.zeros((HIST, D_MODEL), F32)

    h1, u, g = _merge_and_conv_in(x_ref, oa_ref, ob_ref, ga_ref, gb_ref, wo_ref, g1_ref, wc_ref)
    up_ref[HIST:HIST + tt, :] = u
    base = HIST - (CONV_WIDTH - 1)
    rows = min(tt, CONV_ROWS)
    for cc in range(0, D_MODEL, LANES):
        cs = slice(cc, cc + LANES)
        for r0 in range(0, tt, rows):
            acc = jnp.zeros((rows, LANES), F32) + cb_ref[:, cs]
            for phase in range(SUBLANES):
                taps = [w for w in range(CONV_WIDTH) if (base + w) % SUBLANES == phase]
                n_win = rows + (SUBLANES if phase else 0)
                group = None
                for w in taps:
                    start = r0 + base + w - phase
                    term = cw_ref[w:w + 1, cs] * up_ref[start:start + n_win, cs]
                    group = term if group is None else group + term
                if group is not None:
                    acc = acc + group[phase:phase + rows, :]
            c_ref[r0:r0 + rows, cs] = acc
    y_ref[...] = _ln_gate_out(c_ref[...], g, h1, lg_ref, lb_ref, wo2_ref)
    tail = up_ref[tt:tt + HIST, :]
    st_ref[...] = tail
    up_ref[0:HIST, :] = tail


def _layer1_prompt(x, oa, ob, ga, gb, wo, g1, wc, cw, cb, lg, lb, wo2, batch, seq, tt):
    n_t = seq // tt
    kern = functools.partial(_layer1_prompt_kernel, tt=tt)
    row = lambda c: pl.BlockSpec((tt, c), lambda b, t: (b * n_t + t, 0))
    full = lambda a: pl.BlockSpec(a.shape, lambda b, t: (0,) * a.ndim)
    return pl.pallas_call(
        kern,
        grid=(batch, n_t),
        in_specs=[row(D_MODEL), row(D_ATT), row(D_ATT), row(D_ATT), row(D_ATT),
                  full(wo), full(g1), full(wc), full(cw), full(cb), full(lg), full(lb), full(wo2)],
        out_specs=[row(D_MODEL), pl.BlockSpec((None, HIST, D_MODEL), lambda b, t: (b, 0, 0))],
        out_shape=[jax.ShapeDtypeStruct((batch * seq, D_MODEL), F32),
                   jax.ShapeDtypeStruct((batch, HIST, D_MODEL), F32)],
        scratch_shapes=[pltpu.VMEM((HIST + tt, D_MODEL), F32), pltpu.VMEM((tt, D_MODEL), F32)],
        compiler_params=_params(2), name="layer1_prompt",
    )(x, oa, ob, ga, gb, wo, g1, wc, cw, cb, lg, lb, wo2)


def _layer1_sample_kernel(x_ref, oa_ref, ob_ref, ga_ref, gb_ref, wo_ref, g1_ref, wc_ref,
                          cw_ref, cb_ref, lg_ref, lb_ref, wo2_ref, st_ref, y_ref, ns_ref, c_ref, *, n_tok, n_seq):
    n_hist = CONV_WIDTH - 1
    h1, u, g = _merge_and_conv_in(x_ref, oa_ref, ob_ref, ga_ref, gb_ref, wo_ref, g1_ref, wc_ref)
    u_t = [u[t * n_seq:(t + 1) * n_seq, :] for t in range(n_tok)]
    for t in range(n_tok):
        acc = jnp.zeros((n_seq, D_MODEL), F32) + cb_ref[...]
        for r in range(t, n_hist):
            acc = acc + cw_ref[r - t:r - t + 1, :] * st_ref[r]
        for r in range(t + 1):
            acc = acc + cw_ref[n_hist - t + r:n_hist - t + r + 1, :] * u_t[r]
        c_ref[t * n_seq:(t + 1) * n_seq, :] = acc
    y_ref[...] = _ln_gate_out(c_ref[...], g, h1, lg_ref, lb_ref, wo2_ref)
    for r in range(n_hist - n_tok):
        ns_ref[r] = st_ref[r + n_tok]
    for t in range(n_tok):
        ns_ref[n_hist - n_tok + t] = u_t[t]


def _layer1_sample(x, oa, ob, ga, gb, wo, g1, wc, cw, cb, lg, lb, wo2, st, n_tok, n_seq):
    kern = functools.partial(_layer1_sample_kernel, n_tok=n_tok, n_seq=n_seq)
    full = lambda a: pl.BlockSpec(a.shape, lambda i: (0,) * a.ndim)
    args = (x, oa, ob, ga, gb, wo, g1, wc, cw, cb, lg, lb, wo2, st)
    out_shape = [jax.ShapeDtypeStruct(x.shape, F32), jax.ShapeDtypeStruct(st.shape, F32)]
    return pl.pallas_call(
        kern, grid=(1,),
        in_specs=[full(a) for a in args],
        out_specs=[full(s) for s in out_shape], out_shape=out_shape,
        scratch_shapes=[pltpu.VMEM(x.shape, F32)],
        compiler_params=_params(1), name="layer1_sample",
    )(*args)


def _t5_bucket(n):
    exact = N_BUCKETS // 2
    nf = jnp.maximum(n, 1).astype(F32)
    large = exact + (jnp.log(nf / exact) / math.log(MAX_DISTANCE / exact) * (N_BUCKETS - exact)).astype(I32)
    large = jnp.minimum(large, N_BUCKETS - 1)
    return jnp.where(n < exact, n, large)


def _bias_by_distance(table, n):
    return table[_t5_bucket(jnp.arange(n, dtype=I32))]


def _prep_attn_weights(w_in, qn_a, kn_a, qn_b, kn_b):
    grp = lambda c: w_in[:, c * D_ATT:(c + 1) * D_ATT]
    ki = w_in[:, 9 * D_ATT:9 * D_ATT + D_IDX]
    wi = w_in[:, 9 * D_ATT + D_IDX:]
    zeros = lambda c: jnp.zeros((D_MODEL, c), w_in.dtype)
    tile = lambda g: jnp.tile(g, N_HEADS)
    head = np.arange(D_ATT) // HEAD_DIM
    ones_bd = jnp.asarray(head[:, None] == head[None, :], dtype=BF16)
    wn = jnp.concatenate([grp(3), grp(7)], axis=1).astype(BF16)
    wt = jnp.concatenate([grp(0), grp(1), grp(2), grp(4), grp(5), grp(6), grp(8), ki, ki, wi,
                          zeros(NEW_PAD - H_IDX)], axis=1).T.astype(BF16)
    gains_t = jnp.stack([tile(qn_a), tile(kn_a), tile(qn_b), tile(kn_b)]).astype(F32)[:, :, None]
    ws = jnp.concatenate([w_in[:, :9 * D_ATT + D_IDX], zeros(LANES - D_IDX), wi, zeros(LANES - H_IDX)],
                         axis=1).astype(BF16)
    gains_s = jnp.stack([tile(qn_a), tile(kn_a), tile(qn_b), tile(kn_b)]).astype(F32)
    return wn, wt, gains_t, ws, gains_s, ones_bd


def kernel(x_prompt, x_sample, cache_k_a, cache_v_a, cache_k_b, cache_v_b, cache_kidx_b, state_conv, page_table,
           norm_g, rel_bias_table, w_in_attn, q_norm_a, k_norm_a, q_norm_b, k_norm_b, w_out_attn,
           w_in_conv, conv_w, conv_b, conv_ln_g, conv_ln_b, w_out_conv):
    batch, seq, _ = x_prompt.shape
    n_seq, n_tok, _ = x_sample.shape
    n_pages = page_table.shape[1]
    past = n_pages * PAGE_SIZE
    assert seq % TQ == 0 and past % MOBA_BLOCK == 0 and n_tok <= NEW_PAD and TQ >= MAX_DISTANCE
    n_hist = CONV_WIDTH - 1
    wn, wt, gains_t, ws, gains_s, ones_bd = _prep_attn_weights(
        w_in_attn[0], q_norm_a[0], k_norm_a[0], q_norm_b[0], k_norm_b[0])
    g0 = norm_g[0][None, :]
    g1 = norm_g[1][None, :]
    wo = w_out_attn[0].astype(BF16)
    wc = w_in_conv[0].astype(BF16)
    wo2 = w_out_conv[0].astype(BF16)
    cw = jnp.concatenate([conv_w[0], jnp.zeros((HIST - CONV_WIDTH, D_MODEL), F32)], axis=0)
    conv_rows = (cw, conv_b[0][None, :], conv_ln_g[0][None, :], conv_ln_b[0][None, :])
    c31 = rel_bias_table[N_BUCKETS - 1]
    bias_tiles = _bias_tiles(rel_bias_table)

    xp = x_prompt.reshape(batch * seq, D_MODEL)
    (ga, gb, katm, kbtm, kktm, qat, qbt, qit, wit, vat, vbt, ka32, va32, kb32, vb32, ki32) = _project_prompt(
        xp, g0, wn, wt, gains_t, batch, seq)
    oa = _moba_prompt(qat, katm, vat, bias_tiles, batch, seq)
    ob = _dsa_prompt(qbt, qit, wit, kktm, kbtm, vbt, bias_tiles, batch, seq)
    y_p, st_p = _layer1_prompt(xp, oa, ob, ga, gb, wo, g1, wc, *conv_rows, wo2, batch, seq, tt=TQ)
    heads_p = lambda a: jnp.transpose(a.reshape(batch, N_HEADS, HEAD_DIM, seq), (0, 3, 1, 2))[None]
    ki_p = jnp.transpose(ki32, (0, 2, 1))[None]
    cv_p = st_p[None, :, HIST - n_hist:, :]

    rows = n_tok * N_HEADS
    xs = x_sample.reshape(n_seq * n_tok, D_MODEL)
    (qa_s, ka_s, va_s, qb_s, kb_s, vb_s, qi_s, wi_s, ga_s, gb_s, ka32_s, va32_s, kb32_s, vb32_s, ki32_s) = \
        _project_sample(xs, g0, ws, gains_s, ones_bd)
    head_of_lane = np.arange(D_ATT) // HEAD_DIM
    hmask8 = (np.arange(N_HEADS)[:, None] == head_of_lane[None, :])
    hmask = jnp.asarray(np.tile(hmask8, (n_tok, 1)), F32)
    qbd = lambda q: (q.reshape(n_seq, n_tok, 1, D_ATT) * jnp.asarray(hmask8, BF16)).reshape(n_seq, rows, D_ATT)
    newpad = lambda a: jnp.pad(a.reshape(n_seq, n_tok, D_ATT), ((0, 0), (0, NEW_PAD - n_tok), (0, 0)))
    bd = _bias_by_distance(rel_bias_table, 2 * PAGE_SIZE)
    t_of_row = np.arange(rows) // N_HEADS
    h_of_row = np.arange(rows) % N_HEADS
    c_idx = np.arange(PAGE_SIZE)
    dist_last = PAGE_SIZE + t_of_row[:, None] - c_idx[None, :]
    tn = np.arange(NEW_PAD)
    dist_new = np.maximum(t_of_row[:, None] - tn[None, :], 0)
    valid_new = (tn[None, :] <= t_of_row[:, None]) & (tn[None, :] < n_tok)

    def sample_bias(off):
        hh = h_of_row + off
        blast = bd[dist_last, hh[:, None]]
        bnew = jnp.where(valid_new, bd[dist_new, hh[:, None]], NEG_INF)
        return blast, c31[hh][:, None], bnew

    pages = lambda c: jnp.transpose(c[0], (0, 2, 3, 1)).reshape(c.shape[1], D_ATT, PAGE_SIZE)
    kit = jnp.transpose(cache_kidx_b[0], (0, 2, 1))
    blast_a, c31_a, bnew_a = sample_bias(0)
    oa_s = _sample_moba(page_table, pages(cache_k_a), pages(cache_v_a), qbd(qa_s), blast_a, c31_a,
                        newpad(ka_s), newpad(va_s), bnew_a, hmask, n_tok)
    k_top = min(DSA_TOPK_MAX, (past + n_tok) // 4)
    knew_i = jnp.pad(jnp.transpose(ki32_s.reshape(n_seq, n_tok, D_IDX), (0, 2, 1)),
                     ((0, 0), (0, 0), (0, PAGE_SIZE - n_tok))).astype(BF16)
    pen = _sample_index(page_table, kit, qi_s.reshape(n_seq, rows, D_IDX),
                        wi_s[:, :H_IDX].reshape(n_seq, rows, 1), knew_i, n_tok, k_top)
    blast_b, c31_b, bnew_b = sample_bias(N_HEADS)
    ob_s = _sample_dsa(page_table, pages(cache_k_b), pages(cache_v_b), qbd(qb_s), pen, blast_b, c31_b,
                       newpad(kb_s), newpad(vb_s), bnew_b, hmask, n_tok)
    tmaj = lambda a: jnp.transpose(a.reshape(n_seq, n_tok, -1), (1, 0, 2)).reshape(n_tok * n_seq, -1)
    st_in = jnp.transpose(state_conv[0], (1, 0, 2))
    y_s, ns = _layer1_sample(tmaj(xs), tmaj(oa_s), tmaj(ob_s), tmaj(ga_s), tmaj(gb_s), wo, g1, wc,
                             *conv_rows, wo2, st_in, n_tok, n_seq)
    y_s = jnp.transpose(y_s.reshape(n_tok, n_seq, D_MODEL), (1, 0, 2))
    cv_s = jnp.transpose(ns, (1, 0, 2))[None]
    heads_s = lambda a: a.reshape(1, n_seq, n_tok, N_HEADS, HEAD_DIM)

    return (y_p.reshape(batch, seq, D_MODEL), y_s,
            heads_p(ka32), heads_p(va32), heads_p(kb32), heads_p(vb32), ki_p, cv_p,
            heads_s(ka32_s), heads_s(va32_s), heads_s(kb32_s), heads_s(vb32_s),
            ki32_s.reshape(1, n_seq, n_tok, D_IDX), cv_s)
```

```python
import functools
import math

import numpy as np
import jax
import jax.numpy as jnp
from jax import lax
from jax.experimental import pallas as pl
from jax.experimental.pallas import tpu as pltpu

F32 = jnp.float32
BF16 = jnp.bfloat16
I32 = jnp.int32

D_MODEL = 1024
HEAD_DIM = 64
N_HEADS = 8
D_ATT = N_HEADS * HEAD_DIM
MOBA_BLOCK = 256
MOBA_TOPK = 3
DSA_TOPK_MAX = 256
H_IDX = 8
D_IDX = 64
CONV_WIDTH = 31
N_BUCKETS = 32
MAX_DISTANCE = 128
PAGE_SIZE = 128
RMS_EPS = 1e-6
LN_EPS = 1e-5

LANES = 128
SUBLANES = 8
TQ = MOBA_BLOCK
HIST = 32
NEW_PAD = 16
PAGES_PER_STEP = 32
IDX_PAGES_PER_STEP = 32
CONV_ROWS = 128
FIRST_FAR = 2
INT_MIN = -2 ** 31
NEG_INF = float("-inf")
LOG2E = math.log2(math.e)
VMEM_LIMIT = 56 * 1024 * 1024


def _dot(a, b):
    return jnp.dot(a, b, preferred_element_type=F32)


def _dot_nt(a, b):
    return lax.dot_general(a, b, (((1,), (1,)), ((), ())), preferred_element_type=F32)


def _sigmoid(x):
    return 1.0 / (1.0 + jnp.exp(-x))


def _silu(x):
    return x * _sigmoid(x)


def _rms(x, g):
    return x * lax.rsqrt(jnp.mean(x * x, axis=-1, keepdims=True) + RMS_EPS) * g


def _osm_update(scores, pvs, m, l, acc, axis, exp=jnp.exp):
    m_new = m
    for s in scores:
        m_new = jnp.maximum(m_new, jnp.max(s, axis=axis, keepdims=True))
    m_safe = jnp.where(m_new == NEG_INF, 0.0, m_new)
    alpha = exp(m - m_safe)
    l = alpha * l
    acc = alpha * acc
    for s, pv in zip(scores, pvs):
        p = exp(s - m_safe)
        l = l + jnp.sum(p, axis=axis, keepdims=True)
        acc = acc + pv(p.astype(BF16))
    return m_new, l, acc


def _top_blocks(g, idx, axis):
    picks = []
    for _ in range(MOBA_TOPK):
        mx = jnp.max(g, axis=axis, keepdims=True)
        a = jnp.min(jnp.where(g == mx, idx, LANES * LANES), axis=axis, keepdims=True)
        a = jnp.where(mx == NEG_INF, -1, a)
        picks.append(a)
        g = jnp.where(idx == a, NEG_INF, g)
    return picks


def _f32_key(x):
    b = lax.bitcast_convert_type(x, I32)
    k = b ^ ((b >> 31) & 0x7FFFFFFF)
    return jnp.where(x == 0.0, 0, k)


def _select_threshold(count_fn, k_top, n_keys):
    k_f = float(k_top)
    c0 = count_fn(lambda kt, pos: kt >= 0)
    t = jnp.where(c0 >= k_f, 0, INT_MIN).astype(I32)
    cnt_ge = jnp.where(c0 >= k_f, c0, 0.0)

    def bit_step(b, carry):
        t, cnt_ge = carry
        cand = t | jnp.left_shift(jnp.int32(1), 30 - b)
        cnt = count_fn(lambda kt, pos: kt >= cand)
        return jnp.where(cnt >= k_f, cand, t), jnp.where(cnt >= k_f, cnt, cnt_ge)

    t, cnt_ge = lax.fori_loop(0, 31, bit_step, (t, cnt_ge))
    t = jnp.maximum(t, INT_MIN + 1)
    n_bits = max(1, int(math.ceil(math.log2(n_keys))))

    def tie_search():
        need = k_f - count_fn(lambda kt, pos: kt > t)

        def pos_step(b, x):
            cand = x | jnp.left_shift(jnp.int32(1), n_bits - 1 - b)
            below = count_fn(lambda kt, pos: (kt == t) & (pos < cand))
            return jnp.where(below < need, cand, x)
        return lax.fori_loop(0, n_bits, pos_step, jnp.zeros_like(t))

    any_excess = jnp.max(jnp.where(cnt_ge > k_f, 1.0, 0.0)) > 0.0
    x = lax.cond(any_excess, tie_search, lambda: jnp.full_like(t, n_keys))
    return t, x


def _params(n):
    return pltpu.CompilerParams(dimension_semantics=("arbitrary",) * n, vmem_limit_bytes=VMEM_LIMIT)


def _bucket_thresholds():
    exact = N_BUCKETS // 2
    n = np.arange(MAX_DISTANCE + 1)
    nf = np.maximum(n, 1).astype(np.float32)
    large = exact + (np.log(nf / np.float32(exact)) / np.float32(math.log(MAX_DISTANCE / exact))
                     * np.float32(N_BUCKETS - exact)).astype(np.int32)
    bucket = np.where(n < exact, n, np.minimum(large, N_BUCKETS - 1))
    assert bucket[MAX_DISTANCE] == N_BUCKETS - 1 and np.all(np.diff(bucket) >= 0)
    return [int(np.argmax(bucket >= k)) for k in range(N_BUCKETS)]


def _bias_tiles_kernel(tab_ref, o_ref):
    h = pl.program_id(0)
    thr = _bucket_thresholds()
    key = lax.broadcasted_iota(I32, (TQ, TQ), 0)
    qry = lax.broadcasted_iota(I32, (TQ, TQ), 1)
    for d in range(2):
        dist = d * TQ + qry - key
        val = jnp.full((TQ, TQ), tab_ref[0, h], F32)
        for k in range(1, N_BUCKETS):
            val = jnp.where(dist >= thr[k], tab_ref[k, h], val)
        o_ref[d] = (val - tab_ref[N_BUCKETS - 1, h]) * LOG2E


def _bias_tiles(table):
    n_h = table.shape[1]
    return pl.pallas_call(
        _bias_tiles_kernel,
        grid=(n_h,),
        in_specs=[pl.BlockSpec(memory_space=pltpu.SMEM)],
        out_specs=pl.BlockSpec((None, 2, TQ, TQ), lambda h: (h, 0, 0, 0)),
        out_shape=jax.ShapeDtypeStruct((n_h, 2, TQ, TQ), F32),
        compiler_params=_params(1), name="bias_tiles",
    )(table)


def _proj_prompt_kernel(x_ref, g_ref, wn_ref, wt_ref, gains_ref,
                        ga_ref, gb_ref, katm_ref, kbtm_ref, kktm_ref,
                        qat_ref, qbt_ref, qit_ref, wit_ref, vat_ref, vbt_ref,
                        ka32_ref, va32_ref, kb32_ref, vb32_ref, ki32_ref):
    h = _rms(x_ref[...], g_ref[...]).astype(BF16)
    tm = h.shape[0]
    scale = HEAD_DIM ** -0.5 * LOG2E

    def grp_t(c):
        return _dot_nt(wt_ref[c * D_ATT:(c + 1) * D_ATT, :], h)

    def headnorm_t(zt, row):
        z3 = zt.reshape(N_HEADS, HEAD_DIM, tm)
        ssq = jnp.sum(z3 * z3, axis=1, keepdims=True)
        z3 = z3 * lax.rsqrt(ssq * (1.0 / HEAD_DIM) + RMS_EPS)
        return z3.reshape(D_ATT, tm) * gains_ref[row]

    ga_ref[...] = _dot(h, wn_ref[:, :D_ATT])
    gb_ref[...] = _dot(h, wn_ref[:, D_ATT:])
    qat_ref[...] = (headnorm_t(grp_t(0), 0) * scale).astype(BF16)
    kat = headnorm_t(grp_t(1), 1)
    ka32_ref[...] = kat
    katm_ref[...] = kat.T.astype(BF16)
    vat = grp_t(2)
    va32_ref[...] = vat
    vat_ref[...] = vat.astype(BF16)
    qbt_ref[...] = (headnorm_t(grp_t(3), 2) * scale).astype(BF16)
    kbt = headnorm_t(grp_t(4), 3)
    kb32_ref[...] = kbt
    kbtm_ref[...] = kbt.T.astype(BF16)
    vbt = grp_t(5)
    vb32_ref[...] = vbt
    vbt_ref[...] = vbt.astype(BF16)
    qit_ref[...] = grp_t(6).astype(BF16)
    kk = _dot_nt(wt_ref[7 * D_ATT:7 * D_ATT + LANES, :], h)
    ki32_ref[...] = kk[:D_IDX, :]
    kktm_ref[...] = kk.T.astype(BF16)
    wit_ref[...] = _dot_nt(wt_ref[7 * D_ATT + LANES:, :], h)[:H_IDX, :]


def _project_prompt(x, g, wn, wt, gains_t, batch, seq):
    tm = TQ
    n_t = seq // tm
    n = batch * seq
    row = lambda c: pl.BlockSpec((tm, c), lambda b, t: (b * n_t + t, 0))
    full = lambda a: pl.BlockSpec(a.shape, lambda b, t: (0,) * a.ndim)
    tiles = lambda r: pl.BlockSpec((None, None, r, tm), lambda b, t: (b, t, 0, 0))
    cols = lambda r: pl.BlockSpec((None, r, tm), lambda b, t: (b, 0, t))
    sd = jax.ShapeDtypeStruct
    tsd = lambda r, dt: sd((batch, n_t, r, tm), dt)
    out_shape = ([sd((n, D_ATT), F32)] * 2 + [sd((n, D_ATT), BF16)] * 2 + [sd((n, LANES), BF16)]
                 + [tsd(D_ATT, BF16)] * 3 + [tsd(H_IDX, F32)] + [tsd(D_ATT, BF16)] * 2
                 + [sd((batch, D_ATT, seq), F32)] * 4 + [sd((batch, D_IDX, seq), F32)])
    out_specs = ([row(D_ATT)] * 4 + [row(LANES)] + [tiles(D_ATT)] * 3 + [tiles(H_IDX)] + [tiles(D_ATT)] * 2
                 + [cols(D_ATT)] * 4 + [cols(D_IDX)])
    return pl.pallas_call(
        _proj_prompt_kernel,
        grid=(batch, n_t),
        in_specs=[row(D_MODEL), full(g), full(wn), full(wt), full(gains_t)],
        out_specs=out_specs, out_shape=out_shape,
        compiler_params=_params(2), name="proj_prompt",
    )(x, g, wn, wt, gains_t)


def _headnorm_rows(z, ones_ref, gain):
    ssq = _dot((z * z).astype(BF16), ones_ref[...])
    return z * lax.rsqrt(ssq * (1.0 / HEAD_DIM) + RMS_EPS) * gain


def _proj_sample_kernel(x_ref, g_ref, w_ref, gains_ref, ones_ref,
                        qa_ref, ka_ref, va_ref, qb_ref, kb_ref, vb_ref, qi_ref, wi_ref,
                        ga_ref, gb_ref, ka32_ref, va32_ref, kb32_ref, vb32_ref, ki32_ref):
    h = _rms(x_ref[...], g_ref[...]).astype(BF16)
    scale = HEAD_DIM ** -0.5

    def grp(c):
        return _dot(h, w_ref[:, c * D_ATT:(c + 1) * D_ATT])

    qa_ref[...] = (_headnorm_rows(grp(0), ones_ref, gains_ref[0:1, :]) * scale).astype(BF16)
    ka = _headnorm_rows(grp(1), ones_ref, gains_ref[1:2, :])
    ka32_ref[...] = ka
    ka_ref[...] = ka.astype(BF16)
    va = grp(2)
    va32_ref[...] = va
    va_ref[...] = va.astype(BF16)
    ga_ref[...] = grp(3)
    qb_ref[...] = (_headnorm_rows(grp(4), ones_ref, gains_ref[2:3, :]) * scale).astype(BF16)
    kb = _headnorm_rows(grp(5), ones_ref, gains_ref[3:4, :])
    kb32_ref[...] = kb
    kb_ref[...] = kb.astype(BF16)
    vb = grp(6)
    vb32_ref[...] = vb
    vb_ref[...] = vb.astype(BF16)
    gb_ref[...] = grp(7)
    qi_ref[...] = grp(8).astype(BF16)
    tail = _dot(h, w_ref[:, 9 * D_ATT:])
    ki32_ref[...] = tail[:, :D_IDX]
    wi_ref[...] = tail[:, LANES:]


def _project_sample(x, g, w, gains, ones_bd):
    n = x.shape[0]
    full = lambda a: pl.BlockSpec(a.shape, lambda i: (0,) * a.ndim)
    sd = lambda c, dt: jax.ShapeDtypeStruct((n, c), dt)
    out_shape = ([sd(D_ATT, BF16)] * 7 + [sd(LANES, F32)] + [sd(D_ATT, F32)] * 6 + [sd(D_IDX, F32)])
    return pl.pallas_call(
        _proj_sample_kernel,
        grid=(1,),
        in_specs=[full(x), full(g), full(w), full(gains), full(ones_bd)],
        out_specs=[full(s) for s in out_shape], out_shape=out_shape,
        compiler_params=_params(1), name="proj_sample",
    )(x, g, w, gains, ones_bd)


def _head_row_masks():
    r = lax.broadcasted_iota(I32, (LANES, 1), 0)
    return r < HEAD_DIM, r >= HEAD_DIM


def _pair_attention(i, qms, k_tile, v_tile, near_bias, mask_own, mask_past):
    def update(tiles, carry):
        raw = lambda half, j, bias: _dot(k_tile(j), qms[half]) + (bias(half) if bias else 0.0)
        scores = [[mask(half, j, raw(half, j, bias)) for j, bias, mask in tiles] for half in range(2)]
        out = []
        for half in range(2):
            pvs = [lambda p, j=j, half=half: _dot(v_tile(j, half), p) for j, _, _ in tiles]
            out.append(_osm_update(scores[half], pvs, *carry[half], axis=0, exp=jnp.exp2))
        return tuple(out)

    init = (jnp.full((1, TQ), NEG_INF, F32), jnp.zeros((1, TQ), F32), jnp.zeros((HEAD_DIM, TQ), F32))
    tiles = [(i, lambda half: near_bias(half, 0), lambda half, j, s: mask_own(half, s))]
    for d in range(1, 2 + FIRST_FAR):
        exists = i >= d
        tiles.append((jnp.maximum(i - d, 0), (lambda half: near_bias(half, 1)) if d == 1 else None,
                      lambda half, j, s, exists=exists: mask_past(half, j, s, exists)))
    carry = update(tiles, (init, init))
    n_far = jnp.maximum(i - 1 - FIRST_FAR, 0)
    done = 0
    for width in (4, 2, 1):
        n_w = (n_far - done) // width
        carry = lax.fori_loop(
            0, n_w,
            lambda u, c, done=done, width=width: update(
                [(done + width * u + w, None, lambda half, j, s: mask_past(half, j, s, True))
                 for w in range(width)], c),
            carry)
        done = done + n_w * width
    return [c[2] / c[1] for c in carry]


def _moba_prompt_kernel(qt_ref, k_ref, vt_ref, bias_ref, o_ref, kmean_ref, *, n_blk):
    i = pl.program_id(2)
    n_sub = kmean_ref.shape[0]

    @pl.when(i == 0)
    def _():
        kmean_ref[...] = jnp.zeros_like(kmean_ref)
        for j in range(n_blk):
            kb = k_ref[j * TQ:(j + 1) * TQ, :].astype(F32)
            kmean_ref[j:j + 1, :] = jnp.mean(kb, axis=0, keepdims=True)

    masks = _head_row_masks()
    qt = qt_ref[...]
    key = lax.broadcasted_iota(I32, (TQ, TQ), 0)
    qry = lax.broadcasted_iota(I32, (TQ, TQ), 1)
    blk = lax.broadcasted_iota(I32, (n_sub, TQ), 0)
    kmean = kmean_ref[...].astype(BF16)
    qms = [jnp.where(masks[half], qt, jnp.zeros_like(qt)) for half in range(2)]
    picks = [_top_blocks(jnp.where(blk < i, _dot(kmean, qm), NEG_INF), blk, axis=0) for qm in qms]

    def picked(half, j, s, exists):
        sel = (picks[half][0] == j) | (picks[half][1] == j) | (picks[half][2] == j)
        return jnp.where(sel if exists is True else sel & exists, s, NEG_INF)

    outs = _pair_attention(
        i, qms,
        k_tile=lambda j: k_ref[pl.ds(pl.multiple_of(j * TQ, TQ), TQ), :],
        v_tile=lambda j, half: vt_ref[j, half * HEAD_DIM:(half + 1) * HEAD_DIM, :],
        near_bias=lambda half, d: bias_ref[half, d],
        mask_own=lambda half, s: jnp.where(key <= qry, s, NEG_INF),
        mask_past=picked)
    o_ref[...] = jnp.concatenate(outs, axis=0).T


def _moba_prompt(qt, k, vt, bias, batch, seq):
    n_q = seq // TQ
    n_sub = -(-n_q // SUBLANES) * SUBLANES
    kern = functools.partial(_moba_prompt_kernel, n_blk=n_q)
    return pl.pallas_call(
        kern,
        grid=(batch, N_HEADS // 2, n_q),
        in_specs=[
            pl.BlockSpec((None, None, LANES, TQ), lambda b, hp, i: (b, i, hp, 0)),
            pl.BlockSpec((seq, LANES), lambda b, hp, i: (b, hp)),
            pl.BlockSpec((None, n_q, LANES, TQ), lambda b, hp, i: (b, 0, hp, 0)),
            pl.BlockSpec((2, 2, TQ, TQ), lambda b, hp, i: (hp, 0, 0, 0)),
        ],
        out_specs=pl.BlockSpec((TQ, LANES), lambda b, hp, i: (b * n_q + i, hp)),
        out_shape=jax.ShapeDtypeStruct((batch * seq, D_ATT), F32),
        scratch_shapes=[pltpu.VMEM((n_sub, LANES), F32)],
        compiler_params=_params(3), name="moba_prompt",
    )(qt, k, vt, bias)


def _dsa_prompt_kernel(qt_ref, qit_ref, wit_ref, kk_ref, k_ref, vt_ref, bias_ref, o_ref,
                       keys_ref, pen_ref, *, seq, k_top):
    i = pl.program_id(1)
    masks = _head_row_masks()
    key = lax.broadcasted_iota(I32, (TQ, TQ), 0)
    qry = lax.broadcasted_iota(I32, (TQ, TQ), 1)
    zero_b = jnp.zeros((LANES, TQ), BF16)

    def head_qt(ref, h):
        pair, half = divmod(h, 2)
        return jnp.where(masks[half], ref[pair * LANES:(pair + 1) * LANES, :], zero_b)

    def index_tile(j):
        kkj = kk_ref[pl.ds(pl.multiple_of(j * TQ, TQ), TQ), :]
        acc = jnp.zeros((TQ, TQ), F32)
        for h in range(H_IDX):
            acc = acc + wit_ref[h:h + 1, :] * jnp.maximum(_dot(kkj, head_qt(qit_ref, h)), 0.0)
        return _f32_key(acc)

    def index_pair(u, c):
        keys_ref[2 * u] = index_tile(2 * u)
        keys_ref[2 * u + 1] = index_tile(2 * u + 1)
        return c

    def index_one(j, c):
        keys_ref[j] = index_tile(j)
        return c

    lax.fori_loop(0, i // 2, index_pair, 0)
    lax.fori_loop(2 * (i // 2), i, index_one, 0)
    keys_ref[i] = jnp.where(key <= qry, index_tile(i), INT_MIN)

    def count_fn(pred):
        def one(j):
            w = jnp.where(pred(keys_ref[j], key + j * TQ), 1.0, 0.0)
            return jnp.sum(w.reshape(TQ // SUBLANES, SUBLANES, TQ), axis=0)
        n_two = (i + 1) // 2
        part = lax.fori_loop(0, n_two, lambda u, p: p + one(2 * u) + one(2 * u + 1),
                             jnp.zeros((SUBLANES, TQ), F32))
        part = lax.fori_loop(2 * n_two, i + 1, lambda j, p: p + one(j), part)
        return jnp.sum(part, axis=0, keepdims=True)

    t, x = _select_threshold(count_fn, k_top, seq)

    def pen_body(j, c):
        kt = keys_ref[j]
        sel = (kt > t) | ((kt == t) & (key + j * TQ <= x))
        pen_ref[j] = jnp.where(sel, 0.0, NEG_INF)
        return c

    lax.fori_loop(0, i + 1, pen_body, 0)
    outs = []
    for pair in range(N_HEADS // 2):
        lanes = slice(pair * LANES, (pair + 1) * LANES)
        vrows = lambda half, pair=pair: slice((2 * pair + half) * HEAD_DIM, (2 * pair + half + 1) * HEAD_DIM)
        outs += _pair_attention(
            i, [head_qt(qt_ref, 2 * pair + half) for half in range(2)],
            k_tile=lambda j, lanes=lanes: k_ref[pl.ds(pl.multiple_of(j * TQ, TQ), TQ), lanes],
            v_tile=lambda j, half, vrows=vrows: vt_ref[j, vrows(half), :],
            near_bias=lambda half, d, pair=pair: bias_ref[2 * pair + half, d],
            mask_own=lambda half, s: s + pen_ref[i],
            mask_past=lambda half, j, s, exists: (
                s + pen_ref[j] if exists is True else jnp.where(exists, s + pen_ref[j], NEG_INF)))
    o_ref[...] = jnp.concatenate(outs, axis=0).T


def _dsa_prompt(qt, qit, wit, kk, k, vt, bias, batch, seq):
    n_q = seq // TQ
    k_top = min(DSA_TOPK_MAX, seq // 4)
    kern = functools.partial(_dsa_prompt_kernel, seq=seq, k_top=k_top)
    tile = lambda r: pl.BlockSpec((None, None, r, TQ), lambda b, i: (b, i, 0, 0))
    rows = lambda c: pl.BlockSpec((seq, c), lambda b, i: (b, 0))
    return pl.pallas_call(
        kern,
        grid=(batch, n_q),
        in_specs=[
            tile(D_ATT), tile(D_ATT), tile(H_IDX),
            rows(LANES), rows(D_ATT),
            pl.BlockSpec((None, n_q, D_ATT, TQ), lambda b, i: (b, 0, 0, 0)),
            pl.BlockSpec((N_HEADS, 2, TQ, TQ), lambda b, i: (1, 0, 0, 0)),
        ],
        out_specs=pl.BlockSpec((TQ, D_ATT), lambda b, i: (b * n_q + i, 0)),
        out_shape=jax.ShapeDtypeStruct((batch * seq, D_ATT), F32),
        scratch_shapes=[pltpu.VMEM((n_q, TQ, TQ), I32), pltpu.VMEM((n_q, TQ, TQ), F32)],
        compiler_params=_params(2), name="dsa_prompt",
    )(qt, qit, wit, kk, k, vt, bias)


def _expand_tokens(a, n_tok):
    return jnp.broadcast_to(a[:, None, :], (n_tok, N_HEADS, a.shape[-1])).reshape(n_tok * N_HEADS, a.shape[-1])


def _collapse_heads(o, hmask, n_tok):
    return jnp.sum((o * hmask).reshape(n_tok, N_HEADS, D_ATT), axis=1)


def _sample_index_kernel(pt_ref, *refs, n_pages, n_tok, k_top, n_pp):
    kit_refs = refs[:n_pp]
    qi_ref, wcol_ref, knew_ref, pen_ref, keys_ref = refs[n_pp:]
    g = pl.program_id(1)

    def keys_of(kt):
        x = jnp.maximum(_dot(qi_ref[...], kt), 0.0) * wcol_ref[...]
        return _f32_key(jnp.sum(x.reshape(n_tok, H_IDX, kt.shape[1]), axis=1))

    keys = keys_of(jnp.concatenate([r[...] for r in kit_refs], axis=1).astype(BF16))
    for o in range(n_pp):
        keys_ref[g * n_pp + o] = keys[:, o * LANES:(o + 1) * LANES]

    @pl.when(g == n_pages // n_pp - 1)
    def _():
        lane = lax.broadcasted_iota(I32, (n_tok, LANES), 1)
        tok = lax.broadcasted_iota(I32, (n_tok, LANES), 0)
        keys_ref[n_pages] = jnp.where(lane <= tok, keys_of(knew_ref[...]), INT_MIN)
        shape = (n_pages + 1, n_tok, LANES)
        pos = lax.broadcasted_iota(I32, shape, 0) * LANES + lax.broadcasted_iota(I32, shape, 2)
        all_keys = keys_ref[...]

        def count_fn(pred):
            w = jnp.where(pred(all_keys, pos), 1.0, 0.0)
            terms = [w[j] for j in range(n_pages + 1)]
            while len(terms) > 1:
                terms = [a + b for a, b in zip(terms[::2], terms[1::2])] + terms[len(terms) & ~1:]
            return jnp.sum(terms[0], axis=1, keepdims=True)

        t, x = _select_threshold(count_fn, k_top, (n_pages + 1) * LANES)
        sel = (all_keys > t) | ((all_keys == t) & (pos <= x))
        pen_ref[...] = jnp.where(sel, 0.0, NEG_INF)


def _sample_index(page_table, kit, qi, wcol, knew, n_tok, k_top):
    n_seq, n_pages = page_table.shape
    n_pp = math.gcd(IDX_PAGES_PER_STEP, n_pages)
    kern = functools.partial(_sample_index_kernel, n_pages=n_pages, n_tok=n_tok, k_top=k_top, n_pp=n_pp)
    per_seq = lambda a: pl.BlockSpec((None,) + a.shape[1:], lambda b, g, pt: (b,) + (0,) * (a.ndim - 1))
    page = lambda o: pl.BlockSpec((None, D_IDX, PAGE_SIZE), lambda b, g, pt: (pt[b, g * n_pp + o], 0, 0))
    grid_spec = pltpu.PrefetchScalarGridSpec(
        num_scalar_prefetch=1, grid=(n_seq, n_pages // n_pp),
        in_specs=[page(o) for o in range(n_pp)] + [per_seq(qi), per_seq(wcol), per_seq(knew)],
        out_specs=pl.BlockSpec((None, n_pages + 1, n_tok, LANES), lambda b, g, pt: (b, 0, 0, 0)),
        scratch_shapes=[pltpu.VMEM((n_pages + 1, n_tok, LANES), I32)])
    return pl.pallas_call(
        kern, grid_spec=grid_spec,
        out_shape=jax.ShapeDtypeStruct((n_seq, n_pages + 1, n_tok, LANES), F32),
        compiler_params=_params(2), name="sample_index",
    )(page_table, *([kit] * n_pp), qi, wcol, knew)


def _sample_dsa_kernel(pt_ref, *refs, n_pages, n_tok, n_pp):
    kt_refs, vt_refs = refs[:n_pp], refs[n_pp:2 * n_pp]
    (q_ref, pen_ref, blast_ref, c31_ref, knew_ref, vnew_ref, bnew_ref, hmask_ref,
     o_ref, m_ref, l_ref, acc_ref) = refs[2 * n_pp:]
    g = pl.program_id(1)
    last = g == n_pages // n_pp - 1

    @pl.when(g == 0)
    def _():
        m_ref[...] = jnp.full(m_ref.shape, NEG_INF, F32)
        l_ref[...] = jnp.zeros(l_ref.shape, F32)
        acc_ref[...] = jnp.zeros(acc_ref.shape, F32)

    q = q_ref[...]
    scores, pvs = [], []
    for o in range(n_pp):
        bias = c31_ref[...] if o < n_pp - 1 else jnp.where(last, blast_ref[...], c31_ref[...])
        scores.append(_dot(q, kt_refs[o][...].astype(BF16)) + bias + _expand_tokens(pen_ref[g * n_pp + o], n_tok))
        pvs.append(lambda pp, o=o: _dot_nt(pp, vt_refs[o][...].astype(BF16)))
    m, l, acc = _osm_update(scores, pvs, m_ref[...], l_ref[...], acc_ref[...], axis=1)
    m_ref[...] = m
    l_ref[...] = l
    acc_ref[...] = acc

    @pl.when(last)
    def _():
        pen_new = _expand_tokens(pen_ref[n_pages][:, :NEW_PAD], n_tok)
        s_n = _dot_nt(q, knew_ref[...]) + bnew_ref[...] + pen_new
        _, l2, acc2 = _osm_update([s_n], [lambda pp: _dot(pp, vnew_ref[...])], m, l, acc, axis=1)
        o_ref[...] = _collapse_heads(acc2 / l2, hmask_ref[...], n_tok)


def _sample_dsa(page_table, kt, vt, q, pen, blast, c31col, knew, vnew, bnew, hmask, n_tok):
    n_seq, n_pages = page_table.shape
    n_pp = math.gcd(PAGES_PER_STEP, n_pages)
    rows = n_tok * N_HEADS
    kern = functools.partial(_sample_dsa_kernel, n_pages=n_pages, n_tok=n_tok, n_pp=n_pp)
    per_seq = lambda a: pl.BlockSpec((None,) + a.shape[1:], lambda b, g, pt: (b,) + (0,) * (a.ndim - 1))
    full = lambda a: pl.BlockSpec(a.shape, lambda b, g, pt: (0,) * a.ndim)
    page = lambda o: pl.BlockSpec((None, D_ATT, PAGE_SIZE), lambda b, g, pt: (pt[b, g * n_pp + o], 0, 0))
    pages = [page(o) for o in range(n_pp)]
    grid_spec = pltpu.PrefetchScalarGridSpec(
        num_scalar_prefetch=1, grid=(n_seq, n_pages // n_pp),
        in_specs=pages + pages + [per_seq(q), per_seq(pen), full(blast), full(c31col),
                                  per_seq(knew), per_seq(vnew), full(bnew), full(hmask)],
        out_specs=pl.BlockSpec((None, n_tok, D_ATT), lambda b, g, pt: (b, 0, 0)),
        scratch_shapes=[pltpu.VMEM((rows, 1), F32), pltpu.VMEM((rows, 1), F32), pltpu.VMEM((rows, D_ATT), F32)])
    return pl.pallas_call(
        kern, grid_spec=grid_spec,
        out_shape=jax.ShapeDtypeStruct((n_seq, n_tok, D_ATT), F32),
        compiler_params=_params(2), name="sample_dsa",
    )(page_table, *([kt] * n_pp), *([vt] * n_pp), q, pen, blast, c31col, knew, vnew, bnew, hmask)


def _sample_moba_kernel(pt_ref, *refs, n_blk, n_tok, n_pp):
    kt_refs, vt_refs = refs[:n_pp], refs[n_pp:2 * n_pp]
    (q_ref, blast_ref, c31_ref, knew_ref, vnew_ref, bnew_ref, hmask_ref,
     o_ref, gate_ref, ms_ref, ls_ref, accs_ref) = refs[2 * n_pp:]
    g = pl.program_id(1)
    n_bb = n_pp // 2
    last = g == n_blk // n_bb - 1
    rows = n_tok * N_HEADS
    lane = lax.broadcasted_iota(I32, (rows, LANES), 1)

    @pl.when(g == 0)
    def _():
        gate_ref[...] = jnp.full(gate_ref.shape, NEG_INF, F32)
        ms_ref[...] = jnp.full(ms_ref.shape, NEG_INF, F32)
        ls_ref[...] = jnp.zeros(ls_ref.shape, F32)

    q = q_ref[...]
    gates, ms, ls = gate_ref[...], ms_ref[...], ls_ref[...]
    raw = [_dot(q, r[...].astype(BF16)) for r in kt_refs]
    parts = []
    for bb in range(n_bb):
        r0, r1 = raw[2 * bb], raw[2 * bb + 1]
        gate = jnp.sum(r0 + r1, axis=1, keepdims=True)
        s0 = r0 + c31_ref[...]
        s1 = r1 + (c31_ref[...] if bb < n_bb - 1 else jnp.where(last, blast_ref[...], c31_ref[...]))
        m = jnp.maximum(jnp.max(s0, axis=1, keepdims=True), jnp.max(s1, axis=1, keepdims=True))
        p0 = jnp.exp(s0 - m)
        p1 = jnp.exp(s1 - m)
        l = jnp.sum(p0, axis=1, keepdims=True) + jnp.sum(p1, axis=1, keepdims=True)
        parts.append((gate, m, l, p0.astype(BF16), p1.astype(BF16)))
    for bb, (gate, m, l, p0, p1) in enumerate(parts):
        j = g * n_bb + bb
        accs_ref[j] = (_dot_nt(p0, vt_refs[2 * bb][...].astype(BF16))
                       + _dot_nt(p1, vt_refs[2 * bb + 1][...].astype(BF16)))
        gates = jnp.where(lane == j, gate, gates)
        ms = jnp.where(lane == j, m, ms)
        ls = jnp.where(lane == j, l, ls)
    gate_ref[...] = gates
    ms_ref[...] = ms
    ls_ref[...] = ls

    @pl.when(last)
    def _():
        picks = _top_blocks(jnp.where(lane < n_blk, gates, NEG_INF), lane, axis=1)
        sel = (lane == picks[0]) | (lane == picks[1]) | (lane == picks[2])
        s_n = _dot_nt(q, knew_ref[...]) + bnew_ref[...]
        m_o = jnp.max(s_n, axis=1, keepdims=True)
        p_o = jnp.exp(s_n - m_o)
        l_o = jnp.sum(p_o, axis=1, keepdims=True)
        m_all = jnp.maximum(m_o, jnp.max(jnp.where(sel, ms, NEG_INF), axis=1, keepdims=True))
        w = jnp.where(sel, jnp.exp(ms - m_all), 0.0)
        w_o = jnp.exp(m_o - m_all)
        l_all = jnp.sum(w * ls, axis=1, keepdims=True) + w_o * l_o
        out = w_o * _dot(p_o.astype(BF16), vnew_ref[...])
        for jj in range(n_blk):
            out = out + w[:, jj:jj + 1] * accs_ref[jj]
        o_ref[...] = _collapse_heads(out / l_all, hmask_ref[...], n_tok)


def _sample_moba(page_table, kt, vt, q, blast, c31col, knew, vnew, bnew, hmask, n_tok):
    n_seq, n_pages = page_table.shape
    pages_per_blk = MOBA_BLOCK // PAGE_SIZE
    n_blk = n_pages // pages_per_blk
    n_pp = pages_per_blk * math.gcd(PAGES_PER_STEP // pages_per_blk, n_blk)
    rows = n_tok * N_HEADS
    kern = functools.partial(_sample_moba_kernel, n_blk=n_blk, n_tok=n_tok, n_pp=n_pp)
    per_seq = lambda a: pl.BlockSpec((None,) + a.shape[1:], lambda b, g, pt: (b,) + (0,) * (a.ndim - 1))
    full = lambda a: pl.BlockSpec(a.shape, lambda b, g, pt: (0,) * a.ndim)
    page = lambda o: pl.BlockSpec((None, D_ATT, PAGE_SIZE), lambda b, g, pt: (pt[b, g * n_pp + o], 0, 0))
    pages = [page(o) for o in range(n_pp)]
    grid_spec = pltpu.PrefetchScalarGridSpec(
        num_scalar_prefetch=1, grid=(n_seq, n_pages // n_pp),
        in_specs=pages + pages + [per_seq(q), full(blast), full(c31col),
                                  per_seq(knew), per_seq(vnew), full(bnew), full(hmask)],
        out_specs=pl.BlockSpec((None, n_tok, D_ATT), lambda b, g, pt: (b, 0, 0)),
        scratch_shapes=[pltpu.VMEM((rows, LANES), F32)] * 3 + [pltpu.VMEM((n_blk, rows, D_ATT), F32)])
    return pl.pallas_call(
        kern, grid_spec=grid_spec,
        out_shape=jax.ShapeDtypeStruct((n_seq, n_tok, D_ATT), F32),
        compiler_params=_params(2), name="sample_moba",
    )(page_table, *([kt] * n_pp), *([vt] * n_pp), q, blast, c31col, knew, vnew, bnew, hmask)


def _merge_and_conv_in(x_ref, oa_ref, ob_ref, ga_ref, gb_ref, wo_ref, g1_ref, wc_ref):
    o = jnp.concatenate([oa_ref[...] * _silu(ga_ref[...]), ob_ref[...] * _silu(gb_ref[...])], axis=1)
    h1 = x_ref[...] + _dot(o.astype(BF16), wo_ref[...])
    hn = _rms(h1, g1_ref[...]).astype(BF16)
    a = _dot(hn, wc_ref[:, 0:D_MODEL])
    b = _dot(hn, wc_ref[:, D_MODEL:2 * D_MODEL])
    g = _dot(hn, wc_ref[:, 2 * D_MODEL:3 * D_MODEL])
    return h1, a * _sigmoid(b), g


def _ln_gate_out(c, g, h1, lg_ref, lb_ref, wo2_ref):
    mu = jnp.mean(c, axis=-1, keepdims=True)
    d = c - mu
    var = jnp.mean(d * d, axis=-1, keepdims=True)
    cn = d * lax.rsqrt(var + LN_EPS) * lg_ref[...] + lb_ref[...]
    z = (_silu(cn) * _silu(g)).astype(BF16)
    return h1 + _dot(z, wo2_ref[...])


def _layer1_prompt_kernel(x_ref, oa_ref, ob_ref, ga_ref, gb_ref, wo_ref, g1_ref, wc_ref,
                          cw_ref, cb_ref, lg_ref, lb_ref, wo2_ref, y_ref, st_ref, up_ref, c_ref, *, tt):
    t = pl.program_id(1)

    @pl.when(t == 0)
    def _():
        up_ref[0:HIST, :] = jnp.zeros((HIST, D_MODEL), F32)

    h1, u, g = _merge_and_conv_in(x_ref, oa_ref, ob_ref, ga_ref, gb_ref, wo_ref, g1_ref, wc_ref)
    up_ref[HIST:HIST + tt, :] = u
    base = HIST - (CONV_WIDTH - 1)
    rows = min(tt, CONV_ROWS)
    for cc in range(0, D_MODEL, LANES):
        cs = slice(cc, cc + LANES)
        for r0 in range(0, tt, rows):
            acc = jnp.zeros((rows, LANES), F32) + cb_ref[:, cs]
            for phase in range(SUBLANES):
                taps = [w for w in range(CONV_WIDTH) if (base + w) % SUBLANES == phase]
                n_win = rows + (SUBLANES if phase else 0)
                group = None
                for w in taps:
                    start = r0 + base + w - phase
                    term = cw_ref[w:w + 1, cs] * up_ref[start:start + n_win, cs]
                    group = term if group is None else group + term
                if group is not None:
                    acc = acc + group[phase:phase + rows, :]
            c_ref[r0:r0 + rows, cs] = acc
    y_ref[...] = _ln_gate_out(c_ref[...], g, h1, lg_ref, lb_ref, wo2_ref)
    tail = up_ref[tt:tt + HIST, :]
    st_ref[...] = tail
    up_ref[0:HIST, :] = tail


def _layer1_prompt(x, oa, ob, ga, gb, wo, g1, wc, cw, cb, lg, lb, wo2, batch, seq, tt):
    n_t = seq // tt
    kern = functools.partial(_layer1_prompt_kernel, tt=tt)
    row = lambda c: pl.BlockSpec((tt, c), lambda b, t: (b * n_t + t, 0))
    full = lambda a: pl.BlockSpec(a.shape, lambda b, t: (0,) * a.ndim)
    return pl.pallas_call(
        kern,
        grid=(batch, n_t),
        in_specs=[row(D_MODEL), row(D_ATT), row(D_ATT), row(D_ATT), row(D_ATT),
                  full(wo), full(g1), full(wc), full(cw), full(cb), full(lg), full(lb), full(wo2)],
        out_specs=[row(D_MODEL), pl.BlockSpec((None, HIST, D_MODEL), lambda b, t: (b, 0, 0))],
        out_shape=[jax.ShapeDtypeStruct((batch * seq, D_MODEL), F32),
                   jax.ShapeDtypeStruct((batch, HIST, D_MODEL), F32)],
        scratch_shapes=[pltpu.VMEM((HIST + tt, D_MODEL), F32), pltpu.VMEM((tt, D_MODEL), F32)],
        compiler_params=_params(2), name="layer1_prompt",
    )(x, oa, ob, ga, gb, wo, g1, wc, cw, cb, lg, lb, wo2)


def _layer1_sample_kernel(x_ref, oa_ref, ob_ref, ga_ref, gb_ref, wo_ref, g1_ref, wc_ref,
                          cw_ref, cb_ref, lg_ref, lb_ref, wo2_ref, st_ref, y_ref, ns_ref, c_ref, *, n_tok, n_seq):
    n_hist = CONV_WIDTH - 1
    h1, u, g = _merge_and_conv_in(x_ref, oa_ref, ob_ref, ga_ref, gb_ref, wo_ref, g1_ref, wc_ref)
    u_t = [u[t * n_seq:(t + 1) * n_seq, :] for t in range(n_tok)]
    for t in range(n_tok):
        acc = jnp.zeros((n_seq, D_MODEL), F32) + cb_ref[...]
        for r in range(t, n_hist):
            acc = acc + cw_ref[r - t:r - t + 1, :] * st_ref[r]
        for r in range(t + 1):
            acc = acc + cw_ref[n_hist - t + r:n_hist - t + r + 1, :] * u_t[r]
        c_ref[t * n_seq:(t + 1) * n_seq, :] = acc
    y_ref[...] = _ln_gate_out(c_ref[...], g, h1, lg_ref, lb_ref, wo2_ref)
    for r in range(n_hist - n_tok):
        ns_ref[r] = st_ref[r + n_tok]
    for t in range(n_tok):
        ns_ref[n_hist - n_tok + t] = u_t[t]


def _layer1_sample(x, oa, ob, ga, gb, wo, g1, wc, cw, cb, lg, lb, wo2, st, n_tok, n_seq):
    kern = functools.partial(_layer1_sample_kernel, n_tok=n_tok, n_seq=n_seq)
    full = lambda a: pl.BlockSpec(a.shape, lambda i: (0,) * a.ndim)
    args = (x, oa, ob, ga, gb, wo, g1, wc, cw, cb, lg, lb, wo2, st)
    out_shape = [jax.ShapeDtypeStruct(x.shape, F32), jax.ShapeDtypeStruct(st.shape, F32)]
    return pl.pallas_call(
        kern, grid=(1,),
        in_specs=[full(a) for a in args],
        out_specs=[full(s) for s in out_shape], out_shape=out_shape,
        scratch_shapes=[pltpu.VMEM(x.shape, F32)],
        compiler_params=_params(1), name="layer1_sample",
    )(*args)


def _t5_bucket(n):
    exact = N_BUCKETS // 2
    nf = jnp.maximum(n, 1).astype(F32)
    large = exact + (jnp.log(nf / exact) / math.log(MAX_DISTANCE / exact) * (N_BUCKETS - exact)).astype(I32)
    large = jnp.minimum(large, N_BUCKETS - 1)
    return jnp.where(n < exact, n, large)


def _bias_by_distance(table, n):
    return table[_t5_bucket(jnp.arange(n, dtype=I32))]


def _prep_attn_weights(w_in, qn_a, kn_a, qn_b, kn_b):
    grp = lambda c: w_in[:, c * D_ATT:(c + 1) * D_ATT]
    ki = w_in[:, 9 * D_ATT:9 * D_ATT + D_IDX]
    wi = w_in[:, 9 * D_ATT + D_IDX:]
    zeros = lambda c: jnp.zeros((D_MODEL, c), w_in.dtype)
    tile = lambda g: jnp.tile(g, N_HEADS)
    head = np.arange(D_ATT) // HEAD_DIM
    ones_bd = jnp.asarray(head[:, None] == head[None, :], dtype=BF16)
    wn = jnp.concatenate([grp(3), grp(7)], axis=1).astype(BF16)
    wt = jnp.concatenate([grp(0), grp(1), grp(2), grp(4), grp(5), grp(6), grp(8), ki, ki, wi,
                          zeros(NEW_PAD - H_IDX)], axis=1).T.astype(BF16)
    gains_t = jnp.stack([tile(qn_a), tile(kn_a), tile(qn_b), tile(kn_b)]).astype(F32)[:, :, None]
    ws = jnp.concatenate([w_in[:, :9 * D_ATT + D_IDX], zeros(LANES - D_IDX), wi, zeros(LANES - H_IDX)],
                         axis=1).astype(BF16)
    gains_s = jnp.stack([tile(qn_a), tile(kn_a), tile(qn_b), tile(kn_b)]).astype(F32)
    return wn, wt, gains_t, ws, gains_s, ones_bd


def kernel(x_prompt, x_sample, cache_k_a, cache_v_a, cache_k_b, cache_v_b, cache_kidx_b, state_conv, page_table,
           norm_g, rel_bias_table, w_in_attn, q_norm_a, k_norm_a, q_norm_b, k_norm_b, w_out_attn,
           w_in_conv, conv_w, conv_b, conv_ln_g, conv_ln_b, w_out_conv):
    batch, seq, _ = x_prompt.shape
    n_seq, n_tok, _ = x_sample.shape
    n_pages = page_table.shape[1]
    past = n_pages * PAGE_SIZE
    assert seq % TQ == 0 and past % MOBA_BLOCK == 0 and n_tok <= NEW_PAD and TQ >= MAX_DISTANCE
    n_hist = CONV_WIDTH - 1
    wn, wt, gains_t, ws, gains_s, ones_bd = _prep_attn_weights(
        w_in_attn[0], q_norm_a[0], k_norm_a[0], q_norm_b[0], k_norm_b[0])
    g0 = norm_g[0][None, :]
    g1 = norm_g[1][None, :]
    wo = w_out_attn[0].astype(BF16)
    wc = w_in_conv[0].astype(BF16)
    wo2 = w_out_conv[0].astype(BF16)
    cw = jnp.concatenate([conv_w[0], jnp.zeros((HIST - CONV_WIDTH, D_MODEL), F32)], axis=0)
    conv_rows = (cw, conv_b[0][None, :], conv_ln_g[0][None, :], conv_ln_b[0][None, :])
    c31 = rel_bias_table[N_BUCKETS - 1]
    bias_tiles = _bias_tiles(rel_bias_table)

    xp = x_prompt.reshape(batch * seq, D_MODEL)
    (ga, gb, katm, kbtm, kktm, qat, qbt, qit, wit, vat, vbt, ka32, va32, kb32, vb32, ki32) = _project_prompt(
        xp, g0, wn, wt, gains_t, batch, seq)
    oa = _moba_prompt(qat, katm, vat, bias_tiles, batch, seq)
    ob = _dsa_prompt(qbt, qit, wit, kktm, kbtm, vbt, bias_tiles, batch, seq)
    y_p, st_p = _layer1_prompt(xp, oa, ob, ga, gb, wo, g1, wc, *conv_rows, wo2, batch, seq, tt=TQ)
    heads_p = lambda a: jnp.transpose(a.reshape(batch, N_HEADS, HEAD_DIM, seq), (0, 3, 1, 2))[None]
    ki_p = jnp.transpose(ki32, (0, 2, 1))[None]
    cv_p = st_p[None, :, HIST - n_hist:, :]

    rows = n_tok * N_HEADS
    xs = x_sample.reshape(n_seq * n_tok, D_MODEL)
    (qa_s, ka_s, va_s, qb_s, kb_s, vb_s, qi_s, wi_s, ga_s, gb_s, ka32_s, va32_s, kb32_s, vb32_s, ki32_s) = \
        _project_sample(xs, g0, ws, gains_s, ones_bd)
    head_of_lane = np.arange(D_ATT) // HEAD_DIM
    hmask8 = (np.arange(N_HEADS)[:, None] == head_of_lane[None, :])
    hmask = jnp.asarray(np.tile(hmask8, (n_tok, 1)), F32)
    qbd = lambda q: (q.reshape(n_seq, n_tok, 1, D_ATT) * jnp.asarray(hmask8, BF16)).reshape(n_seq, rows, D_ATT)
    newpad = lambda a: jnp.pad(a.reshape(n_seq, n_tok, D_ATT), ((0, 0), (0, NEW_PAD - n_tok), (0, 0)))
    bd = _bias_by_distance(rel_bias_table, 2 * PAGE_SIZE)
    t_of_row = np.arange(rows) // N_HEADS
    h_of_row = np.arange(rows) % N_HEADS
    c_idx = np.arange(PAGE_SIZE)
    dist_last = PAGE_SIZE + t_of_row[:, None] - c_idx[None, :]
    tn = np.arange(NEW_PAD)
    dist_new = np.maximum(t_of_row[:, None] - tn[None, :], 0)
    valid_new = (tn[None, :] <= t_of_row[:, None]) & (tn[None, :] < n_tok)

    def sample_bias(off):
        hh = h_of_row + off
        blast = bd[dist_last, hh[:, None]]
        bnew = jnp.where(valid_new, bd[dist_new, hh[:, None]], NEG_INF)
        return blast, c31[hh][:, None], bnew

    pages = lambda c: jnp.transpose(c[0], (0, 2, 3, 1)).reshape(c.shape[1], D_ATT, PAGE_SIZE)
    kit = jnp.transpose(cache_kidx_b[0], (0, 2, 1))
    blast_a, c31_a, bnew_a = sample_bias(0)
    oa_s = _sample_moba(page_table, pages(cache_k_a), pages(cache_v_a), qbd(qa_s), blast_a, c31_a,
                        newpad(ka_s), newpad(va_s), bnew_a, hmask, n_tok)
    k_top = min(DSA_TOPK_MAX, (past + n_tok) // 4)
    knew_i = jnp.pad(jnp.transpose(ki32_s.reshape(n_seq, n_tok, D_IDX), (0, 2, 1)),
                     ((0, 0), (0, 0), (0, PAGE_SIZE - n_tok))).astype(BF16)
    pen = _sample_index(page_table, kit, qi_s.reshape(n_seq, rows, D_IDX),
                        wi_s[:, :H_IDX].reshape(n_seq, rows, 1), knew_i, n_tok, k_top)
    blast_b, c31_b, bnew_b = sample_bias(N_HEADS)
    ob_s = _sample_dsa(page_table, pages(cache_k_b), pages(cache_v_b), qbd(qb_s), pen, blast_b, c31_b,
                       newpad(kb_s), newpad(vb_s), bnew_b, hmask, n_tok)
    tmaj = lambda a: jnp.transpose(a.reshape(n_seq, n_tok, -1), (1, 0, 2)).reshape(n_tok * n_seq, -1)
    st_in = jnp.transpose(state_conv[0], (1, 0, 2))
    y_s, ns = _layer1_sample(tmaj(xs), tmaj(oa_s), tmaj(ob_s), tmaj(ga_s), tmaj(gb_s), wo, g1, wc,
                             *conv_rows, wo2, st_in, n_tok, n_seq)
    y_s = jnp.transpose(y_s.reshape(n_tok, n_seq, D_MODEL), (1, 0, 2))
    cv_s = jnp.transpose(ns, (1, 0, 2))[None]
    heads_s = lambda a: a.reshape(1, n_seq, n_tok, N_HEADS, HEAD_DIM)

    return (y_p.reshape(batch, seq, D_MODEL), y_s,
            heads_p(ka32), heads_p(va32), heads_p(kb32), heads_p(vb32), ki_p, cv_p,
            heads_s(ka32_s), heads_s(va32_s), heads_s(kb32_s), heads_s(vb32_s),
            ki32_s.reshape(1, n_seq, n_tok, D_IDX), cv_s)
```

```python
import functools
import math

import numpy as np
import jax
import jax.numpy as jnp
from jax import lax
from jax.experimental import pallas as pl
from jax.experimental.pallas import tpu as pltpu

F32 = jnp.float32
BF16 = jnp.bfloat16
I32 = jnp.int32

D_MODEL = 1024
HEAD_DIM = 64
N_HEADS = 8
D_ATT = N_HEADS * HEAD_DIM
MOBA_BLOCK = 256
MOBA_TOPK = 3
DSA_TOPK_MAX = 256
H_IDX = 8
D_IDX = 64
CONV_WIDTH = 31
N_BUCKETS = 32
MAX_DISTANCE = 128
PAGE_SIZE = 128
RMS_EPS = 1e-6
LN_EPS = 1e-5

LANES = 128
SUBLANES = 8
TQ = MOBA_BLOCK
HIST = 32
NEW_PAD = 16
PAGES_PER_STEP = 32
IDX_PAGES_PER_STEP = 64
CONV_ROWS = 128
FIRST_FAR = 2
INT_MIN = -2 ** 31
NEG_INF = float("-inf")
LOG2E = math.log2(math.e)
VMEM_LIMIT = 56 * 1024 * 1024


def _dot(a, b):
    return jnp.dot(a, b, preferred_element_type=F32)


def _dot_nt(a, b):
    return lax.dot_general(a, b, (((1,), (1,)), ((), ())), preferred_element_type=F32)


def _sigmoid(x):
    return 1.0 / (1.0 + jnp.exp(-x))


def _silu(x):
    return x * _sigmoid(x)


def _rms(x, g):
    return x * lax.rsqrt(jnp.mean(x * x, axis=-1, keepdims=True) + RMS_EPS) * g


def _osm_update(scores, pvs, m, l, acc, axis, exp=jnp.exp):
    m_new = m
    for s in scores:
        m_new = jnp.maximum(m_new, jnp.max(s, axis=axis, keepdims=True))
    m_safe = jnp.where(m_new == NEG_INF, 0.0, m_new)
    alpha = exp(m - m_safe)
    l = alpha * l
    acc = alpha * acc
    for s, pv in zip(scores, pvs):
        p = exp(s - m_safe)
        l = l + jnp.sum(p, axis=axis, keepdims=True)
        acc = acc + pv(p.astype(BF16))
    return m_new, l, acc


def _top_blocks(g, idx, axis):
    picks = []
    for _ in range(MOBA_TOPK):
        mx = jnp.max(g, axis=axis, keepdims=True)
        a = jnp.min(jnp.where(g == mx, idx, LANES * LANES), axis=axis, keepdims=True)
        a = jnp.where(mx == NEG_INF, -1, a)
        picks.append(a)
        g = jnp.where(idx == a, NEG_INF, g)
    return picks


def _f32_key(x):
    b = lax.bitcast_convert_type(x, I32)
    k = b ^ ((b >> 31) & 0x7FFFFFFF)
    return jnp.where(x == 0.0, 0, k)


def _select_threshold(count_fn, k_top, n_keys):
    k_f = float(k_top)
    c0 = count_fn(lambda kt, pos: kt >= 0)
    t = jnp.where(c0 >= k_f, 0, INT_MIN).astype(I32)
    cnt_ge = jnp.where(c0 >= k_f, c0, 0.0)

    def bit_step(b, carry):
        t, cnt_ge = carry
        cand = t | jnp.left_shift(jnp.int32(1), 30 - b)
        cnt = count_fn(lambda kt, pos: kt >= cand)
        return jnp.where(cnt >= k_f, cand, t), jnp.where(cnt >= k_f, cnt, cnt_ge)

    t, cnt_ge = lax.fori_loop(0, 31, bit_step, (t, cnt_ge))
    t = jnp.maximum(t, INT_MIN + 1)
    n_bits = max(1, int(math.ceil(math.log2(n_keys))))

    def tie_search():
        need = k_f - count_fn(lambda kt, pos: kt > t)

        def pos_step(b, x):
            cand = x | jnp.left_shift(jnp.int32(1), n_bits - 1 - b)
            below = count_fn(lambda kt, pos: (kt == t) & (pos < cand))
            return jnp.where(below < need, cand, x)
        return lax.fori_loop(0, n_bits, pos_step, jnp.zeros_like(t))

    any_excess = jnp.max(jnp.where(cnt_ge > k_f, 1.0, 0.0)) > 0.0
    x = lax.cond(any_excess, tie_search, lambda: jnp.full_like(t, n_keys))
    return t, x


def _params(n):
    return pltpu.CompilerParams(dimension_semantics=("arbitrary",) * n, vmem_limit_bytes=VMEM_LIMIT)


def _bucket_thresholds():
    exact = N_BUCKETS // 2
    n = np.arange(MAX_DISTANCE + 1)
    nf = np.maximum(n, 1).astype(np.float32)
    large = exact + (np.log(nf / np.float32(exact)) / np.float32(math.log(MAX_DISTANCE / exact))
                     * np.float32(N_BUCKETS - exact)).astype(np.int32)
    bucket = np.where(n < exact, n, np.minimum(large, N_BUCKETS - 1))
    assert bucket[MAX_DISTANCE] == N_BUCKETS - 1 and np.all(np.diff(bucket) >= 0)
    return [int(np.argmax(bucket >= k)) for k in range(N_BUCKETS)]


def _bias_tiles_kernel(tab_ref, o_ref):
    h = pl.program_id(0)
    thr = _bucket_thresholds()
    key = lax.broadcasted_iota(I32, (TQ, TQ), 0)
    qry = lax.broadcasted_iota(I32, (TQ, TQ), 1)
    for d in range(2):
        dist = d * TQ + qry - key
        val = jnp.full((TQ, TQ), tab_ref[0, h], F32)
        for k in range(1, N_BUCKETS):
            val = jnp.where(dist >= thr[k], tab_ref[k, h], val)
        o_ref[d] = (val - tab_ref[N_BUCKETS - 1, h]) * LOG2E


def _bias_tiles(table):
    n_h = table.shape[1]
    return pl.pallas_call(
        _bias_tiles_kernel,
        grid=(n_h,),
        in_specs=[pl.BlockSpec(memory_space=pltpu.SMEM)],
        out_specs=pl.BlockSpec((None, 2, TQ, TQ), lambda h: (h, 0, 0, 0)),
        out_shape=jax.ShapeDtypeStruct((n_h, 2, TQ, TQ), F32),
        compiler_params=_params(1), name="bias_tiles",
    )(table)


def _proj_prompt_kernel(x_ref, g_ref, wn_ref, wt_ref, gains_ref,
                        ga_ref, gb_ref, katm_ref, kbtm_ref, kktm_ref,
                        qat_ref, qbt_ref, qit_ref, wit_ref, vat_ref, vbt_ref,
                        ka32_ref, va32_ref, kb32_ref, vb32_ref, ki32_ref):
    h = _rms(x_ref[...], g_ref[...]).astype(BF16)
    tm = h.shape[0]
    scale = HEAD_DIM ** -0.5 * LOG2E

    def grp_t(c):
        return _dot_nt(wt_ref[c * D_ATT:(c + 1) * D_ATT, :], h)

    def headnorm_t(zt, row):
        z3 = zt.reshape(N_HEADS, HEAD_DIM, tm)
        ssq = jnp.sum(z3 * z3, axis=1, keepdims=True)
        z3 = z3 * lax.rsqrt(ssq * (1.0 / HEAD_DIM) + RMS_EPS)
        return z3.reshape(D_ATT, tm) * gains_ref[row]

    ga_ref[...] = _dot(h, wn_ref[:, :D_ATT])
    gb_ref[...] = _dot(h, wn_ref[:, D_ATT:])
    qat_ref[...] = (headnorm_t(grp_t(0), 0) * scale).astype(BF16)
    kat = headnorm_t(grp_t(1), 1)
    ka32_ref[...] = kat
    katm_ref[...] = kat.T.astype(BF16)
    vat = grp_t(2)
    va32_ref[...] = vat
    vat_ref[...] = vat.astype(BF16)
    qbt_ref[...] = (headnorm_t(grp_t(3), 2) * scale).astype(BF16)
    kbt = headnorm_t(grp_t(4), 3)
    kb32_ref[...] = kbt
    kbtm_ref[...] = kbt.T.astype(BF16)
    vbt = grp_t(5)
    vb32_ref[...] = vbt
    vbt_ref[...] = vbt.astype(BF16)
    qit_ref[...] = grp_t(6).astype(BF16)
    kk = _dot_nt(wt_ref[7 * D_ATT:7 * D_ATT + LANES, :], h)
    ki32_ref[...] = kk[:D_IDX, :]
    kktm_ref[...] = kk.T.astype(BF16)
    wit_ref[...] = _dot_nt(wt_ref[7 * D_ATT + LANES:, :], h)[:H_IDX, :]


def _project_prompt(x, g, wn, wt, gains_t, batch, seq):
    tm = TQ
    n_t = seq // tm
    n = batch * seq
    row = lambda c: pl.BlockSpec((tm, c), lambda b, t: (b * n_t + t, 0))
    full = lambda a: pl.BlockSpec(a.shape, lambda b, t: (0,) * a.ndim)
    tiles = lambda r: pl.BlockSpec((None, None, r, tm), lambda b, t: (b, t, 0, 0))
    cols = lambda r: pl.BlockSpec((None, r, tm), lambda b, t: (b, 0, t))
    sd = jax.ShapeDtypeStruct
    tsd = lambda r, dt: sd((batch, n_t, r, tm), dt)
    out_shape = ([sd((n, D_ATT), F32)] * 2 + [sd((n, D_ATT), BF16)] * 2 + [sd((n, LANES), BF16)]
                 + [tsd(D_ATT, BF16)] * 3 + [tsd(H_IDX, F32)] + [tsd(D_ATT, BF16)] * 2
                 + [sd((batch, D_ATT, seq), F32)] * 4 + [sd((batch, D_IDX, seq), F32)])
    out_specs = ([row(D_ATT)] * 4 + [row(LANES)] + [tiles(D_ATT)] * 3 + [tiles(H_IDX)] + [tiles(D_ATT)] * 2
                 + [cols(D_ATT)] * 4 + [cols(D_IDX)])
    return pl.pallas_call(
        _proj_prompt_kernel,
        grid=(batch, n_t),
        in_specs=[row(D_MODEL), full(g), full(wn), full(wt), full(gains_t)],
        out_specs=out_specs, out_shape=out_shape,
        compiler_params=_params(2), name="proj_prompt",
    )(x, g, wn, wt, gains_t)


def _headnorm_rows(z, ones_ref, gain):
    ssq = _dot((z * z).astype(BF16), ones_ref[...])
    return z * lax.rsqrt(ssq * (1.0 / HEAD_DIM) + RMS_EPS) * gain


def _proj_sample_kernel(x_ref, g_ref, w_ref, gains_ref, ones_ref,
                        qa_ref, ka_ref, va_ref, qb_ref, kb_ref, vb_ref, qi_ref, wi_ref,
                        ga_ref, gb_ref, ka32_ref, va32_ref, kb32_ref, vb32_ref, ki32_ref):
    h = _rms(x_ref[...], g_ref[...]).astype(BF16)
    scale = HEAD_DIM ** -0.5

    def grp(c):
        return _dot(h, w_ref[:, c * D_ATT:(c + 1) * D_ATT])

    qa_ref[...] = (_headnorm_rows(grp(0), ones_ref, gains_ref[0:1, :]) * scale).astype(BF16)
    ka = _headnorm_rows(grp(1), ones_ref, gains_ref[1:2, :])
    ka32_ref[...] = ka
    ka_ref[...] = ka.astype(BF16)
    va = grp(2)
    va32_ref[...] = va
    va_ref[...] = va.astype(BF16)
    ga_ref[...] = grp(3)
    qb_ref[...] = (_headnorm_rows(grp(4), ones_ref, gains_ref[2:3, :]) * scale).astype(BF16)
    kb = _headnorm_rows(grp(5), ones_ref, gains_ref[3:4, :])
    kb32_ref[...] = kb
    kb_ref[...] = kb.astype(BF16)
    vb = grp(6)
    vb32_ref[...] = vb
    vb_ref[...] = vb.astype(BF16)
    gb_ref[...] = grp(7)
    qi_ref[...] = grp(8).astype(BF16)
    tail = _dot(h, w_ref[:, 9 * D_ATT:])
    ki32_ref[...] = tail[:, :D_IDX]
    wi_ref[...] = tail[:, LANES:]


def _project_sample(x, g, w, gains, ones_bd):
    n = x.shape[0]
    full = lambda a: pl.BlockSpec(a.shape, lambda i: (0,) * a.ndim)
    sd = lambda c, dt: jax.ShapeDtypeStruct((n, c), dt)
    out_shape = ([sd(D_ATT, BF16)] * 7 + [sd(LANES, F32)] + [sd(D_ATT, F32)] * 6 + [sd(D_IDX, F32)])
    return pl.pallas_call(
        _proj_sample_kernel,
        grid=(1,),
        in_specs=[full(x), full(g), full(w), full(gains), full(ones_bd)],
        out_specs=[full(s) for s in out_shape], out_shape=out_shape,
        compiler_params=_params(1), name="proj_sample",
    )(x, g, w, gains, ones_bd)


def _head_row_masks():
    r = lax.broadcasted_iota(I32, (LANES, 1), 0)
    return r < HEAD_DIM, r >= HEAD_DIM


def _pair_attention(i, qms, k_tile, v_tile, near_bias, mask_own, mask_past):
    def update(tiles, carry):
        raw = lambda half, j, bias: _dot(k_tile(j), qms[half]) + (bias(half) if bias else 0.0)
        scores = [[mask(half, j, raw(half, j, bias)) for j, bias, mask in tiles] for half in range(2)]
        out = []
        for half in range(2):
            pvs = [lambda p, j=j, half=half: _dot(v_tile(j, half), p) for j, _, _ in tiles]
            out.append(_osm_update(scores[half], pvs, *carry[half], axis=0, exp=jnp.exp2))
        return tuple(out)

    init = (jnp.full((1, TQ), NEG_INF, F32), jnp.zeros((1, TQ), F32), jnp.zeros((HEAD_DIM, TQ), F32))
    tiles = [(i, lambda half: near_bias(half, 0), lambda half, j, s: mask_own(half, s))]
    for d in range(1, 2 + FIRST_FAR):
        exists = i >= d
        tiles.append((jnp.maximum(i - d, 0), (lambda half: near_bias(half, 1)) if d == 1 else None,
                      lambda half, j, s, exists=exists: mask_past(half, j, s, exists)))
    carry = update(tiles, (init, init))
    n_far = jnp.maximum(i - 1 - FIRST_FAR, 0)
    done = 0
    for width in (4, 2, 1):
        n_w = (n_far - done) // width
        carry = lax.fori_loop(
            0, n_w,
            lambda u, c, done=done, width=width: update(
                [(done + width * u + w, None, lambda half, j, s: mask_past(half, j, s, True))
                 for w in range(width)], c),
            carry)
        done = done + n_w * width
    return [c[2] / c[1] for c in carry]


def _moba_prompt_kernel(qt_ref, k_ref, vt_ref, bias_ref, o_ref, kmean_ref, *, n_blk):
    i = pl.program_id(2)
    n_sub = kmean_ref.shape[0]

    @pl.when(i == 0)
    def _():
        kmean_ref[...] = jnp.zeros_like(kmean_ref)
        for j in range(n_blk):
            kb = k_ref[j * TQ:(j + 1) * TQ, :].astype(F32)
            kmean_ref[j:j + 1, :] = jnp.mean(kb, axis=0, keepdims=True)

    masks = _head_row_masks()
    qt = qt_ref[...]
    key = lax.broadcasted_iota(I32, (TQ, TQ), 0)
    qry = lax.broadcasted_iota(I32, (TQ, TQ), 1)
    blk = lax.broadcasted_iota(I32, (n_sub, TQ), 0)
    kmean = kmean_ref[...].astype(BF16)
    qms = [jnp.where(masks[half], qt, jnp.zeros_like(qt)) for half in range(2)]
    picks = [_top_blocks(jnp.where(blk < i, _dot(kmean, qm), NEG_INF), blk, axis=0) for qm in qms]

    def picked(half, j, s, exists):
        sel = (picks[half][0] == j) | (picks[half][1] == j) | (picks[half][2] == j)
        return jnp.where(sel if exists is True else sel & exists, s, NEG_INF)

    outs = _pair_attention(
        i, qms,
        k_tile=lambda j: k_ref[pl.ds(pl.multiple_of(j * TQ, TQ), TQ), :],
        v_tile=lambda j, half: vt_ref[j, half * HEAD_DIM:(half + 1) * HEAD_DIM, :],
        near_bias=lambda half, d: bias_ref[half, d],
        mask_own=lambda half, s: jnp.where(key <= qry, s, NEG_INF),
        mask_past=picked)
    o_ref[...] = jnp.concatenate(outs, axis=0).T


def _moba_prompt(qt, k, vt, bias, batch, seq):
    n_q = seq // TQ
    n_sub = -(-n_q // SUBLANES) * SUBLANES
    kern = functools.partial(_moba_prompt_kernel, n_blk=n_q)
    return pl.pallas_call(
        kern,
        grid=(batch, N_HEADS // 2, n_q),
        in_specs=[
            pl.BlockSpec((None, None, LANES, TQ), lambda b, hp, i: (b, i, hp, 0)),
            pl.BlockSpec((seq, LANES), lambda b, hp, i: (b, hp)),
            pl.BlockSpec((None, n_q, LANES, TQ), lambda b, hp, i: (b, 0, hp, 0)),
            pl.BlockSpec((2, 2, TQ, TQ), lambda b, hp, i: (hp, 0, 0, 0)),
        ],
        out_specs=pl.BlockSpec((TQ, LANES), lambda b, hp, i: (b * n_q + i, hp)),
        out_shape=jax.ShapeDtypeStruct((batch * seq, D_ATT), F32),
        scratch_shapes=[pltpu.VMEM((n_sub, LANES), F32)],
        compiler_params=_params(3), name="moba_prompt",
    )(qt, k, vt, bias)


def _dsa_prompt_kernel(qt_ref, qit_ref, wit_ref, kk_ref, k_ref, vt_ref, bias_ref, o_ref,
                       keys_ref, pen_ref, *, seq, k_top):
    i = pl.program_id(1)
    masks = _head_row_masks()
    key = lax.broadcasted_iota(I32, (TQ, TQ), 0)
    qry = lax.broadcasted_iota(I32, (TQ, TQ), 1)
    zero_b = jnp.zeros((LANES, TQ), BF16)

    def head_qt(ref, h):
        pair, half = divmod(h, 2)
        return jnp.where(masks[half], ref[pair * LANES:(pair + 1) * LANES, :], zero_b)

    def index_tile(j):
        kkj = kk_ref[pl.ds(pl.multiple_of(j * TQ, TQ), TQ), :]
        acc = jnp.zeros((TQ, TQ), F32)
        for h in range(H_IDX):
            acc = acc + wit_ref[h:h + 1, :] * jnp.maximum(_dot(kkj, head_qt(qit_ref, h)), 0.0)
        return _f32_key(acc)

    def index_pair(u, c):
        keys_ref[2 * u] = index_tile(2 * u)
        keys_ref[2 * u + 1] = index_tile(2 * u + 1)
        return c

    def index_one(j, c):
        keys_ref[j] = index_tile(j)
        return c

    lax.fori_loop(0, i // 2, index_pair, 0)
    lax.fori_loop(2 * (i // 2), i, index_one, 0)
    keys_ref[i] = jnp.where(key <= qry, index_tile(i), INT_MIN)

    def count_fn(pred):
        def one(j):
            w = jnp.where(pred(keys_ref[j], key + j * TQ), 1.0, 0.0)
            return jnp.sum(w.reshape(TQ // SUBLANES, SUBLANES, TQ), axis=0)
        n_two = (i + 1) // 2
        part = lax.fori_loop(0, n_two, lambda u, p: p + one(2 * u) + one(2 * u + 1),
                             jnp.zeros((SUBLANES, TQ), F32))
        part = lax.fori_loop(2 * n_two, i + 1, lambda j, p: p + one(j), part)
        return jnp.sum(part, axis=0, keepdims=True)

    t, x = _select_threshold(count_fn, k_top, seq)

    def pen_body(j, c):
        kt = keys_ref[j]
        sel = (kt > t) | ((kt == t) & (key + j * TQ <= x))
        pen_ref[j] = jnp.where(sel, 0.0, NEG_INF)
        return c

    lax.fori_loop(0, i + 1, pen_body, 0)
    outs = []
    for pair in range(N_HEADS // 2):
        lanes = slice(pair * LANES, (pair + 1) * LANES)
        vrows = lambda half, pair=pair: slice((2 * pair + half) * HEAD_DIM, (2 * pair + half + 1) * HEAD_DIM)
        outs += _pair_attention(
            i, [head_qt(qt_ref, 2 * pair + half) for half in range(2)],
            k_tile=lambda j, lanes=lanes: k_ref[pl.ds(pl.multiple_of(j * TQ, TQ), TQ), lanes],
            v_tile=lambda j, half, vrows=vrows: vt_ref[j, vrows(half), :],
            near_bias=lambda half, d, pair=pair: bias_ref[2 * pair + half, d],
            mask_own=lambda half, s: s + pen_ref[i],
            mask_past=lambda half, j, s, exists: (
                s + pen_ref[j] if exists is True else jnp.where(exists, s + pen_ref[j], NEG_INF)))
    o_ref[...] = jnp.concatenate(outs, axis=0).T


def _dsa_prompt(qt, qit, wit, kk, k, vt, bias, batch, seq):
    n_q = seq // TQ
    k_top = min(DSA_TOPK_MAX, seq // 4)
    kern = functools.partial(_dsa_prompt_kernel, seq=seq, k_top=k_top)
    tile = lambda r: pl.BlockSpec((None, None, r, TQ), lambda b, i: (b, i, 0, 0))
    rows = lambda c: pl.BlockSpec((seq, c), lambda b, i: (b, 0))
    return pl.pallas_call(
        kern,
        grid=(batch, n_q),
        in_specs=[
            tile(D_ATT), tile(D_ATT), tile(H_IDX),
            rows(LANES), rows(D_ATT),
            pl.BlockSpec((None, n_q, D_ATT, TQ), lambda b, i: (b, 0, 0, 0)),
            pl.BlockSpec((N_HEADS, 2, TQ, TQ), lambda b, i: (1, 0, 0, 0)),
        ],
        out_specs=pl.BlockSpec((TQ, D_ATT), lambda b, i: (b * n_q + i, 0)),
        out_shape=jax.ShapeDtypeStruct((batch * seq, D_ATT), F32),
        scratch_shapes=[pltpu.VMEM((n_q, TQ, TQ), I32), pltpu.VMEM((n_q, TQ, TQ), F32)],
        compiler_params=_params(2), name="dsa_prompt",
    )(qt, qit, wit, kk, k, vt, bias)


def _expand_tokens(a, n_tok):
    return jnp.broadcast_to(a[:, None, :], (n_tok, N_HEADS, a.shape[-1])).reshape(n_tok * N_HEADS, a.shape[-1])


def _collapse_heads(o, hmask, n_tok):
    return jnp.sum((o * hmask).reshape(n_tok, N_HEADS, D_ATT), axis=1)


def _sample_index_kernel(pt_ref, *refs, n_pages, n_tok, k_top, n_pp):
    kit_refs = refs[:n_pp]
    qi_ref, wcol_ref, knew_ref, pen_ref, keys_ref = refs[n_pp:]
    g = pl.program_id(1)

    def keys_of(kt):
        x = jnp.maximum(_dot(qi_ref[...], kt), 0.0) * wcol_ref[...]
        return _f32_key(jnp.sum(x.reshape(n_tok, H_IDX, kt.shape[1]), axis=1))

    keys = keys_of(jnp.concatenate([r[...] for r in kit_refs], axis=1).astype(BF16))
    for o in range(n_pp):
        keys_ref[g * n_pp + o] = keys[:, o * LANES:(o + 1) * LANES]

    @pl.when(g == n_pages // n_pp - 1)
    def _():
        lane = lax.broadcasted_iota(I32, (n_tok, LANES), 1)
        tok = lax.broadcasted_iota(I32, (n_tok, LANES), 0)
        keys_ref[n_pages] = jnp.where(lane <= tok, keys_of(knew_ref[...]), INT_MIN)
        shape = (n_pages + 1, n_tok, LANES)
        pos = lax.broadcasted_iota(I32, shape, 0) * LANES + lax.broadcasted_iota(I32, shape, 2)
        all_keys = keys_ref[...]

        def count_fn(pred):
            w = jnp.where(pred(all_keys, pos), 1.0, 0.0)
            terms = [w[j] for j in range(n_pages + 1)]
            while len(terms) > 1:
                terms = [a + b for a, b in zip(terms[::2], terms[1::2])] + terms[len(terms) & ~1:]
            return jnp.sum(terms[0], axis=1, keepdims=True)

        t, x = _select_threshold(count_fn, k_top, (n_pages + 1) * LANES)
        sel = (all_keys > t) | ((all_keys == t) & (pos <= x))
        pen_ref[...] = jnp.where(sel, 0.0, NEG_INF)


def _sample_index(page_table, kit, qi, wcol, knew, n_tok, k_top):
    n_seq, n_pages = page_table.shape
    n_pp = math.gcd(IDX_PAGES_PER_STEP, n_pages)
    kern = functools.partial(_sample_index_kernel, n_pages=n_pages, n_tok=n_tok, k_top=k_top, n_pp=n_pp)
    per_seq = lambda a: pl.BlockSpec((None,) + a.shape[1:], lambda b, g, pt: (b,) + (0,) * (a.ndim - 1))
    page = lambda o: pl.BlockSpec((None, D_IDX, PAGE_SIZE), lambda b, g, pt: (pt[b, g * n_pp + o], 0, 0))
    grid_spec = pltpu.PrefetchScalarGridSpec(
        num_scalar_prefetch=1, grid=(n_seq, n_pages // n_pp),
        in_specs=[page(o) for o in range(n_pp)] + [per_seq(qi), per_seq(wcol), per_seq(knew)],
        out_specs=pl.BlockSpec((None, n_pages + 1, n_tok, LANES), lambda b, g, pt: (b, 0, 0, 0)),
        scratch_shapes=[pltpu.VMEM((n_pages + 1, n_tok, LANES), I32)])
    return pl.pallas_call(
        kern, grid_spec=grid_spec,
        out_shape=jax.ShapeDtypeStruct((n_seq, n_pages + 1, n_tok, LANES), F32),
        compiler_params=_params(2), name="sample_index",
    )(page_table, *([kit] * n_pp), qi, wcol, knew)


def _sample_dsa_kernel(pt_ref, *refs, n_pages, n_tok, n_pp):
    kt_refs, vt_refs = refs[:n_pp], refs[n_pp:2 * n_pp]
    (q_ref, pen_ref, blast_ref, c31_ref, knew_ref, vnew_ref, bnew_ref, hmask_ref,
     o_ref, m_ref, l_ref, acc_ref) = refs[2 * n_pp:]
    g = pl.program_id(1)
    last = g == n_pages // n_pp - 1

    @pl.when(g == 0)
    def _():
        m_ref[...] = jnp.full(m_ref.shape, NEG_INF, F32)
        l_ref[...] = jnp.zeros(l_ref.shape, F32)
        acc_ref[...] = jnp.zeros(acc_ref.shape, F32)

    q = q_ref[...]
    scores, pvs = [], []
    for o in range(n_pp):
        bias = c31_ref[...] if o < n_pp - 1 else jnp.where(last, blast_ref[...], c31_ref[...])
        scores.append(_dot(q, kt_refs[o][...].astype(BF16)) + bias + _expand_tokens(pen_ref[g * n_pp + o], n_tok))
        pvs.append(lambda pp, o=o: _dot_nt(pp, vt_refs[o][...].astype(BF16)))
    m, l, acc = _osm_update(scores, pvs, m_ref[...], l_ref[...], acc_ref[...], axis=1)
    m_ref[...] = m
    l_ref[...] = l
    acc_ref[...] = acc

    @pl.when(last)
    def _():
        pen_new = _expand_tokens(pen_ref[n_pages][:, :NEW_PAD], n_tok)
        s_n = _dot_nt(q, knew_ref[...]) + bnew_ref[...] + pen_new
        _, l2, acc2 = _osm_update([s_n], [lambda pp: _dot(pp, vnew_ref[...])], m, l, acc, axis=1)
        o_ref[...] = _collapse_heads(acc2 / l2, hmask_ref[...], n_tok)


def _sample_dsa(page_table, kt, vt, q, pen, blast, c31col, knew, vnew, bnew, hmask, n_tok):
    n_seq, n_pages = page_table.shape
    n_pp = math.gcd(PAGES_PER_STEP, n_pages)
    rows = n_tok * N_HEADS
    kern = functools.partial(_sample_dsa_kernel, n_pages=n_pages, n_tok=n_tok, n_pp=n_pp)
    per_seq = lambda a: pl.BlockSpec((None,) + a.shape[1:], lambda b, g, pt: (b,) + (0,) * (a.ndim - 1))
    full = lambda a: pl.BlockSpec(a.shape, lambda b, g, pt: (0,) * a.ndim)
    page = lambda o: pl.BlockSpec((None, D_ATT, PAGE_SIZE), lambda b, g, pt: (pt[b, g * n_pp + o], 0, 0))
    pages = [page(o) for o in range(n_pp)]
    grid_spec = pltpu.PrefetchScalarGridSpec(
        num_scalar_prefetch=1, grid=(n_seq, n_pages // n_pp),
        in_specs=pages + pages + [per_seq(q), per_seq(pen), full(blast), full(c31col),
                                  per_seq(knew), per_seq(vnew), full(bnew), full(hmask)],
        out_specs=pl.BlockSpec((None, n_tok, D_ATT), lambda b, g, pt: (b, 0, 0)),
        scratch_shapes=[pltpu.VMEM((rows, 1), F32), pltpu.VMEM((rows, 1), F32), pltpu.VMEM((rows, D_ATT), F32)])
    return pl.pallas_call(
        kern, grid_spec=grid_spec,
        out_shape=jax.ShapeDtypeStruct((n_seq, n_tok, D_ATT), F32),
        compiler_params=_params(2), name="sample_dsa",
    )(page_table, *([kt] * n_pp), *([vt] * n_pp), q, pen, blast, c31col, knew, vnew, bnew, hmask)


def _sample_moba_kernel(pt_ref, *refs, n_blk, n_tok, n_pp):
    kt_refs, vt_refs = refs[:n_pp], refs[n_pp:2 * n_pp]
    (q_ref, blast_ref, c31_ref, knew_ref, vnew_ref, bnew_ref, hmask_ref,
     o_ref, gate_ref, ms_ref, ls_ref, accs_ref) = refs[2 * n_pp:]
    g = pl.program_id(1)
    n_bb = n_pp // 2
    last = g == n_blk // n_bb - 1
    rows = n_tok * N_HEADS
    lane = lax.broadcasted_iota(I32, (rows, LANES), 1)

    @pl.when(g == 0)
    def _():
        gate_ref[...] = jnp.full(gate_ref.shape, NEG_INF, F32)
        ms_ref[...] = jnp.full(ms_ref.shape, NEG_INF, F32)
        ls_ref[...] = jnp.zeros(ls_ref.shape, F32)

    q = q_ref[...]
    gates, ms, ls = gate_ref[...], ms_ref[...], ls_ref[...]
    raw = [_dot(q, r[...].astype(BF16)) for r in kt_refs]
    parts = []
    for bb in range(n_bb):
        r0, r1 = raw[2 * bb], raw[2 * bb + 1]
        gate = jnp.sum(r0 + r1, axis=1, keepdims=True)
        s0 = r0 + c31_ref[...]
        s1 = r1 + (c31_ref[...] if bb < n_bb - 1 else jnp.where(last, blast_ref[...], c31_ref[...]))
        m = jnp.maximum(jnp.max(s0, axis=1, keepdims=True), jnp.max(s1, axis=1, keepdims=True))
        p0 = jnp.exp(s0 - m)
        p1 = jnp.exp(s1 - m)
        l = jnp.sum(p0, axis=1, keepdims=True) + jnp.sum(p1, axis=1, keepdims=True)
        parts.append((gate, m, l, p0.astype(BF16), p1.astype(BF16)))
    for bb, (gate, m, l, p0, p1) in enumerate(parts):
        j = g * n_bb + bb
        accs_ref[j] = (_dot_nt(p0, vt_refs[2 * bb][...].astype(BF16))
                       + _dot_nt(p1, vt_refs[2 * bb + 1][...].astype(BF16)))
        gates = jnp.where(lane == j, gate, gates)
        ms = jnp.where(lane == j, m, ms)
        ls = jnp.where(lane == j, l, ls)
    gate_ref[...] = gates
    ms_ref[...] = ms
    ls_ref[...] = ls

    @pl.when(last)
    def _():
        picks = _top_blocks(jnp.where(lane < n_blk, gates, NEG_INF), lane, axis=1)
        sel = (lane == picks[0]) | (lane == picks[1]) | (lane == picks[2])
        s_n = _dot_nt(q, knew_ref[...]) + bnew_ref[...]
        m_o = jnp.max(s_n, axis=1, keepdims=True)
        p_o = jnp.exp(s_n - m_o)
        l_o = jnp.sum(p_o, axis=1, keepdims=True)
        m_all = jnp.maximum(m_o, jnp.max(jnp.where(sel, ms, NEG_INF), axis=1, keepdims=True))
        w = jnp.where(sel, jnp.exp(ms - m_all), 0.0)
        w_o = jnp.exp(m_o - m_all)
        l_all = jnp.sum(w * ls, axis=1, keepdims=True) + w_o * l_o
        out = w_o * _dot(p_o.astype(BF16), vnew_ref[...])
        for jj in range(n_blk):
            out = out + w[:, jj:jj + 1] * accs_ref[jj]
        o_ref[...] = _collapse_heads(out / l_all, hmask_ref[...], n_tok)


def _sample_moba(page_table, kt, vt, q, blast, c31col, knew, vnew, bnew, hmask, n_tok):
    n_seq, n_pages = page_table.shape
    pages_per_blk = MOBA_BLOCK // PAGE_SIZE
    n_blk = n_pages // pages_per_blk
    n_pp = pages_per_blk * math.gcd(PAGES_PER_STEP // pages_per_blk, n_blk)
    rows = n_tok * N_HEADS
    kern = functools.partial(_sample_moba_kernel, n_blk=n_blk, n_tok=n_tok, n_pp=n_pp)
    per_seq = lambda a: pl.BlockSpec((None,) + a.shape[1:], lambda b, g, pt: (b,) + (0,) * (a.ndim - 1))
    full = lambda a: pl.BlockSpec(a.shape, lambda b, g, pt: (0,) * a.ndim)
    page = lambda o: pl.BlockSpec((None, D_ATT, PAGE_SIZE), lambda b, g, pt: (pt[b, g * n_pp + o], 0, 0))
    pages = [page(o) for o in range(n_pp)]
    grid_spec = pltpu.PrefetchScalarGridSpec(
        num_scalar_prefetch=1, grid=(n_seq, n_pages // n_pp),
        in_specs=pages + pages + [per_seq(q), full(blast), full(c31col),
                                  per_seq(knew), per_seq(vnew), full(bnew), full(hmask)],
        out_specs=pl.BlockSpec((None, n_tok, D_ATT), lambda b, g, pt: (b, 0, 0)),
        scratch_shapes=[pltpu.VMEM((rows, LANES), F32)] * 3 + [pltpu.VMEM((n_blk, rows, D_ATT), F32)])
    return pl.pallas_call(
        kern, grid_spec=grid_spec,
        out_shape=jax.ShapeDtypeStruct((n_seq, n_tok, D_ATT), F32),
        compiler_params=_params(2), name="sample_moba",
    )(page_table, *([kt] * n_pp), *([vt] * n_pp), q, blast, c31col, knew, vnew, bnew, hmask)


def _merge_and_conv_in(x_ref, oa_ref, ob_ref, ga_ref, gb_ref, wo_ref, g1_ref, wc_ref):
    o = jnp.concatenate([oa_ref[...] * _silu(ga_ref[...]), ob_ref[...] * _silu(gb_ref[...])], axis=1)
    h1 = x_ref[...] + _dot(o.astype(BF16), wo_ref[...])
    hn = _rms(h1, g1_ref[...]).astype(BF16)
    a = _dot(hn, wc_ref[:, 0:D_MODEL])
    b = _dot(hn, wc_ref[:, D_MODEL:2 * D_MODEL])
    g = _dot(hn, wc_ref[:, 2 * D_MODEL:3 * D_MODEL])
    return h1, a * _sigmoid(b), g


def _ln_gate_out(c, g, h1, lg_ref, lb_ref, wo2_ref):
    mu = jnp.mean(c, axis=-1, keepdims=True)
    d = c - mu
    var = jnp.mean(d * d, axis=-1, keepdims=True)
    cn = d * lax.rsqrt(var + LN_EPS) * lg_ref[...] + lb_ref[...]
    z = (_silu(cn) * _silu(g)).astype(BF16)
    return h1 + _dot(z, wo2_ref[...])


def _layer1_prompt_kernel(x_ref, oa_ref, ob_ref, ga_ref, gb_ref, wo_ref, g1_ref, wc_ref,
                          cw_ref, cb_ref, lg_ref, lb_ref, wo2_ref, y_ref, st_ref, up_ref, c_ref, *, tt):
    t = pl.program_id(1)

    @pl.when(t == 0)
    def _():
        up_ref[0:HIST, :] = jnp.zeros((HIST, D_MODEL), F32)

    h1, u, g = _merge_and_conv_in(x_ref, oa_ref, ob_ref, ga_ref, gb_ref, wo_ref, g1_ref, wc_ref)
    up_ref[HIST:HIST + tt, :] = u
    base = HIST - (CONV_WIDTH - 1)
    rows = min(tt, CONV_ROWS)
    for cc in range(0, D_MODEL, LANES):
        cs = slice(cc, cc + LANES)
        for r0 in range(0, tt, rows):
            acc = jnp.zeros((rows, LANES), F32) + cb_ref[:, cs]
            for phase in range(SUBLANES):
                taps = [w for w in range(CONV_WIDTH) if (base + w) % SUBLANES == phase]
                n_win = rows + (SUBLANES if phase else 0)
                group = None
                for w in taps:
                    start = r0 + base + w - phase
                    term = cw_ref[w:w + 1, cs] * up_ref[start:start + n_win, cs]
                    group = term if group is None else group + term
                if group is not None:
                    acc = acc + group[phase:phase + rows, :]
            c_ref[r0:r0 + rows, cs] = acc
    y_ref[...] = _ln_gate_out(c_ref[...], g, h1, lg_ref, lb_ref, wo2_ref)
    tail = up_ref[tt:tt + HIST, :]
    st_ref[...] = tail
    up_ref[0:HIST, :] = tail


def _layer1_prompt(x, oa, ob, ga, gb, wo, g1, wc, cw, cb, lg, lb, wo2, batch, seq, tt):
    n_t = seq // tt
    kern = functools.partial(_layer1_prompt_kernel, tt=tt)
    row = lambda c: pl.BlockSpec((tt, c), lambda b, t: (b * n_t + t, 0))
    full = lambda a: pl.BlockSpec(a.shape, lambda b, t: (0,) * a.ndim)
    return pl.pallas_call(
        kern,
        grid=(batch, n_t),
        in_specs=[row(D_MODEL), row(D_ATT), row(D_ATT), row(D_ATT), row(D_ATT),
                  full(wo), full(g1), full(wc), full(cw), full(cb), full(lg), full(lb), full(wo2)],
        out_specs=[row(D_MODEL), pl.BlockSpec((None, HIST, D_MODEL), lambda b, t: (b, 0, 0))],
        out_shape=[jax.ShapeDtypeStruct((batch * seq, D_MODEL), F32),
                   jax.ShapeDtypeStruct((batch, HIST, D_MODEL), F32)],
        scratch_shapes=[pltpu.VMEM((HIST + tt, D_MODEL), F32), pltpu.VMEM((tt, D_MODEL), F32)],
        compiler_params=_params(2), name="layer1_prompt",
    )(x, oa, ob, ga, gb, wo, g1, wc, cw, cb, lg, lb, wo2)


def _layer1_sample_kernel(x_ref, oa_ref, ob_ref, ga_ref, gb_ref, wo_ref, g1_ref, wc_ref,
                          cw_ref, cb_ref, lg_ref, lb_ref, wo2_ref, st_ref, y_ref, ns_ref, c_ref, *, n_tok, n_seq):
    n_hist = CONV_WIDTH - 1
    h1, u, g = _merge_and_conv_in(x_ref, oa_ref, ob_ref, ga_ref, gb_ref, wo_ref, g1_ref, wc_ref)
    u_t = [u[t * n_seq:(t + 1) * n_seq, :] for t in range(n_tok)]
    for t in range(n_tok):
        acc = jnp.zeros((n_seq, D_MODEL), F32) + cb_ref[...]
        for r in range(t, n_hist):
            acc = acc + cw_ref[r - t:r - t + 1, :] * st_ref[r]
        for r in range(t + 1):
            acc = acc + cw_ref[n_hist - t + r:n_hist - t + r + 1, :] * u_t[r]
        c_ref[t * n_seq:(t + 1) * n_seq, :] = acc
    y_ref[...] = _ln_gate_out(c_ref[...], g, h1, lg_ref, lb_ref, wo2_ref)
    for r in range(n_hist - n_tok):
        ns_ref[r] = st_ref[r + n_tok]
    for t in range(n_tok):
        ns_ref[n_hist - n_tok + t] = u_t[t]


def _layer1_sample(x, oa, ob, ga, gb, wo, g1, wc, cw, cb, lg, lb, wo2, st, n_tok, n_seq):
    kern = functools.partial(_layer1_sample_kernel, n_tok=n_tok, n_seq=n_seq)
    full = lambda a: pl.BlockSpec(a.shape, lambda i: (0,) * a.ndim)
    args = (x, oa, ob, ga, gb, wo, g1, wc, cw, cb, lg, lb, wo2, st)
    out_shape = [jax.ShapeDtypeStruct(x.shape, F32), jax.ShapeDtypeStruct(st.shape, F32)]
    return pl.pallas_call(
        kern, grid=(1,),
        in_specs=[full(a) for a in args],
        out_specs=[full(s) for s in out_shape], out_shape=out_shape,
        scratch_shapes=[pltpu.VMEM(x.shape, F32)],
        compiler_params=_params(1), name="layer1_sample",
    )(*args)


def _t5_bucket(n):
    exact = N_BUCKETS // 2
    nf = jnp.maximum(n, 1).astype(F32)
    large = exact + (jnp.log(nf / exact) / math.log(MAX_DISTANCE / exact) * (N_BUCKETS - exact)).astype(I32)
    large = jnp.minimum(large, N_BUCKETS - 1)
    return jnp.where(n < exact, n, large)


def _bias_by_distance(table, n):
    return table[_t5_bucket(jnp.arange(n, dtype=I32))]


def _prep_attn_weights(w_in, qn_a, kn_a, qn_b, kn_b):
    grp = lambda c: w_in[:, c * D_ATT:(c + 1) * D_ATT]
    ki = w_in[:, 9 * D_ATT:9 * D_ATT + D_IDX]
    wi = w_in[:, 9 * D_ATT + D_IDX:]
    zeros = lambda c: jnp.zeros((D_MODEL, c), w_in.dtype)
    tile = lambda g: jnp.tile(g, N_HEADS)
    head = np.arange(D_ATT) // HEAD_DIM
    ones_bd = jnp.asarray(head[:, None] == head[None, :], dtype=BF16)
    wn = jnp.concatenate([grp(3), grp(7)], axis=1).astype(BF16)
    wt = jnp.concatenate([grp(0), grp(1), grp(2), grp(4), grp(5), grp(6), grp(8), ki, ki, wi,
                          zeros(NEW_PAD - H_IDX)], axis=1).T.astype(BF16)
    gains_t = jnp.stack([tile(qn_a), tile(kn_a), tile(qn_b), tile(kn_b)]).astype(F32)[:, :, None]
    ws = jnp.concatenate([w_in[:, :9 * D_ATT + D_IDX], zeros(LANES - D_IDX), wi, zeros(LANES - H_IDX)],
                         axis=1).astype(BF16)
    gains_s = jnp.stack([tile(qn_a), tile(kn_a), tile(qn_b), tile(kn_b)]).astype(F32)
    return wn, wt, gains_t, ws, gains_s, ones_bd


def kernel(x_prompt, x_sample, cache_k_a, cache_v_a, cache_k_b, cache_v_b, cache_kidx_b, state_conv, page_table,
           norm_g, rel_bias_table, w_in_attn, q_norm_a, k_norm_a, q_norm_b, k_norm_b, w_out_attn,
           w_in_conv, conv_w, conv_b, conv_ln_g, conv_ln_b, w_out_conv):
    batch, seq, _ = x_prompt.shape
    n_seq, n_tok, _ = x_sample.shape
    n_pages = page_table.shape[1]
    past = n_pages * PAGE_SIZE
    assert seq % TQ == 0 and past % MOBA_BLOCK == 0 and n_tok <= NEW_PAD and TQ >= MAX_DISTANCE
    n_hist = CONV_WIDTH - 1
    wn, wt, gains_t, ws, gains_s, ones_bd = _prep_attn_weights(
        w_in_attn[0], q_norm_a[0], k_norm_a[0], q_norm_b[0], k_norm_b[0])
    g0 = norm_g[0][None, :]
    g1 = norm_g[1][None, :]
    wo = w_out_attn[0].astype(BF16)
    wc = w_in_conv[0].astype(BF16)
    wo2 = w_out_conv[0].astype(BF16)
    cw = jnp.concatenate([conv_w[0], jnp.zeros((HIST - CONV_WIDTH, D_MODEL), F32)], axis=0)
    conv_rows = (cw, conv_b[0][None, :], conv_ln_g[0][None, :], conv_ln_b[0][None, :])
    c31 = rel_bias_table[N_BUCKETS - 1]
    bias_tiles = _bias_tiles(rel_bias_table)

    xp = x_prompt.reshape(batch * seq, D_MODEL)
    (ga, gb, katm, kbtm, kktm, qat, qbt, qit, wit, vat, vbt, ka32, va32, kb32, vb32, ki32) = _project_prompt(
        xp, g0, wn, wt, gains_t, batch, seq)
    oa = _moba_prompt(qat, katm, vat, bias_tiles, batch, seq)
    ob = _dsa_prompt(qbt, qit, wit, kktm, kbtm, vbt, bias_tiles, batch, seq)
    y_p, st_p = _layer1_prompt(xp, oa, ob, ga, gb, wo, g1, wc, *conv_rows, wo2, batch, seq, tt=TQ)
    heads_p = lambda a: jnp.transpose(a.reshape(batch, N_HEADS, HEAD_DIM, seq), (0, 3, 1, 2))[None]
    ki_p = jnp.transpose(ki32, (0, 2, 1))[None]
    cv_p = st_p[None, :, HIST - n_hist:, :]

    rows = n_tok * N_HEADS
    xs = x_sample.reshape(n_seq * n_tok, D_MODEL)
    (qa_s, ka_s, va_s, qb_s, kb_s, vb_s, qi_s, wi_s, ga_s, gb_s, ka32_s, va32_s, kb32_s, vb32_s, ki32_s) = \
        _project_sample(xs, g0, ws, gains_s, ones_bd)
    head_of_lane = np.arange(D_ATT) // HEAD_DIM
    hmask8 = (np.arange(N_HEADS)[:, None] == head_of_lane[None, :])
    hmask = jnp.asarray(np.tile(hmask8, (n_tok, 1)), F32)
    qbd = lambda q: (q.reshape(n_seq, n_tok, 1, D_ATT) * jnp.asarray(hmask8, BF16)).reshape(n_seq, rows, D_ATT)
    newpad = lambda a: jnp.pad(a.reshape(n_seq, n_tok, D_ATT), ((0, 0), (0, NEW_PAD - n_tok), (0, 0)))
    bd = _bias_by_distance(rel_bias_table, 2 * PAGE_SIZE)
    t_of_row = np.arange(rows) // N_HEADS
    h_of_row = np.arange(rows) % N_HEADS
    c_idx = np.arange(PAGE_SIZE)
    dist_last = PAGE_SIZE + t_of_row[:, None] - c_idx[None, :]
    tn = np.arange(NEW_PAD)
    dist_new = np.maximum(t_of_row[:, None] - tn[None, :], 0)
    valid_new = (tn[None, :] <= t_of_row[:, None]) & (tn[None, :] < n_tok)

    def sample_bias(off):
        hh = h_of_row + off
        blast = bd[dist_last, hh[:, None]]
        bnew = jnp.where(valid_new, bd[dist_new, hh[:, None]], NEG_INF)
        return blast, c31[hh][:, None], bnew

    pages = lambda c: jnp.transpose(c[0], (0, 2, 3, 1)).reshape(c.shape[1], D_ATT, PAGE_SIZE)
    kit = jnp.transpose(cache_kidx_b[0], (0, 2, 1))
    blast_a, c31_a, bnew_a = sample_bias(0)
    oa_s = _sample_moba(page_table, pages(cache_k_a), pages(cache_v_a), qbd(qa_s), blast_a, c31_a,
                        newpad(ka_s), newpad(va_s), bnew_a, hmask, n_tok)
    k_top = min(DSA_TOPK_MAX, (past + n_tok) // 4)
    knew_i = jnp.pad(jnp.transpose(ki32_s.reshape(n_seq, n_tok, D_IDX), (0, 2, 1)),
                     ((0, 0), (0, 0), (0, PAGE_SIZE - n_tok))).astype(BF16)
    pen = _sample_index(page_table, kit, qi_s.reshape(n_seq, rows, D_IDX),
                        wi_s[:, :H_IDX].reshape(n_seq, rows, 1), knew_i, n_tok, k_top)
    blast_b, c31_b, bnew_b = sample_bias(N_HEADS)
    ob_s = _sample_dsa(page_table, pages(cache_k_b), pages(cache_v_b), qbd(qb_s), pen, blast_b, c31_b,
                       newpad(kb_s), newpad(vb_s), bnew_b, hmask, n_tok)
    tmaj = lambda a: jnp.transpose(a.reshape(n_seq, n_tok, -1), (1, 0, 2)).reshape(n_tok * n_seq, -1)
    st_in = jnp.transpose(state_conv[0], (1, 0, 2))
    y_s, ns = _layer1_sample(tmaj(xs), tmaj(oa_s), tmaj(ob_s), tmaj(ga_s), tmaj(gb_s), wo, g1, wc,
                             *conv_rows, wo2, st_in, n_tok, n_seq)
    y_s = jnp.transpose(y_s.reshape(n_tok, n_seq, D_MODEL), (1, 0, 2))
    cv_s = jnp.transpose(ns, (1, 0, 2))[None]
    heads_s = lambda a: a.reshape(1, n_seq, n_tok, N_HEADS, HEAD_DIM)

    return (y_p.reshape(batch, seq, D_MODEL), y_s,
            heads_p(ka32), heads_p(va32), heads_p(kb32), heads_p(vb32), ki_p, cv_p,
            heads_s(ka32_s), heads_s(va32_s), heads_s(kb32_s), heads_s(vb32_s),
            ki32_s.reshape(1, n_seq, n_tok, D_IDX), cv_s)
```
